```python
import functools
import jax, jax.numpy as jnp
from jax import lax
import numpy as np

D_MODEL = 4096
BATCH = 4
SEQ = 2048
DEPTH = 1
DEC_BATCH = 128
DEC_SEQ = 4
PAST_LEN = 16384
PAGE_SIZE = 128

D_MIX = D_MODEL
QK_NOPE_DIM = 128
QK_ROPE_DIM = 64
V_HEAD_DIM = 128
MLA_HEADS = (D_MIX // 2) // V_HEAD_DIM
MLA_DIM = MLA_HEADS * V_HEAD_DIM
Q_LORA_RANK = 896
KV_LORA_RANK = 512
RWKV_HEAD_DIM = 64
RWKV_DIM = D_MIX - MLA_DIM
RWKV_HEADS = RWKV_DIM // RWKV_HEAD_DIM
DECAY_LORA = 128
ICLR_LORA = 128
GATE_LORA = 256
W_MLA_IN = Q_LORA_RANK + KV_LORA_RANK + QK_ROPE_DIM
W_RWKV_IN = 3 * RWKV_DIM + DECAY_LORA + ICLR_LORA + GATE_LORA
IN_COLS = W_MLA_IN + W_RWKV_IN
D_FF = 11008
N_ADA = 9
Q_BLOCK = 128
ROPE_THETA = 10000.0
SM_SCALE = (QK_NOPE_DIM + QK_ROPE_DIM) ** -0.5
ALPHA = (2 * DEPTH) ** 0.25
BETA = (8 * DEPTH) ** -0.25
LN_EPS = 1e-5
RMS_EPS = 1e-6
GN_EPS = 64e-5

kernel_name = "hymba_mla_rwkv7_macaron_deepnorm_adaln_step"


def _layernorm(x, g, b):
    xf = x.astype(jnp.float32)
    mu = xf.mean(-1, keepdims=True)
    var = jnp.square(xf - mu).mean(-1, keepdims=True)
    return ((xf - mu) * lax.rsqrt(var + LN_EPS)).astype(x.dtype) * g + b


def _rmsnorm(x, g):
    xf = x.astype(jnp.float32)
    return (xf * lax.rsqrt(jnp.mean(xf * xf, -1, keepdims=True) + RMS_EPS)).astype(x.dtype) * g


def _rope_tables(T, offset):
    pos = jnp.arange(T, dtype=jnp.float32) + offset
    inv = ROPE_THETA ** (-jnp.arange(0, QK_ROPE_DIM, 2, dtype=jnp.float32) / QK_ROPE_DIM)
    ang = pos[:, None] * inv[None, :]
    return jnp.cos(ang), jnp.sin(ang)


def _rope(x, cos, sin):
    half = x.shape[-1] // 2
    x1, x2 = x[..., :half], x[..., half:]
    cos = cos.astype(x.dtype)
    sin = sin.astype(x.dtype)
    return jnp.concatenate([x1 * cos - x2 * sin, x1 * sin + x2 * cos], axis=-1)


def _modulate(x, shift, scale):
    return x * (1 + scale[:, None, :]) + shift[:, None, :]


def _swiglu(h, w_in, w_out):
    gate, up = jnp.split(h @ w_in, 2, axis=-1)
    return (jax.nn.silu(gate) * up) @ w_out


def _attend_prefix(q_lat, q_rope, ckv, kr):
    f32 = jnp.float32
    T = q_lat.shape[1]
    outs = []
    for i in range(T // Q_BLOCK):
        s0, e = i * Q_BLOCK, (i + 1) * Q_BLOCK
        kl = ckv[:, :e].astype(f32)
        s = (jnp.einsum('bthl,bsl->bhts', q_lat[:, s0:e].astype(f32), kl)
             + jnp.einsum('bthr,bsr->bhts', q_rope[:, s0:e].astype(f32), kr[:, :e].astype(f32))) * SM_SCALE
        mask = (s0 + jnp.arange(Q_BLOCK))[:, None] >= jnp.arange(e)[None, :]
        p = jax.nn.softmax(jnp.where(mask, s, -jnp.inf), axis=-1)
        outs.append(jnp.einsum('bhts,bsl->bthl', p, kl))
    return jnp.concatenate(outs, axis=1).astype(q_lat.dtype)


def _attend_paged(cache_lat, cache_kr, page_table, layer, q_lat, q_rope, ckv, kr):
    f32 = jnp.float32
    ql, qr = q_lat.astype(f32), q_rope.astype(f32)
    DB, T, H, L = ql.shape

    def step(carry, phys):
        m, lsum, acc = carry
        kl = cache_lat[layer, phys].astype(f32)
        kp = cache_kr[layer, phys].astype(f32)
        s = (jnp.einsum('bthl,bpl->bhtp', ql, kl) + jnp.einsum('bthr,bpr->bhtp', qr, kp)) * SM_SCALE
        m_new = jnp.maximum(m, s.max(-1))
        corr = jnp.exp(m - m_new)
        p = jnp.exp(s - m_new[..., None])
        return (m_new, lsum * corr + p.sum(-1), acc * corr[..., None] + jnp.einsum('bhtp,bpl->bhtl', p, kl)), None

    init = (jnp.full((DB, H, T), -jnp.inf, f32), jnp.zeros((DB, H, T), f32), jnp.zeros((DB, H, T, L), f32))
    (m, lsum, acc), _ = lax.scan(step, init, page_table.T)
    kl = ckv.astype(f32)
    s = (jnp.einsum('bthl,bsl->bhts', ql, kl) + jnp.einsum('bthr,bsr->bhts', qr, kr.astype(f32))) * SM_SCALE
    mask = jnp.arange(T)[:, None] >= jnp.arange(T)[None, :]
    s = jnp.where(mask, s, -jnp.inf)
    m_new = jnp.maximum(m, s.max(-1))
    corr = jnp.exp(m - m_new)
    p = jnp.exp(s - m_new[..., None])
    lsum = lsum * corr + p.sum(-1)
    acc = acc * corr[..., None] + jnp.einsum('bhts,bsl->bhtl', p, kl)
    out = acc / lsum[..., None]
    return jnp.transpose(out, (0, 2, 1, 3)).astype(q_lat.dtype)


def _wkv_scan(S0, r, decay, k, v, kk, a):
    f32 = jnp.float32
    xs = tuple(jnp.moveaxis(t.astype(f32), 1, 0) for t in (r, decay, k, v, kk, a))

    def step(S, inp):
        r_t, w_t, k_t, v_t, kk_t, a_t = inp
        sa = jnp.einsum('bhvk,bhk->bhv', S, -kk_t)
        S = (S * w_t[:, :, None, :] + sa[..., None] * (kk_t * a_t)[:, :, None, :]
             + v_t[..., None] * k_t[:, :, None, :])
        return S, jnp.einsum('bhvk,bhk->bhv', S, r_t)

    S, ys = lax.scan(step, S0.astype(f32), xs)
    return jnp.moveaxis(ys, 0, 1), S


def _mixer(h, cos, sin, S0, shift0, lp, attend):
    f32 = jnp.float32
    B, T, _ = h.shape
    proj = h @ lp['w_in']
    cq, ckv_raw, kr_raw, u = jnp.split(proj, [Q_LORA_RANK, Q_LORA_RANK + KV_LORA_RANK, W_MLA_IN], axis=-1)
    q = (_rmsnorm(cq, lp['g_q']) @ lp['w_uq']).reshape(B, T, MLA_HEADS, QK_NOPE_DIM + QK_ROPE_DIM)
    q_nope = q[..., :QK_NOPE_DIM]
    q_rope = _rope(q[..., QK_NOPE_DIM:], cos[:, None, :], sin[:, None, :])
    ckv = _rmsnorm(ckv_raw, lp['g_kv'])
    kr = _rope(kr_raw, cos, sin)
    w_ukv = lp['w_ukv'].reshape(KV_LORA_RANK, MLA_HEADS, QK_NOPE_DIM + V_HEAD_DIM)
    w_uk, w_uv = w_ukv[..., :QK_NOPE_DIM], w_ukv[..., QK_NOPE_DIM:]
    q_lat = jnp.einsum('bthn,lhn->bthl', q_nope, w_uk)
    o_lat = attend(q_lat, q_rope, ckv, kr)
    attn_out = jnp.einsum('bthl,lhv->bthv', o_lat, w_uv).reshape(B, T, MLA_DIM)
    u_prev = jnp.concatenate([shift0[:, None, :].astype(u.dtype), u[:, :-1]], axis=1)
    um = u + (u_prev - u) * lp['mu_shift']
    r, k, v, dw, da, dg = jnp.split(um, [RWKV_DIM, 2 * RWKV_DIM, 3 * RWKV_DIM, 3 * RWKV_DIM + DECAY_LORA,
                                         3 * RWKV_DIM + DECAY_LORA + ICLR_LORA], axis=-1)
    w_log = -jax.nn.softplus(-(lp['w0'] + jnp.tanh(dw) @ lp['w_decay'])) - 0.5
    decay = jnp.exp(-jnp.exp(w_log.astype(f32)))
    a = jax.nn.sigmoid(lp['a0'] + da @ lp['w_iclr'])
    g = jax.nn.sigmoid(dg) @ lp['w_gate']
    hs = (B, T, RWKV_HEADS, RWKV_HEAD_DIM)
    kk = (k * lp['k_k']).reshape(hs).astype(f32)
    kk = kk / jnp.maximum(jnp.sqrt(jnp.sum(kk * kk, -1, keepdims=True)), 1e-12)
    k = k * (1 + (a - 1) * lp['k_a'])
    r4, k4, v4 = r.reshape(hs), k.reshape(hs), v.reshape(hs)
    y, S = _wkv_scan(S0, r4, decay.reshape(hs), k4, v4, kk, a.reshape(hs))
    ym = y.mean(-1, keepdims=True)
    yv = jnp.square(y - ym).mean(-1, keepdims=True)
    yn = ((y - ym) * lax.rsqrt(yv + GN_EPS)).reshape(B, T, RWKV_DIM).astype(h.dtype) * lp['lnx_g'] + lp['lnx_b']
    bonus = (jnp.sum(r4 * k4 * lp['r_k'], -1, keepdims=True) * v4).reshape(B, T, RWKV_DIM)
    rwkv_out = (yn + bonus) * g
    mix = jnp.concatenate([attn_out, rwkv_out], axis=-1) @ lp['w_o']
    return mix, (ckv, kr, S.astype(S0.dtype), u[:, -1])


def _layer(x, c, cos, sin, S0, shift0, lp, attend):
    ada = (jax.nn.silu(c) @ lp['w_ada'] + lp['b_ada']).reshape(c.shape[0], N_ADA, D_MODEL)
    sh1, sc1, g1, sh2, sc2, g2, sh3, sc3, g3 = [ada[:, i] for i in range(N_ADA)]
    h = _modulate(x, sh1, sc1)
    x = _layernorm(ALPHA * x + 0.5 * g1[:, None, :] * _swiglu(h, lp['w_ffn1_in'], lp['w_ffn1_out']),
                   lp['ln_g'][0], lp['ln_b'][0])
    h = _modulate(x, sh2, sc2)
    mix, new_state = _mixer(h, cos, sin, S0, shift0, lp, attend)
    x = _layernorm(ALPHA * x + g2[:, None, :] * mix, lp['ln_g'][1], lp['ln_b'][1])
    h = _modulate(x, sh3, sc3)
    x = _layernorm(ALPHA * x + 0.5 * g3[:, None, :] * _swiglu(h, lp['w_ffn2_in'], lp['w_ffn2_out']),
                   lp['ln_g'][2], lp['ln_b'][2])
    return x, new_state


def setup_inputs(seed: int = 0) -> dict:
    key = jax.random.key(seed)
    ks = jax.random.split(key, 40)
    f32 = jnp.float32

    def nrm(i, shape, scale):
        return jax.random.normal(ks[i], shape, f32) * scale

    n_pages = PAST_LEN // PAGE_SIZE
    used = DEC_BATCH * n_pages
    n_pool = used + used // 4
    page_table = jax.random.permutation(ks[0], n_pool)[:used].reshape(DEC_BATCH, n_pages).astype(jnp.int32)
    return {
        "x_prompt": nrm(1, (BATCH, SEQ, D_MODEL), 1.0),
        "x_sample": nrm(2, (DEC_BATCH, DEC_SEQ, D_MODEL), 1.0),
        "c_prompt": nrm(3, (BATCH, D_MODEL), 1.0),
        "c_sample": nrm(4, (DEC_BATCH, D_MODEL), 1.0),
        "cache_kv_latent": nrm(5, (DEPTH, n_pool, PAGE_SIZE, KV_LORA_RANK), 1.0),
        "cache_k_rope": nrm(6, (DEPTH, n_pool, PAGE_SIZE, QK_ROPE_DIM), 1.0),
        "state_wkv": nrm(7, (DEPTH, DEC_BATCH, RWKV_HEADS, RWKV_HEAD_DIM, RWKV_HEAD_DIM), 0.5),
        "state_shift": nrm(8, (DEPTH, DEC_BATCH, W_RWKV_IN), 1.0),
        "page_table": page_table,
        "w_ada": nrm(9, (DEPTH, D_MODEL, N_ADA * D_MODEL), 0.5 * D_MODEL ** -0.5),
        "b_ada": nrm(10, (DEPTH, N_ADA * D_MODEL), 0.02),
        "ln_g": 1.0 + nrm(11, (DEPTH, 3, D_MODEL), 0.02),
        "ln_b": nrm(12, (DEPTH, 3, D_MODEL), 0.02),
        "w_ffn1_in": nrm(13, (DEPTH, D_MODEL, 2 * D_FF), D_MODEL ** -0.5),
        "w_ffn1_out": nrm(14, (DEPTH, D_FF, D_MODEL), BETA * D_FF ** -0.5),
        "w_ffn2_in": nrm(15, (DEPTH, D_MODEL, 2 * D_FF), D_MODEL ** -0.5),
        "w_ffn2_out": nrm(16, (DEPTH, D_FF, D_MODEL), BETA * D_FF ** -0.5),
        "w_in": nrm(17, (DEPTH, D_MODEL, IN_COLS), D_MODEL ** -0.5),
        "g_q": 1.0 + nrm(18, (DEPTH, Q_LORA_RANK), 0.02),
        "g_kv": 1.0 + nrm(19, (DEPTH, KV_LORA_RANK), 0.02),
        "w_uq": nrm(20, (DEPTH, Q_LORA_RANK, MLA_HEADS * (QK_NOPE_DIM + QK_ROPE_DIM)), Q_LORA_RANK ** -0.5),
        "w_ukv": nrm(21, (DEPTH, KV_LORA_RANK, MLA_HEADS * (QK_NOPE_DIM + V_HEAD_DIM)), KV_LORA_RANK ** -0.5),
        "mu_shift": jax.random.uniform(ks[22], (DEPTH, W_RWKV_IN), f32, 0.0, 1.0),
        "w0": jax.random.uniform(ks[23], (DEPTH, RWKV_DIM), f32, -6.0, 1.0),
        "w_decay": nrm(24, (DEPTH, DECAY_LORA, RWKV_DIM), 0.1 * DECAY_LORA ** -0.5),
        "a0": nrm(25, (DEPTH, RWKV_DIM), 0.1),
        "w_iclr": nrm(26, (DEPTH, ICLR_LORA, RWKV_DIM), 0.5 * ICLR_LORA ** -0.5),
        "w_gate": nrm(27, (DEPTH, GATE_LORA, RWKV_DIM), GATE_LORA ** -0.5),
        "k_k": 0.85 + nrm(28, (DEPTH, RWKV_DIM), 0.02),
        "k_a": 1.0 + nrm(29, (DEPTH, RWKV_DIM), 0.02),
        "r_k": nrm(30, (DEPTH, RWKV_HEADS, RWKV_HEAD_DIM), 0.1),
        "lnx_g": 1.0 + nrm(31, (DEPTH, RWKV_DIM), 0.02),
        "lnx_b": nrm(32, (DEPTH, RWKV_DIM), 0.02),
        "w_o": nrm(33, (DEPTH, D_MIX, D_MODEL), BETA * D_MIX ** -0.5),
    }


def reference(x_prompt, x_sample, c_prompt, c_sample, cache_kv_latent, cache_k_rope, state_wkv, state_shift,
              page_table, w_ada, b_ada, ln_g, ln_b, w_ffn1_in, w_ffn1_out, w_ffn2_in, w_ffn2_out, w_in, g_q, g_kv,
              w_uq, w_ukv, mu_shift, w0, w_decay, a0, w_iclr, w_gate, k_k, k_a, r_k, lnx_g, lnx_b, w_o):
    past = page_table.shape[1] * PAGE_SIZE
    cos_p, sin_p = _rope_tables(x_prompt.shape[1], 0)
    cos_s, sin_s = _rope_tables(x_sample.shape[1], past)
    S0_prompt = jnp.zeros((x_prompt.shape[0], RWKV_HEADS, RWKV_HEAD_DIM, RWKV_HEAD_DIM), x_prompt.dtype)
    shift0_prompt = jnp.zeros((x_prompt.shape[0], W_RWKV_IN), x_prompt.dtype)
    yp, ys = x_prompt, x_sample
    lat_p, kr_p, wkv_p, sh_p = [], [], [], []
    lat_s, kr_s, wkv_s, sh_s = [], [], [], []
    for l in range(DEPTH):
        lp = dict(w_ada=w_ada[l], b_ada=b_ada[l], ln_g=ln_g[l], ln_b=ln_b[l],
                  w_ffn1_in=w_ffn1_in[l], w_ffn1_out=w_ffn1_out[l], w_ffn2_in=w_ffn2_in[l], w_ffn2_out=w_ffn2_out[l],
                  w_in=w_in[l], g_q=g_q[l], g_kv=g_kv[l], w_uq=w_uq[l], w_ukv=w_ukv[l],
                  mu_shift=mu_shift[l], w0=w0[l], w_decay=w_decay[l], a0=a0[l], w_iclr=w_iclr[l],
                  w_gate=w_gate[l], k_k=k_k[l], k_a=k_a[l], r_k=r_k[l], lnx_g=lnx_g[l], lnx_b=lnx_b[l],
                  w_o=w_o[l])
        yp, (ckv, kr, S, sh) = _layer(yp, c_prompt, cos_p, sin_p, S0_prompt, shift0_prompt, lp, _attend_prefix)
        lat_p.append(ckv); kr_p.append(kr); wkv_p.append(S); sh_p.append(sh)
        paged = functools.partial(_attend_paged, cache_kv_latent, cache_k_rope, page_table, l)
        ys, (ckv, kr, S, sh) = _layer(ys, c_sample, cos_s, sin_s, state_wkv[l], state_shift[l], lp, paged)
        lat_s.append(ckv); kr_s.append(kr); wkv_s.append(S); sh_s.append(sh)
    return (yp, ys,
            jnp.stack(lat_p), jnp.stack(kr_p), jnp.stack(wkv_p), jnp.stack(sh_p),
            jnp.stack(lat_s), jnp.stack(kr_s), jnp.stack(wkv_s), jnp.stack(sh_s))
```

```python
import functools
import math

import jax
import jax.numpy as jnp
from jax import lax
from jax.experimental import pallas as pl
from jax.experimental.pallas import tpu as pltpu

F32 = jnp.float32
BF16 = jnp.bfloat16

QK_NOPE_DIM = 128
QK_ROPE_DIM = 64
V_HEAD_DIM = 128
MLA_HEADS = 16
Q_LORA_RANK = 896
KV_LORA_RANK = 512
RWKV_HEAD_DIM = 64
RWKV_HEADS = 32
RWKV_DIM = RWKV_HEADS * RWKV_HEAD_DIM
DECAY_LORA = 128
ICLR_LORA = 128
GATE_LORA = 256
W_MLA_IN = Q_LORA_RANK + KV_LORA_RANK + QK_ROPE_DIM
PAGE_SIZE = 128
ROPE_THETA = 10000.0
SM_SCALE = (QK_NOPE_DIM + QK_ROPE_DIM) ** -0.5
DEPTH = 1
ALPHA = (2 * DEPTH) ** 0.25
LN_EPS = 1e-5
RMS_EPS = 1e-6
GN_EPS = 64e-5

V7X_VMEM_BYTES = 64 * 1024 * 1024
V7X_LANES = 128
MIB = 1024 * 1024

MLA_COLS = 2048
MLA_KV_OFF = 1024

ROW_CHUNK = 64
N_CHUNK = 512
WKV_HG = 4


def _params(sem, vmem_bytes):
    limit = min(int(vmem_bytes) + 8 * MIB, V7X_VMEM_BYTES - 6 * MIB)
    return pltpu.CompilerParams(dimension_semantics=sem, vmem_limit_bytes=limit)


def _dot(a, b):
    return jnp.dot(a, b, preferred_element_type=F32)


def _dot_nt(a, b):
    return lax.dot_general(a, b, (((1,), (1,)), ((), ())), preferred_element_type=F32)


def _dot_tn(a, b):
    return lax.dot_general(a, b, (((0,), (0,)), ((), ())), preferred_element_type=F32)


def _split_bf16(x, parts):
    out = []
    rem = x
    for _ in range(parts):
        h = rem.astype(BF16)
        out.append(h)
        rem = rem - h.astype(F32)
    return out


def _layernorm_rows(z, g, b):
    mu = jnp.mean(z, axis=-1, keepdims=True)
    d = z - mu
    var = jnp.mean(d * d, axis=-1, keepdims=True)
    return d * lax.rsqrt(var + LN_EPS) * g + b


def _for_row_chunks(n_rows, fn):
    def body(i, carry):
        fn(pl.ds(pl.multiple_of(i * ROW_CHUNK, ROW_CHUNK), ROW_CHUNK))
        return carry

    lax.fori_loop(0, n_rows // ROW_CHUNK, body, 0)


def _rows_of(ref, rs):
    return ref[...] if ref.shape[0] == 1 else ref[rs, :]


def _modulate_into(h_scr, x_ref, sh_ref, sc_ref):
    def chunk(rs):
        h_scr[rs, :] = (x_ref[rs, :] * (1.0 + _rows_of(sc_ref, rs)) + _rows_of(sh_ref, rs)).astype(BF16)

    _for_row_chunks(h_scr.shape[0], chunk)


def _residual_layernorm(o_ref, x_ref, gate_ref, lng_ref, lnb_ref, gate_scale):
    def chunk(rs):
        z = ALPHA * x_ref[rs, :] + gate_scale * _rows_of(gate_ref, rs) * o_ref[rs, :]
        o_ref[rs, :] = _layernorm_rows(z, lng_ref[...], lnb_ref[...])

    _for_row_chunks(o_ref.shape[0], chunk)


def _ada_kernel(c_ref, w_ref, b_ref, o_ref):
    c = c_ref[...]
    a = (c * jax.nn.sigmoid(c)).astype(BF16)
    o_ref[...] = _dot(a, w_ref[...].astype(BF16)) + b_ref[...]


def _ada(c_all, w_ada, b_ada, tn=512):
    m, k = c_all.shape
    n = w_ada.shape[1]
    need = 2 * (k * tn * 4) + k * tn * 2 + 2 * m * k * 4 + 4 * m * tn * 4
    return pl.pallas_call(
        _ada_kernel,
        grid=(n // tn,),
        in_specs=[
            pl.BlockSpec((m, k), lambda j: (0, 0)),
            pl.BlockSpec((k, tn), lambda j: (0, j)),
            pl.BlockSpec((1, tn), lambda j: (0, j)),
        ],
        out_specs=pl.BlockSpec((m, tn), lambda j: (0, j)),
        out_shape=jax.ShapeDtypeStruct((m, n), F32),
        compiler_params=_params(("parallel",), need),
        name="ada",
    )(c_all, w_ada, b_ada.reshape(1, n))


def _ffn_kernel(x_ref, sh_ref, sc_ref, g_ref, wg_ref, wu_ref, wo_ref, lng_ref, lnb_ref, o_ref, h_scr, *, nj):
    j = pl.program_id(1)

    @pl.when(j == 0)
    def _():
        _modulate_into(h_scr, x_ref, sh_ref, sc_ref)
        o_ref[...] = jnp.zeros_like(o_ref)

    h = h_scr[...]
    gate = _dot(h, wg_ref[...])
    up = _dot(h, wu_ref[...])
    act = (gate * jax.nn.sigmoid(gate) * up).astype(BF16)
    for n0 in range(0, o_ref.shape[1], N_CHUNK):
        o_ref[:, n0:n0 + N_CHUNK] += _dot(act, wo_ref[:, n0:n0 + N_CHUNK])

    @pl.when(j == nj - 1)
    def _():
        _residual_layernorm(o_ref, x_ref, g_ref, lng_ref, lnb_ref, 0.5)


def _ffn(x, sh, sc, g, w_in, w_out, ln_g, ln_b, *, tm, tf, tiles_per_group):
    r, d = x.shape
    ff = w_out.shape[0]
    nj = ff // tf
    mr = sh.shape[1]
    mod_spec = pl.BlockSpec((None, mr, d), lambda i, j: (i // tiles_per_group, 0, 0))
    need = 4 * tm * d * 4 + tm * d * 2 + 2 * 3 * d * tf * 2 + 6 * mr * d * 4 + 6 * tm * tf * 4
    return pl.pallas_call(
        functools.partial(_ffn_kernel, nj=nj),
        grid=(r // tm, nj),
        in_specs=[
            pl.BlockSpec((tm, d), lambda i, j: (i, 0)),
            mod_spec, mod_spec, mod_spec,
            pl.BlockSpec((d, tf), lambda i, j: (0, j)),
            pl.BlockSpec((d, tf), lambda i, j: (0, nj + j)),
            pl.BlockSpec((tf, d), lambda i, j: (j, 0)),
            pl.BlockSpec((1, d), lambda i, j: (0, 0)),
            pl.BlockSpec((1, d), lambda i, j: (0, 0)),
        ],
        out_specs=pl.BlockSpec((tm, d), lambda i, j: (i, 0)),
        out_shape=jax.ShapeDtypeStruct((r, d), F32),
        scratch_shapes=[pltpu.VMEM((tm, d), BF16)],
        compiler_params=_params(("parallel", "arbitrary"), need),
        name="ffn",
    )(x, sh, sc, g, w_in, w_in, w_out, ln_g.reshape(1, d), ln_b.reshape(1, d))


def _mm_mod_kernel(a_ref, sh_ref, sc_ref, w_ref, o_ref, h_scr):
    @pl.when(pl.program_id(1) == 0)
    def _():
        _modulate_into(h_scr, a_ref, sh_ref, sc_ref)

    o_ref[...] = _dot(h_scr[...], w_ref[...]).astype(o_ref.dtype)


def _mm_rms_kernel(a_ref, g_ref, w_ref, o_ref, h_scr):
    @pl.when(pl.program_id(1) == 0)
    def _():
        a = a_ref[...]
        ms = jnp.mean(a * a, axis=-1, keepdims=True)
        h_scr[...] = (a * lax.rsqrt(ms + RMS_EPS) * g_ref[...]).astype(BF16)

    o_ref[...] = _dot(h_scr[...], w_ref[...]).astype(o_ref.dtype)


def _mm_plain_kernel(a_ref, w_ref, o_ref, h_scr):
    @pl.when(pl.program_id(1) == 0)
    def _():
        h_scr[...] = a_ref[...].astype(BF16)

    o_ref[...] = _dot(h_scr[...], w_ref[...]).astype(o_ref.dtype)


def _mm(a, w, *, tm, tn, k=None, a_colblk=0, mod=None, rms_g=None, tiles_per_group=1, out_dtype=F32, name="mm"):
    r = a.shape[0]
    k = a.shape[1] if k is None else k
    n = w.shape[1]
    a_spec = pl.BlockSpec((tm, k), lambda i, j: (i, a_colblk))
    w_spec = pl.BlockSpec((k, tn), lambda i, j: (0, j))
    need = 2 * tm * k * a.dtype.itemsize + tm * k * 2 + 2 * k * tn * 2 + 4 * tm * tn * 4
    if mod is not None:
        sh, sc = mod
        mr = sh.shape[1]
        mod_spec = pl.BlockSpec((None, mr, k), lambda i, j: (i // tiles_per_group, 0, 0))
        kern, ins, args = _mm_mod_kernel, [a_spec, mod_spec, mod_spec, w_spec], (a, sh, sc, w)
        need += 4 * mr * k * 4
    elif rms_g is not None:
        kern, ins, args = _mm_rms_kernel, [a_spec, pl.BlockSpec((1, k), lambda i, j: (0, 0)), w_spec], (a, rms_g.reshape(1, k), w)
    else:
        kern, ins, args = _mm_plain_kernel, [a_spec, w_spec], (a, w)
    return pl.pallas_call(
        kern,
        grid=(r // tm, n // tn),
        in_specs=ins,
        out_specs=pl.BlockSpec((tm, tn), lambda i, j: (i, j)),
        out_shape=jax.ShapeDtypeStruct((r, n), out_dtype),
        scratch_shapes=[pltpu.VMEM((tm, k), BF16)],
        compiler_params=_params(("parallel", "arbitrary"), need),
        name=name,
    )(*args)


def _bmm_kernel(a_ref, w_ref, o_ref):
    o_ref[...] = _dot(a_ref[...].astype(BF16), w_ref[...]).astype(o_ref.dtype)


def _bmm(a, w, *, tm, out_dtype=F32, name="bmm"):
    r = a.shape[0]
    h, ka, nb = w.shape
    need = 2 * tm * ka * 4 + 2 * ka * nb * 2 + 2 * tm * nb * 4
    return pl.pallas_call(
        _bmm_kernel,
        grid=(r // tm, h),
        in_specs=[
            pl.BlockSpec((tm, ka), lambda i, hh: (i, hh)),
            pl.BlockSpec((None, ka, nb), lambda i, hh: (hh, 0, 0)),
        ],
        out_specs=pl.BlockSpec((tm, nb), lambda i, hh: (i, hh)),
        out_shape=jax.ShapeDtypeStruct((r, h * nb), out_dtype),
        compiler_params=_params(("parallel", "parallel"), need),
        name=name,
    )(a, w)


def _oproj_kernel(x_ref, g_ref, a1_ref, a2_ref, w1_ref, w2_ref, lng_ref, lnb_ref, o_ref, *, nk):
    kk = pl.program_id(1)

    @pl.when(kk == 0)
    def _():
        o_ref[...] = jnp.zeros_like(o_ref)

    a1 = a1_ref[...].astype(BF16)
    a2 = a2_ref[...].astype(BF16)
    for n0 in range(0, o_ref.shape[1], N_CHUNK):
        cols = slice(n0, n0 + N_CHUNK)
        o_ref[:, cols] += _dot(a1, w1_ref[:, cols]) + _dot(a2, w2_ref[:, cols])

    @pl.when(kk == nk - 1)
    def _():
        _residual_layernorm(o_ref, x_ref, g_ref, lng_ref, lnb_ref, 1.0)


def _oproj(x, g, attn, rwkv, w_o, ln_g, ln_b, *, tm, tk, tiles_per_group):
    r, d = x.shape
    half = attn.shape[1]
    nk = half // tk
    mr = g.shape[1]
    need = 4 * tm * d * 4 + 2 * mr * d * 4 + 4 * tm * tk * 4 + 4 * tk * d * 2
    return pl.pallas_call(
        functools.partial(_oproj_kernel, nk=nk),
        grid=(r // tm, nk),
        in_specs=[
            pl.BlockSpec((tm, d), lambda i, kk: (i, 0)),
            pl.BlockSpec((None, mr, d), lambda i, kk: (i // tiles_per_group, 0, 0)),
            pl.BlockSpec((tm, tk), lambda i, kk: (i, kk)),
            pl.BlockSpec((tm, tk), lambda i, kk: (i, kk)),
            pl.BlockSpec((tk, d), lambda i, kk: (kk, 0)),
            pl.BlockSpec((tk, d), lambda i, kk: (nk + kk, 0)),
            pl.BlockSpec((1, d), lambda i, kk: (0, 0)),
            pl.BlockSpec((1, d), lambda i, kk: (0, 0)),
        ],
        out_specs=pl.BlockSpec((tm, d), lambda i, kk: (i, 0)),
        out_shape=jax.ShapeDtypeStruct((r, d), F32),
        compiler_params=_params(("parallel", "arbitrary"), need),
        name="oproj",
    )(x, g, attn, rwkv, w_o, w_o, ln_g.reshape(1, d), ln_b.reshape(1, d))


def _lat_kernel(m_ref, g_ref, cos_ref, sin_ref, ckv_ref, kr_ref, kr2_ref):
    m = m_ref[...]
    kv = m[:, :KV_LORA_RANK]
    ms = jnp.mean(kv * kv, axis=-1, keepdims=True)
    ckv_ref[...] = kv * lax.rsqrt(ms + RMS_EPS) * g_ref[...]
    a2 = m[:, KV_LORA_RANK:KV_LORA_RANK + V7X_LANES]
    b2 = m[:, KV_LORA_RANK + V7X_LANES:KV_LORA_RANK + 2 * V7X_LANES]
    kr2 = a2 * cos_ref[...] + b2 * sin_ref[...]
    kr2_ref[...] = kr2
    kr_ref[...] = kr2[:, :QK_ROPE_DIM]


def _lat_prep(mla, g_kv, cos4, sin4, *, tm):
    r = mla.shape[0]
    nb = cos4.shape[0] // tm
    wblk = MLA_COLS - MLA_KV_OFF
    need = 2 * tm * wblk * 4 + 2 * tm * (KV_LORA_RANK + 3 * V7X_LANES + 2 * V7X_LANES) * 4
    return pl.pallas_call(
        _lat_kernel,
        grid=(r // tm,),
        in_specs=[
            pl.BlockSpec((tm, wblk), lambda i: (i, MLA_KV_OFF // wblk)),
            pl.BlockSpec((1, KV_LORA_RANK), lambda i: (0, 0)),
            pl.BlockSpec((tm, V7X_LANES), lambda i: (i % nb, 0)),
            pl.BlockSpec((tm, V7X_LANES), lambda i: (i % nb, 0)),
        ],
        out_specs=[
            pl.BlockSpec((tm, KV_LORA_RANK), lambda i: (i, 0)),
            pl.BlockSpec((tm, QK_ROPE_DIM), lambda i: (i, 0)),
            pl.BlockSpec((tm, V7X_LANES), lambda i: (i, 0)),
        ],
        out_shape=[
            jax.ShapeDtypeStruct((r, KV_LORA_RANK), F32),
            jax.ShapeDtypeStruct((r, QK_ROPE_DIM), F32),
            jax.ShapeDtypeStruct((r, V7X_LANES), F32),
        ],
        compiler_params=_params(("parallel",), need),
        name="lat_prep",
    )(mla, g_kv.reshape(1, KV_LORA_RANK), cos4, sin4)


def _ropeq_kernel(a_ref, b_ref, cos_ref, sin_ref, o_ref):
    reps = a_ref.shape[1] // V7X_LANES
    c = jnp.concatenate([cos_ref[...]] * reps, axis=1)
    s = jnp.concatenate([sin_ref[...]] * reps, axis=1)
    o_ref[...] = (a_ref[...] * c + b_ref[...] * s).astype(o_ref.dtype)


def _rope_q(qall, cos4, sin4, *, tm):
    r = qall.shape[0]
    w = MLA_HEADS * QK_ROPE_DIM
    nb = cos4.shape[0] // tm
    need = 4 * tm * w * 4 + 2 * tm * w * 2 + 4 * tm * w * 4
    return pl.pallas_call(
        _ropeq_kernel,
        grid=(r // tm,),
        in_specs=[
            pl.BlockSpec((tm, w), lambda i: (i, 2)),
            pl.BlockSpec((tm, w), lambda i: (i, 3)),
            pl.BlockSpec((tm, V7X_LANES), lambda i: (i % nb, 0)),
            pl.BlockSpec((tm, V7X_LANES), lambda i: (i % nb, 0)),
        ],
        out_specs=pl.BlockSpec((tm, w), lambda i: (i, 0)),
        out_shape=jax.ShapeDtypeStruct((r, w), BF16),
        compiler_params=_params(("parallel",), need),
        name="rope_q",
    )(qall, qall, cos4, sin4)


def _flash_kernel(qn_ref, qr_ref, kn_ref, kr_ref, v_ref, o_ref, m_scr, l_scr, acc_scr, *, tq, tk):
    qi = pl.program_id(2)
    ki = pl.program_id(3)

    @pl.when(ki == 0)
    def _():
        m_scr[...] = jnp.full_like(m_scr, -jnp.inf)
        l_scr[...] = jnp.zeros_like(l_scr)
        acc_scr[...] = jnp.zeros_like(acc_scr)

    @pl.when(ki <= qi)
    def _():
        qn = qn_ref[...].astype(BF16)
        qr = qr_ref[...]
        kn = kn_ref[...].astype(BF16)
        kr2 = kr_ref[...].astype(BF16)
        v = v_ref[...].astype(BF16)
        lane = lax.broadcasted_iota(jnp.int32, (1, V7X_LANES), 1)
        row = qi * tq + lax.broadcasted_iota(jnp.int32, (tq, 1), 0)
        col = ki * tk + lax.broadcasted_iota(jnp.int32, (1, tk), 1)
        causal = row >= col
        for e in range(2):
            hs = slice(e * QK_NOPE_DIM, (e + 1) * QK_NOPE_DIM)
            in_head = (lane >= QK_ROPE_DIM) if e else (lane < QK_ROPE_DIM)
            qre = jnp.where(in_head, qr, jnp.zeros_like(qr))
            s = (_dot_nt(qn[:, hs], kn[:, hs]) + _dot_nt(qre, kr2)) * SM_SCALE
            s = jnp.where(causal, s, -jnp.inf)
            m_prev = m_scr[e]
            m_new = jnp.maximum(m_prev, jnp.max(s, axis=-1, keepdims=True))
            corr = jnp.exp(m_prev - m_new)
            p = jnp.exp(s - m_new)
            l_scr[e] = l_scr[e] * corr + jnp.sum(p, axis=-1, keepdims=True)
            acc_scr[e] = acc_scr[e] * corr + _dot(p.astype(BF16), v[:, hs])
            m_scr[e] = m_new

    @pl.when(ki == qi)
    def _():
        for e in range(2):
            hs = slice(e * V_HEAD_DIM, (e + 1) * V_HEAD_DIM)
            o_ref[:, hs] = acc_scr[e] / l_scr[e]


def _flash(qall, q_rope, kvup, kr2, *, batch, seq, tq):
    tk = tq
    nq = seq // tq
    r = qall.shape[0]
    hp = MLA_HEADS // 2
    need = 2 * (tq * 256 * 4 + tq * 128 * 2 + 2 * tk * 256 * 4 + tk * 128 * 4 + tq * 256 * 4) + 4 * tq * 128 * 4 + 8 * tq * tk * 4
    return pl.pallas_call(
        functools.partial(_flash_kernel, tq=tq, tk=tk),
        grid=(batch, hp, nq, nq),
        in_specs=[
            pl.BlockSpec((tq, 2 * QK_NOPE_DIM), lambda b, h, qi, ki: (b * nq + qi, h)),
            pl.BlockSpec((tq, V7X_LANES), lambda b, h, qi, ki: (b * nq + qi, h)),
            pl.BlockSpec((tk, 2 * QK_NOPE_DIM), lambda b, h, qi, ki: (b * nq + jnp.minimum(ki, qi), h)),
            pl.BlockSpec((tk, V7X_LANES), lambda b, h, qi, ki: (b * nq + jnp.minimum(ki, qi), 0)),
            pl.BlockSpec((tk, 2 * V_HEAD_DIM), lambda b, h, qi, ki: (b * nq + jnp.minimum(ki, qi), hp + h)),
        ],
        out_specs=pl.BlockSpec((tq, 2 * V_HEAD_DIM), lambda b, h, qi, ki: (b * nq + qi, h)),
        out_shape=jax.ShapeDtypeStruct((r, MLA_HEADS * V_HEAD_DIM), F32),
        scratch_shapes=[
            pltpu.VMEM((2, tq, 1), F32),
            pltpu.VMEM((2, tq, 1), F32),
            pltpu.VMEM((2, tq, V_HEAD_DIM), F32),
        ],
        compiler_params=_params(("parallel", "parallel", "parallel", "arbitrary"), need),
        name="flash",
    )(qall, q_rope, kvup, kr2, kvup)


def _paged_kernel(pt_ref, ql_ref, qr_ref, cn_ref, krn_ref, *rest, npg, nchunk, new_len, heads):
    lat_refs = rest[:npg]
    kr_refs = rest[npg:2 * npg]
    o_ref, m_scr, l_scr, acc_scr = rest[2 * npg:]
    c = pl.program_id(1)

    @pl.when(c == 0)
    def _():
        m_scr[...] = jnp.full_like(m_scr, -jnp.inf)
        l_scr[...] = jnp.zeros_like(l_scr)
        acc_scr[...] = jnp.zeros_like(acc_scr)

    ql = ql_ref[...]
    qr = qr_ref[...]

    def update(s, vals):
        m_prev = m_scr[...]
        m_new = jnp.maximum(m_prev, jnp.max(s, axis=-1, keepdims=True))
        corr = jnp.exp(m_prev - m_new)
        p = jnp.exp(s - m_new)
        l_scr[...] = l_scr[...] * corr + jnp.sum(p, axis=-1, keepdims=True)
        w = s.shape[1] // len(vals)
        pv = _dot(p[:, :w].astype(BF16), vals[0])
        for i in range(1, len(vals)):
            pv += _dot(p[:, i * w:(i + 1) * w].astype(BF16), vals[i])
        acc_scr[...] = acc_scr[...] * corr + pv
        m_scr[...] = m_new

    kls = [lat_refs[i][...].astype(BF16) for i in range(npg)]
    s = jnp.concatenate(
        [_dot_nt(ql, kls[i]) + _dot_nt(qr, kr_refs[i][...].astype(BF16)) for i in range(npg)], axis=1)
    update(s * SM_SCALE, kls)

    @pl.when(c == nchunk - 1)
    def _():
        kn = cn_ref[...].astype(BF16)
        s2 = (_dot_nt(ql, kn) + _dot_nt(qr, krn_ref[...].astype(BF16))) * SM_SCALE
        rows = s2.shape[0]
        t_row = lax.broadcasted_iota(jnp.int32, (rows, 1), 0) // heads
        t_col = lax.broadcasted_iota(jnp.int32, (1, s2.shape[1]), 1)
        ok = (t_col <= t_row) & (t_col < new_len)
        update(jnp.where(ok, s2, -jnp.inf), [kn])
        o_ref[...] = acc_scr[...] / l_scr[...]


def _paged(page_table, q_lat, q_rope, ckv_new, kr_new, cache_lat, cache_kr, *, npg, new_len):
    db, rows, lat = q_lat.shape
    n_pages = page_table.shape[1]
    nchunk = n_pages // npg
    npad = ckv_new.shape[1]
    pt_flat = page_table.reshape(-1)

    def page_map(i):
        return lambda b, c, pt: (0, pt[b * n_pages + c * npg + i], 0, 0)

    in_specs = [
        pl.BlockSpec((None, rows, lat), lambda b, c, pt: (b, 0, 0)),
        pl.BlockSpec((None, rows, QK_ROPE_DIM), lambda b, c, pt: (b, 0, 0)),
        pl.BlockSpec((None, npad, lat), lambda b, c, pt: (b, 0, 0)),
        pl.BlockSpec((None, npad, QK_ROPE_DIM), lambda b, c, pt: (b, 0, 0)),
    ]
    in_specs += [pl.BlockSpec((None, None, PAGE_SIZE, lat), page_map(i)) for i in range(npg)]
    in_specs += [pl.BlockSpec((None, None, PAGE_SIZE, QK_ROPE_DIM), page_map(i)) for i in range(npg)]
    need = (2 * npg * PAGE_SIZE * (lat + V7X_LANES) * 4 + npg * PAGE_SIZE * lat * 2
            + 6 * rows * npg * PAGE_SIZE * 4 + 8 * rows * lat * 4)
    grid_spec = pltpu.PrefetchScalarGridSpec(
        num_scalar_prefetch=1,
        grid=(db, nchunk),
        in_specs=in_specs,
        out_specs=pl.BlockSpec((None, rows, lat), lambda b, c, pt: (b, 0, 0)),
        scratch_shapes=[
            pltpu.VMEM((rows, 1), F32),
            pltpu.VMEM((rows, 1), F32),
            pltpu.VMEM((rows, lat), F32),
        ],
    )
    return pl.pallas_call(
        functools.partial(_paged_kernel, npg=npg, nchunk=nchunk, new_len=new_len, heads=MLA_HEADS),
        grid_spec=grid_spec,
        out_shape=jax.ShapeDtypeStruct((db, rows, lat), F32),
        compiler_params=_params(("parallel", "arbitrary"), need),
        name="paged",
    )(pt_flat, q_lat, q_rope, ckv_new, kr_new, *([cache_lat] * npg), *([cache_kr] * npg))


def _prep_kernel(u_ref, up_ref, mu_ref, w0_ref, a0_ref, kk_ref, ka_ref, wd_ref, wi_ref, wg_ref,
                 r_ref, k_ref, v_ref, kkr_ref, a_ref, lw_ref, g_ref):
    u = u_ref[...]
    um = u + (up_ref[...] - u) * mu_ref[...]
    n = RWKV_DIM
    k = um[:, n:2 * n]
    dw = um[:, 3 * n:3 * n + DECAY_LORA]
    da = um[:, 3 * n + DECAY_LORA:3 * n + DECAY_LORA + ICLR_LORA]
    dg = um[:, 3 * n + DECAY_LORA + ICLR_LORA:]
    z = -(w0_ref[...] + _dot(jnp.tanh(dw).astype(BF16), wd_ref[...]))
    softplus = jnp.maximum(z, 0.0) + jnp.log1p(jnp.exp(-jnp.abs(z)))
    lw_ref[...] = -jnp.exp(-softplus - 0.5)
    a = jax.nn.sigmoid(a0_ref[...] + _dot(da.astype(BF16), wi_ref[...]))
    a_ref[...] = a
    g_ref[...] = _dot(jax.nn.sigmoid(dg).astype(BF16), wg_ref[...])
    r_ref[...] = um[:, :n]
    v_ref[...] = um[:, 2 * n:3 * n]
    kkr_ref[...] = k * kk_ref[...]
    k_ref[...] = k * (1.0 + (a - 1.0) * ka_ref[...])


def _rwkv_prep(u, u_prev, mu, w0, a0, k_k, k_a, w_decay, w_iclr, w_gate, *, tm):
    r, wu = u.shape
    n = RWKV_DIM
    row = lambda width: pl.BlockSpec((1, width), lambda i: (0, 0))
    full = lambda arr: pl.BlockSpec(arr.shape, lambda i: (0, 0))
    out_spec = pl.BlockSpec((tm, n), lambda i: (i, 0))
    need = 2 * 2 * tm * wu * 4 + 2 * 7 * tm * n * 4 + 4 * tm * wu * 4
    return pl.pallas_call(
        _prep_kernel,
        grid=(r // tm,),
        in_specs=[
            pl.BlockSpec((tm, wu), lambda i: (i, 0)),
            pl.BlockSpec((tm, wu), lambda i: (i, 0)),
            row(wu), row(n), row(n), row(n), row(n),
            full(w_decay), full(w_iclr), full(w_gate),
        ],
        out_specs=[out_spec] * 7,
        out_shape=[jax.ShapeDtypeStruct((r, n), F32)] * 7,
        compiler_params=_params(("parallel",), need),
        name="rwkv_prep",
    )(u, u_prev, mu.reshape(1, wu), w0.reshape(1, n), a0.reshape(1, n), k_k.reshape(1, n), k_a.reshape(1, n),
      w_decay, w_iclr, w_gate)


def _wkv_kernel(*refs, c_len, hg, n_chunks, has_s0):
    if has_s0:
        (r_ref, k_ref, v_ref, kkr_ref, a_ref, lw_ref, g_ref, rk_ref, lng_ref, lnb_ref, s0_ref,
         o_ref, sout_ref, st_scr) = refs
    else:
        (r_ref, k_ref, v_ref, kkr_ref, a_ref, lw_ref, g_ref, rk_ref, lng_ref, lnb_ref,
         o_ref, sout_ref, st_scr) = refs
    n = RWKV_HEAD_DIM
    w = hg * n
    rw = hg * c_len
    shift = int(math.log2(n))
    ci = pl.program_id(2)

    lane_head = lax.broadcasted_iota(jnp.int32, (1, w), 1) >> shift
    key_head = lax.broadcasted_iota(jnp.int32, (w, 1), 0) >> shift
    same_head = key_head == lane_head
    row_head = lax.broadcasted_iota(jnp.int32, (rw, 1), 0) // c_len
    stack_mask = row_head == lane_head

    @pl.when(ci == 0)
    def _():
        if has_s0:
            s0t = jnp.transpose(s0_ref[...].reshape(w, n))
            st_scr[...] = jnp.where(same_head, jnp.concatenate([s0t] * hg, axis=0), 0.0)
        else:
            st_scr[...] = jnp.zeros_like(st_scr)

    r = r_ref[...]
    k = k_ref[...]
    v = v_ref[...]
    kkr = kkr_ref[...]
    a = a_ref[...]
    lw = lw_ref[...]

    seg_ones = jnp.where(same_head, 1.0, 0.0).astype(BF16)

    def seg_sum(x):
        hi, lo = _split_bf16(x, 2)
        return _dot(hi, seg_ones) + _dot(lo, seg_ones)

    t_r = lax.broadcasted_iota(jnp.int32, (c_len, 1), 0)
    t_c = lax.broadcasted_iota(jnp.int32, (1, c_len), 1)
    tril = jnp.where(t_c <= t_r, 1.0, 0.0).astype(BF16)
    l1, l2, l3 = _split_bf16(lw, 3)
    cum = _dot(tril, l1) + _dot(tril, l2) + _dot(tril, l3)
    cum_end = cum[c_len - 1:c_len, :]

    kkn = kkr / jnp.maximum(jnp.sqrt(seg_sum(kkr * kkr)), 1e-12)
    kka = kkn * a
    p_inv = jnp.exp(-cum)
    p_tail = jnp.exp(cum_end - cum)

    def stack(x):
        return jnp.where(stack_mask, jnp.concatenate([x] * hg, axis=0), 0.0).astype(BF16)

    a_s = stack(jnp.exp(cum - lw) * (-kkn))
    r_s = stack(jnp.exp(cum) * r)
    b_s = stack(p_inv * kka)
    k_s = stack(p_inv * k)
    v_s = stack(v)
    be_s = stack(p_tail * kka)
    ke_s = stack(p_tail * k)

    ri = lax.broadcasted_iota(jnp.int32, (rw, 1), 0)
    cj = lax.broadcasted_iota(jnp.int32, (1, rw), 1)
    strict = cj < ri
    incl = cj <= ri
    l_ab = jnp.where(strict, _dot_nt(a_s, b_s), 0.0)
    l_ak = jnp.where(strict, _dot_nt(a_s, k_s), 0.0).astype(BF16)
    m_rb = jnp.where(incl, _dot_nt(r_s, b_s), 0.0).astype(BF16)
    m_rk = jnp.where(incl, _dot_nt(r_s, k_s), 0.0).astype(BF16)

    st = st_scr[...]
    st_b = st.astype(BF16)
    x = _dot(a_s, st_b) + _dot(l_ak, v_s)
    n_it = int(math.log2(c_len))
    li = l_ab.astype(BF16)
    for it in range(n_it):
        x = x + _dot(li, x.astype(BF16))
        if it < n_it - 1:
            li = _dot(li, li).astype(BF16)
    u_s = x.astype(BF16)

    y_s = _dot(r_s, st_b) + _dot(m_rb, u_s) + _dot(m_rk, v_s)
    y = y_s[0:c_len]
    for e in range(1, hg):
        y = y + y_s[e * c_len:(e + 1) * c_len]

    pc_col = jnp.transpose(jnp.broadcast_to(jnp.exp(cum_end), (V7X_LANES, w)))[:, 0:1]
    st_new = pc_col * st + _dot_tn(be_s, u_s) + _dot_tn(ke_s, v_s)
    st_scr[...] = st_new

    inv_n = 1.0 / n
    mean = seg_sum(y) * inv_n
    d = y - mean
    var = seg_sum(d * d) * inv_n
    yn = d * lax.rsqrt(var + GN_EPS) * lng_ref[...] + lnb_ref[...]
    bonus = seg_sum(r * k * rk_ref[...]) * v
    o_ref[...] = (yn + bonus) * g_ref[...]

    @pl.when(ci == n_chunks - 1)
    def _():
        z = st_new[0:n]
        for e in range(1, hg):
            z = z + st_new[e * n:(e + 1) * n]
        sout_ref[...] = jnp.transpose(z).reshape(hg, n, n)


def _wkv(r, k, v, kkr, a, lw, g, r_k, lnx_g, lnx_b, s0, *, batch, c_len, n_chunks):
    rows, dim = r.shape
    hg = WKV_HG
    n = RWKV_HEAD_DIM
    w = hg * n
    ng = dim // w
    blk = pl.BlockSpec((c_len, w), lambda b, gi, ci: (b * n_chunks + ci, gi))
    vec = pl.BlockSpec((1, w), lambda b, gi, ci: (0, gi))
    st_spec = pl.BlockSpec((None, hg, n, n), lambda b, gi, ci: (b, gi, 0, 0))
    in_specs = [blk] * 7 + [vec] * 3
    args = [r, k, v, kkr, a, lw, g, r_k.reshape(1, dim), lnx_g.reshape(1, dim), lnx_b.reshape(1, dim)]
    if s0 is not None:
        in_specs.append(st_spec)
        args.append(s0)
    rw = hg * c_len
    need = 2 * 8 * c_len * w * 4 + 3 * w * w * 4 + 16 * rw * w * 4 + 8 * rw * rw * 4
    return pl.pallas_call(
        functools.partial(_wkv_kernel, c_len=c_len, hg=hg, n_chunks=n_chunks, has_s0=s0 is not None),
        grid=(batch, ng, n_chunks),
        in_specs=in_specs,
        out_specs=[blk, st_spec],
        out_shape=[
            jax.ShapeDtypeStruct((rows, dim), F32),
            jax.ShapeDtypeStruct((batch, dim // n, n, n), F32),
        ],
        scratch_shapes=[pltpu.VMEM((w, w), F32)],
        compiler_params=_params(("parallel", "parallel", "arbitrary"), need),
        name="wkv",
    )(*args)


def _rope_tables(pos):
    inv = ROPE_THETA ** (-jnp.arange(0, QK_ROPE_DIM, 2, dtype=F32) / QK_ROPE_DIM)
    ang = pos[:, None] * inv[None, :]
    cos, sin = jnp.cos(ang), jnp.sin(ang)
    return jnp.concatenate([cos] * 4, axis=1), jnp.concatenate([-sin, sin] * 2, axis=1)


def _swap_halves(w):
    half = w.shape[-1] // 2
    return jnp.concatenate([w[..., half:], w[..., :half]], axis=-1)


def _prepare_weights(w_in, w_uq, w_ukv, w_o, w_decay, w_iclr, w_gate):
    d = w_in.shape[0]
    rope = w_in[:, Q_LORA_RANK + KV_LORA_RANK:W_MLA_IN]
    rope_sw = _swap_halves(rope)
    zeros = lambda c: jnp.zeros((d, c), w_in.dtype)
    w_mla = jnp.concatenate(
        [w_in[:, :Q_LORA_RANK], zeros(MLA_KV_OFF - Q_LORA_RANK), w_in[:, Q_LORA_RANK:Q_LORA_RANK + KV_LORA_RANK],
         rope, rope, rope_sw, rope_sw, zeros(MLA_COLS - MLA_KV_OFF - KV_LORA_RANK - 4 * QK_ROPE_DIM)], axis=1).astype(BF16)
    w_rwkv = w_in[:, W_MLA_IN:].astype(BF16)
    uq = w_uq.reshape(Q_LORA_RANK, MLA_HEADS, QK_NOPE_DIM + QK_ROPE_DIM)
    uq_rope = uq[..., QK_NOPE_DIM:]
    w_q = jnp.concatenate(
        [uq[..., :QK_NOPE_DIM].reshape(Q_LORA_RANK, -1), uq_rope.reshape(Q_LORA_RANK, -1),
         _swap_halves(uq_rope).reshape(Q_LORA_RANK, -1)], axis=1).astype(BF16)
    ukv = w_ukv.reshape(KV_LORA_RANK, MLA_HEADS, QK_NOPE_DIM + V_HEAD_DIM)
    w_uk, w_uv = ukv[..., :QK_NOPE_DIM], ukv[..., QK_NOPE_DIM:]
    w_kvup = jnp.concatenate([w_uk.reshape(KV_LORA_RANK, -1), w_uv.reshape(KV_LORA_RANK, -1)], axis=1).astype(BF16)
    w_uk_t = jnp.transpose(w_uk, (1, 2, 0)).astype(BF16)
    w_uv_h = jnp.transpose(w_uv, (1, 0, 2)).astype(BF16)
    return dict(w_mla=w_mla, w_rwkv=w_rwkv, w_q=w_q, w_kvup=w_kvup, w_uk_t=w_uk_t, w_uv_h=w_uv_h,
                w_o=w_o.astype(BF16), w_decay=w_decay.astype(BF16), w_iclr=w_iclr.astype(BF16),
                w_gate=w_gate.astype(BF16))


def _group_layer(x, mods, tiles_per_group, tm, wts, p, *, cos4, sin4, attend, wkv_run, u_prev_fn):
    sh1, sc1, g1, sh2, sc2, g2, sh3, sc3, g3 = mods
    x1 = _ffn(x, sh1, sc1, g1, wts["ffn1_in"], wts["ffn1_out"], p["ln_g"][0], p["ln_b"][0],
              tm=tm, tf=256, tiles_per_group=tiles_per_group)
    mla = _mm(x1, wts["w_mla"], tm=tm, tn=512, mod=(sh2, sc2), tiles_per_group=tiles_per_group, name="proj_mla")
    u = _mm(x1, wts["w_rwkv"], tm=tm, tn=512, mod=(sh2, sc2), tiles_per_group=tiles_per_group, name="proj_rwkv")
    ckv, kr, kr2 = _lat_prep(mla, p["g_kv"], cos4, sin4, tm=tm)
    qall = _mm(mla, wts["w_q"], tm=tm, tn=512, k=Q_LORA_RANK, rms_g=p["g_q"], name="q_proj")
    q_rope = _rope_q(qall, cos4, sin4, tm=tm)
    attn = attend(qall, q_rope, ckv, kr, kr2)
    prep = _rwkv_prep(u, u_prev_fn(u), p["mu_shift"], p["w0"], p["a0"], p["k_k"], p["k_a"],
                      wts["w_decay"], wts["w_iclr"], wts["w_gate"], tm=128)
    rwkv, s_new = wkv_run(prep)
    x2 = _oproj(x1, g2, attn, rwkv, wts["w_o"], p["ln_g"][1], p["ln_b"][1], tm=tm, tk=256,
                tiles_per_group=tiles_per_group)
    y = _ffn(x2, sh3, sc3, g3, wts["ffn2_in"], wts["ffn2_out"], p["ln_g"][2], p["ln_b"][2],
             tm=tm, tf=256, tiles_per_group=tiles_per_group)
    return y, ckv, kr, s_new, u


def kernel(x_prompt, x_sample, c_prompt, c_sample, cache_kv_latent, cache_k_rope, state_wkv, state_shift, page_table, w_ada, b_ada, ln_g, ln_b, w_ffn1_in, w_ffn1_out, w_ffn2_in, w_ffn2_out, w_in, g_q, g_kv, w_uq, w_ukv, mu_shift, w0, w_decay, a0, w_iclr, w_gate, k_k, k_a, r_k, lnx_g, lnx_b, w_o):
    batch, seq, d = x_prompt.shape
    db, dseq, _ = x_sample.shape
    depth = w_ada.shape[0]
    assert depth == DEPTH == 1
    n_ada = w_ada.shape[2] // d
    past = page_table.shape[1] * PAGE_SIZE
    wu = state_shift.shape[2]

    c_all = jnp.concatenate([c_prompt, c_sample], axis=0)
    cos_p, sin_p = _rope_tables(jnp.arange(seq, dtype=F32) + 0)
    pos_s = jnp.tile(jnp.arange(dseq, dtype=F32) + past, db)
    cos_s, sin_s = _rope_tables(pos_s)

    yp = x_prompt.reshape(batch * seq, d)
    ys = x_sample.reshape(db * dseq, d)
    outs_p, outs_s = [], []
    for l in range(depth):
        p = dict(ln_g=ln_g[l], ln_b=ln_b[l], g_q=g_q[l], g_kv=g_kv[l], mu_shift=mu_shift[l], w0=w0[l], a0=a0[l],
                 k_k=k_k[l], k_a=k_a[l], r_k=r_k[l], lnx_g=lnx_g[l], lnx_b=lnx_b[l])
        wts = _prepare_weights(w_in[l], w_uq[l], w_ukv[l], w_o[l], w_decay[l], w_iclr[l], w_gate[l])
        wts.update(ffn1_in=w_ffn1_in[l].astype(BF16), ffn1_out=w_ffn1_out[l].astype(BF16),
                   ffn2_in=w_ffn2_in[l].astype(BF16), ffn2_out=w_ffn2_out[l].astype(BF16))
        ada = _ada(c_all, w_ada[l], b_ada[l]).reshape(batch + db, n_ada, d)
        mods_p = [ada[:batch, i][:, None, :] for i in range(n_ada)]
        mods_s = [jnp.repeat(ada[batch:, i], dseq, axis=0)[None] for i in range(n_ada)]

        tm_p = 512

        def attend_p(qall, q_rope, ckv, kr, kr2):
            kvup = _mm(ckv, wts["w_kvup"], tm=tm_p, tn=512, name="kv_up")
            return _flash(qall, q_rope, kvup, kr2, batch=batch, seq=seq, tq=512)

        def u_prev_p(u):
            u3 = u.reshape(batch, seq, wu)
            return jnp.concatenate([jnp.zeros((batch, 1, wu), u.dtype), u3[:, :-1]], axis=1).reshape(batch * seq, wu)

        def wkv_p(prep):
            return _wkv(*prep, p["r_k"], p["lnx_g"], p["lnx_b"], None, batch=batch, c_len=64, n_chunks=seq // 64)

        yp, ckv_p, kr_p, s_p, u_p = _group_layer(
            yp, mods_p, seq // tm_p, tm_p, wts, p, cos4=cos_p, sin4=sin_p, attend=attend_p, wkv_run=wkv_p,
            u_prev_fn=u_prev_p)
        outs_p.append((ckv_p.reshape(batch, seq, -1), kr_p.reshape(batch, seq, -1), s_p,
                       u_p.reshape(batch, seq, wu)[:, -1]))

        tm_s = db * dseq
        c_pad = 16

        def attend_s(qall, q_rope, ckv, kr, kr2):
            rows = dseq * MLA_HEADS
            q_lat = _bmm(qall, wts["w_uk_t"], tm=tm_s, out_dtype=BF16, name="q_absorb")
            pad = lambda t: jnp.pad(t.reshape(db, dseq, -1), ((0, 0), (0, c_pad - dseq), (0, 0)))
            o_lat = _paged(page_table, q_lat.reshape(db, rows, KV_LORA_RANK), q_rope.reshape(db, rows, QK_ROPE_DIM),
                           pad(ckv), pad(kr), cache_kv_latent, cache_k_rope, npg=8, new_len=dseq)
            return _bmm(o_lat.reshape(db * dseq, MLA_HEADS * KV_LORA_RANK), wts["w_uv_h"], tm=tm_s, name="v_up")

        def u_prev_s(u):
            u3 = u.reshape(db, dseq, wu)
            return jnp.concatenate([state_shift[l][:, None, :], u3[:, :-1]], axis=1).reshape(db * dseq, wu)

        def wkv_s(prep):
            padded = [jnp.pad(t.reshape(db, dseq, -1), ((0, 0), (0, c_pad - dseq), (0, 0))).reshape(db * c_pad, -1)
                      for t in prep]
            out, s_new = _wkv(*padded, p["r_k"], p["lnx_g"], p["lnx_b"], state_wkv[l], batch=db, c_len=c_pad, n_chunks=1)
            return out.reshape(db, c_pad, -1)[:, :dseq].reshape(db * dseq, -1), s_new

        ys, ckv_s, kr_s, s_s, u_s = _group_layer(
            ys, mods_s, 1, tm_s, wts, p, cos4=cos_s, sin4=sin_s, attend=attend_s, wkv_run=wkv_s, u_prev_fn=u_prev_s)
        outs_s.append((ckv_s.reshape(db, dseq, -1), kr_s.reshape(db, dseq, -1), s_s,
                       u_s.reshape(db, dseq, wu)[:, -1]))

    stack = lambda outs, i: jnp.stack([o[i] for o in outs])
    return (yp.reshape(batch, seq, d), ys.reshape(db, dseq, d),
            stack(outs_p, 0), stack(outs_p, 1), stack(outs_p, 2), stack(outs_p, 3),
            stack(outs_s, 0), stack(outs_s, 1), stack(outs_s, 2), stack(outs_s, 3))
```

```python
import functools
import math

import jax
import jax.numpy as jnp
from jax import lax
from jax.experimental import pallas as pl
from jax.experimental.pallas import tpu as pltpu

F32 = jnp.float32
BF16 = jnp.bfloat16

QK_NOPE_DIM = 128
QK_ROPE_DIM = 64
V_HEAD_DIM = 128
MLA_HEADS = 16
Q_LORA_RANK = 896
KV_LORA_RANK = 512
RWKV_HEAD_DIM = 64
RWKV_HEADS = 32
RWKV_DIM = RWKV_HEADS * RWKV_HEAD_DIM
DECAY_LORA = 128
ICLR_LORA = 128
GATE_LORA = 256
W_MLA_IN = Q_LORA_RANK + KV_LORA_RANK + QK_ROPE_DIM
PAGE_SIZE = 128
ROPE_THETA = 10000.0
SM_SCALE = (QK_NOPE_DIM + QK_ROPE_DIM) ** -0.5
DEPTH = 1
ALPHA = (2 * DEPTH) ** 0.25
LN_EPS = 1e-5
RMS_EPS = 1e-6
GN_EPS = 64e-5

V7X_VMEM_BYTES = 64 * 1024 * 1024
V7X_LANES = 128
SUBLANES = 8
MIB = 1024 * 1024

MLA_COLS = 2048
MLA_KV_OFF = 1024

ROW_CHUNK = 64
N_CHUNK = 512
PREP_TM = 128
PAGES_PER_STEP = 32
WKV_HG = 4


def _params(sem, vmem_bytes):
    limit = min(int(vmem_bytes) + 8 * MIB, V7X_VMEM_BYTES - 6 * MIB)
    return pltpu.CompilerParams(dimension_semantics=sem, vmem_limit_bytes=limit)


def _dot(a, b):
    return jnp.dot(a, b, preferred_element_type=F32)


def _dot_nt(a, b):
    return lax.dot_general(a, b, (((1,), (1,)), ((), ())), preferred_element_type=F32)


def _dot_tn(a, b):
    return lax.dot_general(a, b, (((0,), (0,)), ((), ())), preferred_element_type=F32)


def _split_bf16(x, parts):
    out = []
    rem = x
    for _ in range(parts):
        h = rem.astype(BF16)
        out.append(h)
        rem = rem - h.astype(F32)
    return out


def _layernorm_rows(z, g, b):
    mu = jnp.mean(z, axis=-1, keepdims=True)
    d = z - mu
    var = jnp.mean(d * d, axis=-1, keepdims=True)
    return d * lax.rsqrt(var + LN_EPS) * g + b


def _for_row_chunks(n_rows, fn):
    def body(i, carry):
        fn(pl.ds(pl.multiple_of(i * ROW_CHUNK, ROW_CHUNK), ROW_CHUNK))
        return carry

    lax.fori_loop(0, n_rows // ROW_CHUNK, body, 0)


def _rows_of(ref, rs):
    return ref[...] if ref.shape[0] == 1 else ref[rs, :]


def _modulate_into(h_scr, x_ref, sh_ref, sc_ref):
    def chunk(rs):
        h_scr[rs, :] = (x_ref[rs, :] * (1.0 + _rows_of(sc_ref, rs)) + _rows_of(sh_ref, rs)).astype(BF16)

    _for_row_chunks(h_scr.shape[0], chunk)


def _residual_layernorm(o_ref, x_ref, gate_ref, lng_ref, lnb_ref, gate_scale):
    def chunk(rs):
        z = ALPHA * x_ref[rs, :] + gate_scale * _rows_of(gate_ref, rs) * o_ref[rs, :]
        o_ref[rs, :] = _layernorm_rows(z, lng_ref[...], lnb_ref[...])

    _for_row_chunks(o_ref.shape[0], chunk)


def _ada_kernel(c_ref, w_ref, b_ref, o_ref):
    c = c_ref[...]
    a = (c * jax.nn.sigmoid(c)).astype(BF16)
    o_ref[...] = _dot(a, w_ref[...].astype(BF16)) + b_ref[...]


def _ada(c_all, w_ada, b_ada, tn=512):
    m, k = c_all.shape
    n = w_ada.shape[1]
    need = 2 * (k * tn * 4) + k * tn * 2 + 2 * m * k * 4 + 4 * m * tn * 4
    return pl.pallas_call(
        _ada_kernel,
        grid=(n // tn,),
        in_specs=[
            pl.BlockSpec((m, k), lambda j: (0, 0)),
            pl.BlockSpec((k, tn), lambda j: (0, j)),
            pl.BlockSpec((1, tn), lambda j: (0, j)),
        ],
        out_specs=pl.BlockSpec((m, tn), lambda j: (0, j)),
        out_shape=jax.ShapeDtypeStruct((m, n), F32),
        compiler_params=_params(("parallel",), need),
        name="ada",
    )(c_all, w_ada, b_ada.reshape(1, n))


def _ffn_kernel(x_ref, sh_ref, sc_ref, g_ref, wg_ref, wu_ref, wo_ref, lng_ref, lnb_ref, o_ref, h_scr, *, nj):
    j = pl.program_id(1)

    @pl.when(j == 0)
    def _():
        _modulate_into(h_scr, x_ref, sh_ref, sc_ref)
        o_ref[...] = jnp.zeros_like(o_ref)

    h = h_scr[...]
    gate = _dot(h, wg_ref[...])
    up = _dot(h, wu_ref[...])
    act = (gate * jax.nn.sigmoid(gate) * up).astype(BF16)
    for n0 in range(0, o_ref.shape[1], N_CHUNK):
        o_ref[:, n0:n0 + N_CHUNK] += _dot(act, wo_ref[:, n0:n0 + N_CHUNK])

    @pl.when(j == nj - 1)
    def _():
        _residual_layernorm(o_ref, x_ref, g_ref, lng_ref, lnb_ref, 0.5)


def _ffn(x, sh, sc, g, w_in, w_out, ln_g, ln_b, *, tm, tf, tiles_per_group):
    r, d = x.shape
    ff = w_out.shape[0]
    nj = ff // tf
    mr = sh.shape[1]
    mod_spec = pl.BlockSpec((None, mr, d), lambda i, j: (i // tiles_per_group, 0, 0))
    need = 4 * tm * d * 4 + tm * d * 2 + 2 * 3 * d * tf * 2 + 6 * mr * d * 4 + 6 * tm * tf * 4
    return pl.pallas_call(
        functools.partial(_ffn_kernel, nj=nj),
        grid=(r // tm, nj),
        in_specs=[
            pl.BlockSpec((tm, d), lambda i, j: (i, 0)),
            mod_spec, mod_spec, mod_spec,
            pl.BlockSpec((d, tf), lambda i, j: (0, j)),
            pl.BlockSpec((d, tf), lambda i, j: (0, nj + j)),
            pl.BlockSpec((tf, d), lambda i, j: (j, 0)),
            pl.BlockSpec((1, d), lambda i, j: (0, 0)),
            pl.BlockSpec((1, d), lambda i, j: (0, 0)),
        ],
        out_specs=pl.BlockSpec((tm, d), lambda i, j: (i, 0)),
        out_shape=jax.ShapeDtypeStruct((r, d), F32),
        scratch_shapes=[pltpu.VMEM((tm, d), BF16)],
        compiler_params=_params(("parallel", "arbitrary"), need),
        name="ffn",
    )(x, sh, sc, g, w_in, w_in, w_out, ln_g.reshape(1, d), ln_b.reshape(1, d))


def _mm_mod_kernel(a_ref, sh_ref, sc_ref, w_ref, o_ref, h_scr):
    @pl.when(pl.program_id(1) == 0)
    def _():
        _modulate_into(h_scr, a_ref, sh_ref, sc_ref)

    o_ref[...] = _dot(h_scr[...], w_ref[...]).astype(o_ref.dtype)


def _mm_rms_kernel(a_ref, g_ref, w_ref, o_ref, h_scr):
    @pl.when(pl.program_id(1) == 0)
    def _():
        a = a_ref[...]
        ms = jnp.mean(a * a, axis=-1, keepdims=True)
        h_scr[...] = (a * lax.rsqrt(ms + RMS_EPS) * g_ref[...]).astype(BF16)

    o_ref[...] = _dot(h_scr[...], w_ref[...]).astype(o_ref.dtype)


def _mm_plain_kernel(a_ref, w_ref, o_ref, h_scr):
    @pl.when(pl.program_id(1) == 0)
    def _():
        h_scr[...] = a_ref[...].astype(BF16)

    o_ref[...] = _dot(h_scr[...], w_ref[...]).astype(o_ref.dtype)


def _mm(a, w, *, tm, tn, k=None, a_colblk=0, mod=None, rms_g=None, tiles_per_group=1, out_dtype=F32, name="mm"):
    r = a.shape[0]
    k = a.shape[1] if k is None else k
    n = w.shape[1]
    a_spec = pl.BlockSpec((tm, k), lambda i, j: (i, a_colblk))
    w_spec = pl.BlockSpec((k, tn), lambda i, j: (0, j))
    need = 2 * tm * k * a.dtype.itemsize + tm * k * 2 + 2 * k * tn * 2 + 4 * tm * tn * 4
    if mod is not None:
        sh, sc = mod
        mr = sh.shape[1]
        mod_spec = pl.BlockSpec((None, mr, k), lambda i, j: (i // tiles_per_group, 0, 0))
        kern, ins, args = _mm_mod_kernel, [a_spec, mod_spec, mod_spec, w_spec], (a, sh, sc, w)
        need += 4 * mr * k * 4
    elif rms_g is not None:
        kern, ins, args = _mm_rms_kernel, [a_spec, pl.BlockSpec((1, k), lambda i, j: (0, 0)), w_spec], (a, rms_g.reshape(1, k), w)
    else:
        kern, ins, args = _mm_plain_kernel, [a_spec, w_spec], (a, w)
    return pl.pallas_call(
        kern,
        grid=(r // tm, n // tn),
        in_specs=ins,
        out_specs=pl.BlockSpec((tm, tn), lambda i, j: (i, j)),
        out_shape=jax.ShapeDtypeStruct((r, n), out_dtype),
        scratch_shapes=[pltpu.VMEM((tm, k), BF16)],
        compiler_params=_params(("parallel", "arbitrary"), need),
        name=name,
    )(*args)


def _bmm_kernel(a_ref, w_ref, o_ref):
    o_ref[...] = _dot(a_ref[...].astype(BF16), w_ref[...]).astype(o_ref.dtype)


def _bmm(a, w, *, tm, out_dtype=F32, name="bmm"):
    r = a.shape[0]
    h, ka, nb = w.shape
    need = 2 * tm * ka * 4 + 2 * ka * nb * 2 + 2 * tm * nb * 4
    return pl.pallas_call(
        _bmm_kernel,
        grid=(r // tm, h),
        in_specs=[
            pl.BlockSpec((tm, ka), lambda i, hh: (i, hh)),
            pl.BlockSpec((None, ka, nb), lambda i, hh: (hh, 0, 0)),
        ],
        out_specs=pl.BlockSpec((tm, nb), lambda i, hh: (i, hh)),
        out_shape=jax.ShapeDtypeStruct((r, h * nb), out_dtype),
        compiler_params=_params(("parallel", "parallel"), need),
        name=name,
    )(a, w)


def _oproj_kernel(x_ref, g_ref, a1_ref, a2_ref, w1_ref, w2_ref, lng_ref, lnb_ref, o_ref, *, nk):
    kk = pl.program_id(1)

    @pl.when(kk == 0)
    def _():
        o_ref[...] = jnp.zeros_like(o_ref)

    a1 = a1_ref[...].astype(BF16)
    a2 = a2_ref[...].astype(BF16)
    for n0 in range(0, o_ref.shape[1], N_CHUNK):
        cols = slice(n0, n0 + N_CHUNK)
        o_ref[:, cols] += _dot(a1, w1_ref[:, cols]) + _dot(a2, w2_ref[:, cols])

    @pl.when(kk == nk - 1)
    def _():
        _residual_layernorm(o_ref, x_ref, g_ref, lng_ref, lnb_ref, 1.0)


def _oproj(x, g, attn, rwkv, w_o, ln_g, ln_b, *, tm, tk, tiles_per_group):
    r, d = x.shape
    half = attn.shape[1]
    nk = half // tk
    mr = g.shape[1]
    need = 4 * tm * d * 4 + 2 * mr * d * 4 + 4 * tm * tk * 4 + 4 * tk * d * 2
    return pl.pallas_call(
        functools.partial(_oproj_kernel, nk=nk),
        grid=(r // tm, nk),
        in_specs=[
            pl.BlockSpec((tm, d), lambda i, kk: (i, 0)),
            pl.BlockSpec((None, mr, d), lambda i, kk: (i // tiles_per_group, 0, 0)),
            pl.BlockSpec((tm, tk), lambda i, kk: (i, kk)),
            pl.BlockSpec((tm, tk), lambda i, kk: (i, kk)),
            pl.BlockSpec((tk, d), lambda i, kk: (kk, 0)),
            pl.BlockSpec((tk, d), lambda i, kk: (nk + kk, 0)),
            pl.BlockSpec((1, d), lambda i, kk: (0, 0)),
            pl.BlockSpec((1, d), lambda i, kk: (0, 0)),
        ],
        out_specs=pl.BlockSpec((tm, d), lambda i, kk: (i, 0)),
        out_shape=jax.ShapeDtypeStruct((r, d), F32),
        compiler_params=_params(("parallel", "arbitrary"), need),
        name="oproj",
    )(x, g, attn, rwkv, w_o, w_o, ln_g.reshape(1, d), ln_b.reshape(1, d))


def _lat_kernel(m_ref, g_ref, cos_ref, sin_ref, ckv_ref, kr_ref, kr2_ref):
    m = m_ref[...]
    kv = m[:, :KV_LORA_RANK]
    ms = jnp.mean(kv * kv, axis=-1, keepdims=True)
    ckv_ref[...] = kv * lax.rsqrt(ms + RMS_EPS) * g_ref[...]
    a2 = m[:, KV_LORA_RANK:KV_LORA_RANK + V7X_LANES]
    b2 = m[:, KV_LORA_RANK + V7X_LANES:KV_LORA_RANK + 2 * V7X_LANES]
    kr2 = a2 * cos_ref[...] + b2 * sin_ref[...]
    kr2_ref[...] = kr2
    kr_ref[...] = kr2[:, :QK_ROPE_DIM]


def _lat_prep(mla, g_kv, cos4, sin4, *, tm):
    r = mla.shape[0]
    nb = cos4.shape[0] // tm
    wblk = MLA_COLS - MLA_KV_OFF
    need = 2 * tm * wblk * 4 + 2 * tm * (KV_LORA_RANK + 3 * V7X_LANES + 2 * V7X_LANES) * 4
    return pl.pallas_call(
        _lat_kernel,
        grid=(r // tm,),
        in_specs=[
            pl.BlockSpec((tm, wblk), lambda i: (i, MLA_KV_OFF // wblk)),
            pl.BlockSpec((1, KV_LORA_RANK), lambda i: (0, 0)),
            pl.BlockSpec((tm, V7X_LANES), lambda i: (i % nb, 0)),
            pl.BlockSpec((tm, V7X_LANES), lambda i: (i % nb, 0)),
        ],
        out_specs=[
            pl.BlockSpec((tm, KV_LORA_RANK), lambda i: (i, 0)),
            pl.BlockSpec((tm, QK_ROPE_DIM), lambda i: (i, 0)),
            pl.BlockSpec((tm, V7X_LANES), lambda i: (i, 0)),
        ],
        out_shape=[
            jax.ShapeDtypeStruct((r, KV_LORA_RANK), F32),
            jax.ShapeDtypeStruct((r, QK_ROPE_DIM), F32),
            jax.ShapeDtypeStruct((r, V7X_LANES), F32),
        ],
        compiler_params=_params(("parallel",), need),
        name="lat_prep",
    )(mla, g_kv.reshape(1, KV_LORA_RANK), cos4, sin4)


def _ropeq_kernel(a_ref, b_ref, cos_ref, sin_ref, o_ref):
    reps = a_ref.shape[1] // V7X_LANES
    c = jnp.concatenate([cos_ref[...]] * reps, axis=1)
    s = jnp.concatenate([sin_ref[...]] * reps, axis=1)
    o_ref[...] = (a_ref[...] * c + b_ref[...] * s).astype(o_ref.dtype)


def _rope_q(qall, cos4, sin4, *, tm):
    r = qall.shape[0]
    w = MLA_HEADS * QK_ROPE_DIM
    nb = cos4.shape[0] // tm
    need = 4 * tm * w * 4 + 2 * tm * w * 2 + 4 * tm * w * 4
    return pl.pallas_call(
        _ropeq_kernel,
        grid=(r // tm,),
        in_specs=[
            pl.BlockSpec((tm, w), lambda i: (i, 2)),
            pl.BlockSpec((tm, w), lambda i: (i, 3)),
            pl.BlockSpec((tm, V7X_LANES), lambda i: (i % nb, 0)),
            pl.BlockSpec((tm, V7X_LANES), lambda i: (i % nb, 0)),
        ],
        out_specs=pl.BlockSpec((tm, w), lambda i: (i, 0)),
        out_shape=jax.ShapeDtypeStruct((r, w), F32),
        compiler_params=_params(("parallel",), need),
        name="rope_q",
    )(qall, qall, cos4, sin4)


def _flash_kernel(qn_ref, qr_ref, kn_ref, kr_ref, v_ref, o_ref, m_scr, l_scr, acc_scr, *, tq, tk):
    qi = pl.program_id(2)
    ki = pl.program_id(3)

    @pl.when(ki == 0)
    def _():
        m_scr[...] = jnp.full_like(m_scr, -jnp.inf)
        l_scr[...] = jnp.zeros_like(l_scr)
        acc_scr[...] = jnp.zeros_like(acc_scr)

    def step(on_diagonal):
        qn = (qn_ref[...] * SM_SCALE).astype(BF16)
        qr = qr_ref[...] * SM_SCALE
        kn = kn_ref[...].astype(BF16)
        kr2 = kr_ref[...].astype(BF16)
        v = v_ref[...].astype(BF16)
        lane = lax.broadcasted_iota(jnp.int32, (1, V7X_LANES), 1)
        if on_diagonal:
            causal = lax.broadcasted_iota(jnp.int32, (tq, 1), 0) >= lax.broadcasted_iota(jnp.int32, (1, tk), 1)
        for e in range(2):
            hs = slice(e * QK_NOPE_DIM, (e + 1) * QK_NOPE_DIM)
            in_head = (lane >= QK_ROPE_DIM) if e else (lane < QK_ROPE_DIM)
            qre = jnp.where(in_head, qr, 0.0).astype(BF16)
            s = _dot_nt(qn[:, hs], kn[:, hs]) + _dot_nt(qre, kr2)
            if on_diagonal:
                s = jnp.where(causal, s, -jnp.inf)
            m_prev = m_scr[e]
            m_new = jnp.maximum(m_prev, jnp.max(s, axis=-1, keepdims=True))
            corr = jnp.exp(m_prev - m_new)
            p = jnp.exp(s - m_new)
            l_new = l_scr[e] * corr + jnp.sum(p, axis=-1, keepdims=True)
            acc_new = acc_scr[e] * corr + _dot(p.astype(BF16), v[:, hs])
            if on_diagonal:
                o_ref[:, e * V_HEAD_DIM:(e + 1) * V_HEAD_DIM] = acc_new / l_new
            else:
                l_scr[e] = l_new
                acc_scr[e] = acc_new
                m_scr[e] = m_new

    @pl.when(ki < qi)
    def _():
        step(False)

    @pl.when(ki == qi)
    def _():
        step(True)


def _flash(qall, q_rope, kvup, kr2, *, batch, seq, tq):
    tk = tq
    nq = seq // tq
    r = qall.shape[0]
    hp = MLA_HEADS // 2
    need = 2 * (tq * 256 * 4 + tq * 128 * 2 + 2 * tk * 256 * 4 + tk * 128 * 4 + tq * 256 * 4) + 4 * tq * 128 * 4 + 8 * tq * tk * 4
    return pl.pallas_call(
        functools.partial(_flash_kernel, tq=tq, tk=tk),
        grid=(batch, hp, nq, nq),
        in_specs=[
            pl.BlockSpec((tq, 2 * QK_NOPE_DIM), lambda b, h, qi, ki: (b * nq + qi, h)),
            pl.BlockSpec((tq, V7X_LANES), lambda b, h, qi, ki: (b * nq + qi, h)),
            pl.BlockSpec((tk, 2 * QK_NOPE_DIM), lambda b, h, qi, ki: (b * nq + jnp.minimum(ki, qi), h)),
            pl.BlockSpec((tk, V7X_LANES), lambda b, h, qi, ki: (b * nq + jnp.minimum(ki, qi), 0)),
            pl.BlockSpec((tk, 2 * V_HEAD_DIM), lambda b, h, qi, ki: (b * nq + jnp.minimum(ki, qi), hp + h)),
        ],
        out_specs=pl.BlockSpec((tq, 2 * V_HEAD_DIM), lambda b, h, qi, ki: (b * nq + qi, h)),
        out_shape=jax.ShapeDtypeStruct((r, MLA_HEADS * V_HEAD_DIM), F32),
        scratch_shapes=[
            pltpu.VMEM((2, tq, 1), F32),
            pltpu.VMEM((2, tq, 1), F32),
            pltpu.VMEM((2, tq, V_HEAD_DIM), F32),
        ],
        compiler_params=_params(("parallel", "parallel", "parallel", "arbitrary"), need),
        name="flash",
    )(qall, q_rope, kvup, kr2, kvup)


def _paged_kernel(pt_ref, ql_ref, qr_ref, cn_ref, krn_ref, *rest, npg, nchunk, new_len, heads):
    lat_refs = rest[:npg]
    kr_refs = rest[npg:2 * npg]
    o_ref, m_scr, l_scr, acc_scr = rest[2 * npg:]
    c = pl.program_id(1)

    @pl.when(c == 0)
    def _():
        m_scr[...] = jnp.full_like(m_scr, -jnp.inf)
        l_scr[...] = jnp.zeros_like(l_scr)
        acc_scr[...] = jnp.zeros_like(acc_scr)

    ql = ql_ref[...]
    qr = qr_ref[...].astype(BF16)

    def update(s, vals):
        m_prev = m_scr[...]
        m_new = jnp.maximum(m_prev, jnp.max(s, axis=-1, keepdims=True))
        corr = jnp.exp(m_prev - m_new)
        p = jnp.exp(s - m_new)
        l_scr[...] = l_scr[...] * corr + jnp.sum(p, axis=-1, keepdims=True)
        w = s.shape[1] // len(vals)
        pv = _dot(p[:, :w].astype(BF16), vals[0])
        for i in range(1, len(vals)):
            pv += _dot(p[:, i * w:(i + 1) * w].astype(BF16), vals[i])
        acc_scr[...] = acc_scr[...] * corr + pv
        m_scr[...] = m_new

    kls = [lat_refs[i][...].astype(BF16) for i in range(npg)]
    s = jnp.concatenate(
        [_dot_nt(ql, kls[i]) + _dot(qr, kr_refs[i][...].astype(BF16)) for i in range(npg)], axis=1)
    update(s * SM_SCALE, kls)

    @pl.when(c == nchunk - 1)
    def _():
        kn = cn_ref[...].astype(BF16)
        s2 = (_dot_nt(ql, kn) + _dot_nt(qr, krn_ref[...].astype(BF16))) * SM_SCALE
        rows = s2.shape[0]
        t_row = lax.broadcasted_iota(jnp.int32, (rows, 1), 0) // heads
        t_col = lax.broadcasted_iota(jnp.int32, (1, s2.shape[1]), 1)
        ok = (t_col <= t_row) & (t_col < new_len)
        update(jnp.where(ok, s2, -jnp.inf), [kn])
        o_ref[...] = acc_scr[...] / l_scr[...]


def _paged(page_table, q_lat, q_rope, ckv_new, kr_new, cache_lat, cache_kr_t, *, npg, new_len):
    db, rows, lat = q_lat.shape
    n_pages = page_table.shape[1]
    nchunk = n_pages // npg
    npad = ckv_new.shape[1]
    pt_flat = page_table.reshape(-1)

    def page_map(i):
        return lambda b, c, pt: (0, pt[b * n_pages + c * npg + i], 0, 0)

    in_specs = [
        pl.BlockSpec((None, rows, lat), lambda b, c, pt: (b, 0, 0)),
        pl.BlockSpec((None, rows, QK_ROPE_DIM), lambda b, c, pt: (b, 0, 0)),
        pl.BlockSpec((None, npad, lat), lambda b, c, pt: (b, 0, 0)),
        pl.BlockSpec((None, npad, QK_ROPE_DIM), lambda b, c, pt: (b, 0, 0)),
    ]
    in_specs += [pl.BlockSpec((None, None, PAGE_SIZE, lat), page_map(i)) for i in range(npg)]
    in_specs += [pl.BlockSpec((None, None, QK_ROPE_DIM, PAGE_SIZE), page_map(i)) for i in range(npg)]
    need = (2 * npg * PAGE_SIZE * (lat + V7X_LANES) * 4 + npg * PAGE_SIZE * lat * 2
            + 6 * rows * npg * PAGE_SIZE * 4 + 8 * rows * lat * 4)
    grid_spec = pltpu.PrefetchScalarGridSpec(
        num_scalar_prefetch=1,
        grid=(db, nchunk),
        in_specs=in_specs,
        out_specs=pl.BlockSpec((None, rows, lat), lambda b, c, pt: (b, 0, 0)),
        scratch_shapes=[
            pltpu.VMEM((rows, 1), F32),
            pltpu.VMEM((rows, 1), F32),
            pltpu.VMEM((rows, lat), F32),
        ],
    )
    return pl.pallas_call(
        functools.partial(_paged_kernel, npg=npg, nchunk=nchunk, new_len=new_len, heads=MLA_HEADS),
        grid_spec=grid_spec,
        out_shape=jax.ShapeDtypeStruct((db, rows, lat), F32),
        compiler_params=_params(("parallel", "arbitrary"), need),
        name="paged",
    )(pt_flat, q_lat, q_rope, ckv_new, kr_new, *([cache_lat] * npg), *([cache_kr_t] * npg))


def _prep_kernel(u_ref, tail_ref, s0_ref, mu_ref, w0_ref, a0_ref, kk_ref, ka_ref, wd_ref, wi_ref, wg_ref,
                 r_ref, k_ref, v_ref, kkr_ref, a_ref, lw_ref, g_ref, *, seq_len):
    tm = u_ref.shape[0]
    u = u_ref[...]
    row = lax.broadcasted_iota(jnp.int32, (tm, 1), 0)
    prev = jnp.where(row == 0, tail_ref[SUBLANES - 1:SUBLANES, :], pltpu.roll(u, 1, 0))
    starts = ((pl.program_id(0) * tm + row) & (seq_len - 1)) == 0
    prev = jnp.where(starts, s0_ref[...], prev)
    um = u + (prev - u) * mu_ref[...]
    n = RWKV_DIM
    k = um[:, n:2 * n]
    dw = um[:, 3 * n:3 * n + DECAY_LORA]
    da = um[:, 3 * n + DECAY_LORA:3 * n + DECAY_LORA + ICLR_LORA]
    dg = um[:, 3 * n + DECAY_LORA + ICLR_LORA:]
    z = -(w0_ref[...] + _dot(jnp.tanh(dw).astype(BF16), wd_ref[...]))
    softplus = jnp.maximum(z, 0.0) + jnp.log1p(jnp.exp(-jnp.abs(z)))
    lw_ref[...] = -jnp.exp(-softplus - 0.5)
    a = jax.nn.sigmoid(a0_ref[...] + _dot(da.astype(BF16), wi_ref[...]))
    a_ref[...] = a
    g_ref[...] = _dot(jax.nn.sigmoid(dg).astype(BF16), wg_ref[...])
    r_ref[...] = um[:, :n]
    v_ref[...] = um[:, 2 * n:3 * n]
    kkr_ref[...] = k * kk_ref[...]
    k_ref[...] = k * (1.0 + (a - 1.0) * ka_ref[...])


def _rwkv_prep(u, shift0, mu, w0, a0, k_k, k_a, w_decay, w_iclr, w_gate, *, tm, seq_len):
    r, wu = u.shape
    n = RWKV_DIM
    assert seq_len & (seq_len - 1) == 0
    tiles_per_group = max(seq_len // tm, 1)
    mr = shift0.shape[1]
    row = lambda width: pl.BlockSpec((1, width), lambda i: (0, 0))
    full = lambda arr: pl.BlockSpec(arr.shape, lambda i: (0, 0))
    out_spec = pl.BlockSpec((tm, n), lambda i: (i, 0))
    need = 2 * tm * wu * 4 + 2 * mr * wu * 4 + 2 * 7 * tm * n * 4 + 6 * tm * wu * 4
    return pl.pallas_call(
        functools.partial(_prep_kernel, seq_len=seq_len),
        grid=(r // tm,),
        in_specs=[
            pl.BlockSpec((tm, wu), lambda i: (i, 0)),
            pl.BlockSpec((SUBLANES, wu), lambda i: (jnp.maximum(i * (tm // SUBLANES) - 1, 0), 0)),
            pl.BlockSpec((None, mr, wu), lambda i: (i // tiles_per_group, 0, 0)),
            row(wu), row(n), row(n), row(n), row(n),
            full(w_decay), full(w_iclr), full(w_gate),
        ],
        out_specs=[out_spec] * 7,
        out_shape=[jax.ShapeDtypeStruct((r, n), F32)] * 7,
        compiler_params=_params(("parallel",), need),
        name="rwkv_prep",
    )(u, u, shift0, mu.reshape(1, wu), w0.reshape(1, n), a0.reshape(1, n), k_k.reshape(1, n), k_a.reshape(1, n),
      w_decay, w_iclr, w_gate)


def _wkv_kernel(*refs, c_len, hg, gps, n_chunks, has_s0):
    if has_s0:
        (r_ref, k_ref, v_ref, kkr_ref, a_ref, lw_ref, g_ref, rk_ref, lng_ref, lnb_ref, s0_ref,
         o_ref, sout_ref, st_scr) = refs
    else:
        (r_ref, k_ref, v_ref, kkr_ref, a_ref, lw_ref, g_ref, rk_ref, lng_ref, lnb_ref,
         o_ref, sout_ref, st_scr) = refs
    n = RWKV_HEAD_DIM
    w = hg * n
    rw = hg * c_len
    shift = int(math.log2(n))
    ci = pl.program_id(2)

    lane_head = lax.broadcasted_iota(jnp.int32, (1, w), 1) >> shift
    key_head = lax.broadcasted_iota(jnp.int32, (w, 1), 0) >> shift
    same_head = key_head == lane_head
    row_head = lax.broadcasted_iota(jnp.int32, (rw, 1), 0) // c_len
    stack_mask = row_head == lane_head

    @pl.when(ci == 0)
    def _():
        for gi in range(gps):
            if has_s0:
                s0t = jnp.transpose(s0_ref[gi * hg:(gi + 1) * hg].reshape(w, n))
                st_scr[gi] = jnp.where(same_head, jnp.concatenate([s0t] * hg, axis=0), 0.0)
            else:
                st_scr[gi] = jnp.zeros((w, w), F32)

    new_states = _wkv_groups(gps, r_ref, k_ref, v_ref, kkr_ref, a_ref, lw_ref, g_ref, rk_ref, lng_ref, lnb_ref,
                             o_ref, st_scr, c_len=c_len, hg=hg, same_head=same_head, stack_mask=stack_mask)

    @pl.when(ci == n_chunks - 1)
    def _():
        for gi, st_new in enumerate(new_states):
            z = st_new[0:n]
            for e in range(1, hg):
                z = z + st_new[e * n:(e + 1) * n]
            sout_ref[gi * hg:(gi + 1) * hg] = jnp.transpose(z).reshape(hg, n, n)


def _wkv_groups(gps, r_ref, k_ref, v_ref, kkr_ref, a_ref, lw_ref, g_ref, rk_ref, lng_ref, lnb_ref, o_ref, st_scr, *,
                c_len, hg, same_head, stack_mask):
    n = RWKV_HEAD_DIM
    w = hg * n
    rw = hg * c_len
    groups = range(gps)
    lanes = [slice(gi * w, (gi + 1) * w) for gi in groups]
    each = lambda fn, *cols: [fn(*xs) for xs in zip(*cols)]

    seg_ones = jnp.where(same_head, 1.0, 0.0).astype(BF16)

    def seg_sum(x):
        hi, lo = _split_bf16(x, 2)
        return _dot(hi, seg_ones) + _dot(lo, seg_ones)

    def stack(x):
        return jnp.where(stack_mask, jnp.concatenate([x] * hg, axis=0), 0.0).astype(BF16)

    t_r = lax.broadcasted_iota(jnp.int32, (c_len, 1), 0)
    t_c = lax.broadcasted_iota(jnp.int32, (1, c_len), 1)
    tril = jnp.where(t_c <= t_r, 1.0, 0.0).astype(BF16)
    ri = lax.broadcasted_iota(jnp.int32, (rw, 1), 0)
    cj = lax.broadcasted_iota(jnp.int32, (1, rw), 1)
    strict = cj < ri
    incl = cj <= ri

    r = [r_ref[:, ls] for ls in lanes]
    k = [k_ref[:, ls] for ls in lanes]
    v = [v_ref[:, ls] for ls in lanes]
    a = [a_ref[:, ls] for ls in lanes]
    lw = [lw_ref[:, ls] for ls in lanes]
    kkr = [kkr_ref[:, ls] for ls in lanes]

    def cumsum(x):
        l1, l2, l3 = _split_bf16(x, 3)
        return _dot(tril, l1) + _dot(tril, l2) + _dot(tril, l3)

    cum = each(cumsum, lw)
    cum_end = [c[c_len - 1:c_len, :] for c in cum]
    kkn = each(lambda x: x / jnp.maximum(jnp.sqrt(seg_sum(x * x)), 1e-12), kkr)
    kka = each(lambda x, y: x * y, kkn, a)
    p_inv = each(lambda c: jnp.exp(-c), cum)
    p_tail = each(lambda ce, c: jnp.exp(ce - c), cum_end, cum)

    a_s = each(lambda c, l, x: stack(jnp.exp(c - l) * (-x)), cum, lw, kkn)
    r_s = each(lambda c, x: stack(jnp.exp(c) * x), cum, r)
    b_s = each(lambda p, x: stack(p * x), p_inv, kka)
    k_s = each(lambda p, x: stack(p * x), p_inv, k)
    v_s = each(stack, v)
    be_s = each(lambda p, x: stack(p * x), p_tail, kka)
    ke_s = each(lambda p, x: stack(p * x), p_tail, k)

    l_ab = each(lambda x, y: jnp.where(strict, _dot_nt(x, y), 0.0).astype(BF16), a_s, b_s)
    l_ak = each(lambda x, y: jnp.where(strict, _dot_nt(x, y), 0.0).astype(BF16), a_s, k_s)
    m_rb = each(lambda x, y: jnp.where(incl, _dot_nt(x, y), 0.0).astype(BF16), r_s, b_s)
    m_rk = each(lambda x, y: jnp.where(incl, _dot_nt(x, y), 0.0).astype(BF16), r_s, k_s)

    st = [st_scr[gi] for gi in groups]
    st_b = [s.astype(BF16) for s in st]
    x = each(lambda p, s, l, q: _dot(p, s) + _dot(l, q), a_s, st_b, l_ak, v_s)
    n_it = int(math.log2(c_len))
    li = l_ab
    for it in range(n_it):
        x = each(lambda xx, l: xx + _dot(l, xx.astype(BF16)), x, li)
        if it < n_it - 1:
            li = each(lambda l: _dot(l, l).astype(BF16), li)
    u_s = [xx.astype(BF16) for xx in x]

    def fold_heads(y_s):
        y = y_s[0:c_len]
        for e in range(1, hg):
            y = y + y_s[e * c_len:(e + 1) * c_len]
        return y

    y = each(lambda p, s, m1, u, m2, q: fold_heads(_dot(p, s) + _dot(m1, u) + _dot(m2, q)),
             r_s, st_b, m_rb, u_s, m_rk, v_s)

    def new_state(ce, s, b, u, kk_, q):
        pc_col = jnp.transpose(jnp.broadcast_to(jnp.exp(ce), (V7X_LANES, w)))[:, 0:1]
        return pc_col * s + _dot_tn(b, u) + _dot_tn(kk_, q)

    st_new = each(new_state, cum_end, st, be_s, u_s, ke_s, v_s)
    for gi in groups:
        st_scr[gi] = st_new[gi]

    inv_n = 1.0 / n
    mean = each(lambda yy: seg_sum(yy) * inv_n, y)
    d = each(lambda yy, m: yy - m, y, mean)
    var = each(lambda dd: seg_sum(dd * dd) * inv_n, d)
    bonus = each(lambda rr, kk_, ls, vv: seg_sum(rr * kk_ * rk_ref[:, ls]) * vv, r, k, lanes, v)
    for gi in groups:
        ls = lanes[gi]
        yn = d[gi] * lax.rsqrt(var[gi] + GN_EPS) * lng_ref[:, ls] + lnb_ref[:, ls]
        o_ref[:, ls] = (yn + bonus[gi]) * g_ref[:, ls]
    return st_new


def _wkv(r, k, v, kkr, a, lw, g, r_k, lnx_g, lnx_b, s0, *, batch, c_len, n_chunks, gps):
    rows, dim = r.shape
    hg = WKV_HG
    n = RWKV_HEAD_DIM
    w = hg * n
    wb = gps * w
    blk = pl.BlockSpec((c_len, wb), lambda b, gi, ci: (b * n_chunks + ci, gi))
    vec = pl.BlockSpec((1, wb), lambda b, gi, ci: (0, gi))
    st_spec = pl.BlockSpec((None, gps * hg, n, n), lambda b, gi, ci: (b, gi, 0, 0))
    in_specs = [blk] * 7 + [vec] * 3
    args = [r, k, v, kkr, a, lw, g, r_k.reshape(1, dim), lnx_g.reshape(1, dim), lnx_b.reshape(1, dim)]
    if s0 is not None:
        in_specs.append(st_spec)
        args.append(s0)
    rw = hg * c_len
    need = 2 * 8 * c_len * wb * 4 + gps * (5 * w * w * 4 + 16 * rw * w * 4 + 8 * rw * rw * 4)
    return pl.pallas_call(
        functools.partial(_wkv_kernel, c_len=c_len, hg=hg, gps=gps, n_chunks=n_chunks, has_s0=s0 is not None),
        grid=(batch, dim // wb, n_chunks),
        in_specs=in_specs,
        out_specs=[blk, st_spec],
        out_shape=[
            jax.ShapeDtypeStruct((rows, dim), F32),
            jax.ShapeDtypeStruct((batch, dim // n, n, n), F32),
        ],
        scratch_shapes=[pltpu.VMEM((gps, w, w), F32)],
        compiler_params=_params(("parallel", "parallel", "arbitrary"), need),
        name="wkv",
    )(*args)


def _rope_tables(pos):
    inv = ROPE_THETA ** (-jnp.arange(0, QK_ROPE_DIM, 2, dtype=F32) / QK_ROPE_DIM)
    ang = pos[:, None] * inv[None, :]
    cos, sin = jnp.cos(ang), jnp.sin(ang)
    return jnp.concatenate([cos] * 4, axis=1), jnp.concatenate([-sin, sin] * 2, axis=1)


def _swap_halves(w):
    half = w.shape[-1] // 2
    return jnp.concatenate([w[..., half:], w[..., :half]], axis=-1)


def _prepare_weights(w_in, w_uq, w_ukv, w_o, w_decay, w_iclr, w_gate):
    d = w_in.shape[0]
    rope = w_in[:, Q_LORA_RANK + KV_LORA_RANK:W_MLA_IN]
    rope_sw = _swap_halves(rope)
    zeros = lambda c: jnp.zeros((d, c), w_in.dtype)
    w_mla = jnp.concatenate(
        [w_in[:, :Q_LORA_RANK], zeros(MLA_KV_OFF - Q_LORA_RANK), w_in[:, Q_LORA_RANK:Q_LORA_RANK + KV_LORA_RANK],
         rope, rope, rope_sw, rope_sw, zeros(MLA_COLS - MLA_KV_OFF - KV_LORA_RANK - 4 * QK_ROPE_DIM)], axis=1).astype(BF16)
    w_rwkv = w_in[:, W_MLA_IN:].astype(BF16)
    uq = w_uq.reshape(Q_LORA_RANK, MLA_HEADS, QK_NOPE_DIM + QK_ROPE_DIM)
    uq_rope = uq[..., QK_NOPE_DIM:]
    w_q = jnp.concatenate(
        [uq[..., :QK_NOPE_DIM].reshape(Q_LORA_RANK, -1), uq_rope.reshape(Q_LORA_RANK, -1),
         _swap_halves(uq_rope).reshape(Q_LORA_RANK, -1)], axis=1).astype(BF16)
    ukv = w_ukv.reshape(KV_LORA_RANK, MLA_HEADS, QK_NOPE_DIM + V_HEAD_DIM)
    w_uk, w_uv = ukv[..., :QK_NOPE_DIM], ukv[..., QK_NOPE_DIM:]
    w_kvup = jnp.concatenate([w_uk.reshape(KV_LORA_RANK, -1), w_uv.reshape(KV_LORA_RANK, -1)], axis=1).astype(BF16)
    w_uk_t = jnp.transpose(w_uk, (1, 2, 0)).astype(BF16)
    w_uv_h = jnp.transpose(w_uv, (1, 0, 2)).astype(BF16)
    return dict(w_mla=w_mla, w_rwkv=w_rwkv, w_q=w_q, w_kvup=w_kvup, w_uk_t=w_uk_t, w_uv_h=w_uv_h,
                w_o=w_o.astype(BF16), w_decay=w_decay.astype(BF16), w_iclr=w_iclr.astype(BF16),
                w_gate=w_gate.astype(BF16))


def _group_layer(x, mods, tiles_per_group, tm, wts, p, *, cos4, sin4, attend, wkv_run, shift0, seq_len):
    sh1, sc1, g1, sh2, sc2, g2, sh3, sc3, g3 = mods
    x1 = _ffn(x, sh1, sc1, g1, wts["ffn1_in"], wts["ffn1_out"], p["ln_g"][0], p["ln_b"][0],
              tm=tm, tf=256, tiles_per_group=tiles_per_group)
    mla = _mm(x1, wts["w_mla"], tm=tm, tn=512, mod=(sh2, sc2), tiles_per_group=tiles_per_group, name="proj_mla")
    u = _mm(x1, wts["w_rwkv"], tm=tm, tn=512, mod=(sh2, sc2), tiles_per_group=tiles_per_group, name="proj_rwkv")
    ckv, kr, kr2 = _lat_prep(mla, p["g_kv"], cos4, sin4, tm=tm)
    qall = _mm(mla, wts["w_q"], tm=tm, tn=512, k=Q_LORA_RANK, rms_g=p["g_q"], name="q_proj")
    q_rope = _rope_q(qall, cos4, sin4, tm=tm)
    attn = attend(qall, q_rope, ckv, kr, kr2)
    prep = _rwkv_prep(u, shift0, p["mu_shift"], p["w0"], p["a0"], p["k_k"], p["k_a"],
                      wts["w_decay"], wts["w_iclr"], wts["w_gate"], tm=PREP_TM, seq_len=seq_len)
    rwkv, s_new = wkv_run(prep)
    x2 = _oproj(x1, g2, attn, rwkv, wts["w_o"], p["ln_g"][1], p["ln_b"][1], tm=tm, tk=256,
                tiles_per_group=tiles_per_group)
    y = _ffn(x2, sh3, sc3, g3, wts["ffn2_in"], wts["ffn2_out"], p["ln_g"][2], p["ln_b"][2],
             tm=tm, tf=256, tiles_per_group=tiles_per_group)
    return y, ckv, kr, s_new, u


def kernel(x_prompt, x_sample, c_prompt, c_sample, cache_kv_latent, cache_k_rope, state_wkv, state_shift, page_table, w_ada, b_ada, ln_g, ln_b, w_ffn1_in, w_ffn1_out, w_ffn2_in, w_ffn2_out, w_in, g_q, g_kv, w_uq, w_ukv, mu_shift, w0, w_decay, a0, w_iclr, w_gate, k_k, k_a, r_k, lnx_g, lnx_b, w_o):
    batch, seq, d = x_prompt.shape
    db, dseq, _ = x_sample.shape
    depth = w_ada.shape[0]
    assert depth == DEPTH == 1
    n_ada = w_ada.shape[2] // d
    past = page_table.shape[1] * PAGE_SIZE
    wu = state_shift.shape[2]

    c_all = jnp.concatenate([c_prompt, c_sample], axis=0)
    cos_p, sin_p = _rope_tables(jnp.arange(seq, dtype=F32) + 0)
    pos_s = jnp.tile(jnp.arange(dseq, dtype=F32) + past, db)
    cos_s, sin_s = _rope_tables(pos_s)

    yp = x_prompt.reshape(batch * seq, d)
    ys = x_sample.reshape(db * dseq, d)
    outs_p, outs_s = [], []
    for l in range(depth):
        p = dict(ln_g=ln_g[l], ln_b=ln_b[l], g_q=g_q[l], g_kv=g_kv[l], mu_shift=mu_shift[l], w0=w0[l], a0=a0[l],
                 k_k=k_k[l], k_a=k_a[l], r_k=r_k[l], lnx_g=lnx_g[l], lnx_b=lnx_b[l])
        wts = _prepare_weights(w_in[l], w_uq[l], w_ukv[l], w_o[l], w_decay[l], w_iclr[l], w_gate[l])
        wts.update(ffn1_in=w_ffn1_in[l].astype(BF16), ffn1_out=w_ffn1_out[l].astype(BF16),
                   ffn2_in=w_ffn2_in[l].astype(BF16), ffn2_out=w_ffn2_out[l].astype(BF16))
        ada = _ada(c_all, w_ada[l], b_ada[l]).reshape(batch + db, n_ada, d)
        mods_p = [ada[:batch, i][:, None, :] for i in range(n_ada)]
        mods_s = [jnp.repeat(ada[batch:, i], dseq, axis=0)[None] for i in range(n_ada)]

        tm_p = 512

        def attend_p(qall, q_rope, ckv, kr, kr2):
            kvup = _mm(ckv, wts["w_kvup"], tm=tm_p, tn=512, name="kv_up")
            return _flash(qall, q_rope, kvup, kr2, batch=batch, seq=seq, tq=512)

        def wkv_p(prep):
            return _wkv(*prep, p["r_k"], p["lnx_g"], p["lnx_b"], None, batch=batch, c_len=64, n_chunks=seq // 64,
                        gps=4)

        yp, ckv_p, kr_p, s_p, u_p = _group_layer(
            yp, mods_p, seq // tm_p, tm_p, wts, p, cos4=cos_p, sin4=sin_p, attend=attend_p, wkv_run=wkv_p,
            shift0=jnp.zeros((batch, 1, wu), F32), seq_len=seq)
        outs_p.append((ckv_p.reshape(batch, seq, -1), kr_p.reshape(batch, seq, -1), s_p,
                       u_p.reshape(batch, seq, wu)[:, -1]))

        tm_s = db * dseq
        c_pad = 16

        def attend_s(qall, q_rope, ckv, kr, kr2):
            rows = dseq * MLA_HEADS
            q_lat = _bmm(qall, wts["w_uk_t"], tm=tm_s, out_dtype=BF16, name="q_absorb")
            pad = lambda t: jnp.pad(t.reshape(db, dseq, -1), ((0, 0), (0, c_pad - dseq), (0, 0)))
            o_lat = _paged(page_table, q_lat.reshape(db, rows, KV_LORA_RANK), q_rope.reshape(db, rows, QK_ROPE_DIM),
                           pad(ckv), pad(kr), cache_kv_latent, jnp.swapaxes(cache_k_rope, 2, 3),
                           npg=PAGES_PER_STEP, new_len=dseq)
            return _bmm(o_lat.reshape(db * dseq, MLA_HEADS * KV_LORA_RANK), wts["w_uv_h"], tm=tm_s, name="v_up")

        def wkv_s(prep):
            padded = [jnp.pad(t.reshape(db, dseq, -1), ((0, 0), (0, c_pad - dseq), (0, 0))).reshape(db * c_pad, -1)
                      for t in prep]
            out, s_new = _wkv(*padded, p["r_k"], p["lnx_g"], p["lnx_b"], state_wkv[l], batch=db, c_len=c_pad,
                              n_chunks=1, gps=RWKV_HEADS // WKV_HG)
            return out.reshape(db, c_pad, -1)[:, :dseq].reshape(db * dseq, -1), s_new

        shift_rows = jnp.repeat(state_shift[l], dseq, axis=0).reshape(db * dseq // PREP_TM, PREP_TM, wu)
        ys, ckv_s, kr_s, s_s, u_s = _group_layer(
            ys, mods_s, 1, tm_s, wts, p, cos4=cos_s, sin4=sin_s, attend=attend_s, wkv_run=wkv_s,
            shift0=shift_rows, seq_len=dseq)
        outs_s.append((ckv_s.reshape(db, dseq, -1), kr_s.reshape(db, dseq, -1), s_s,
                       u_s.reshape(db, dseq, wu)[:, -1]))

    stack = lambda outs, i: jnp.stack([o[i] for o in outs])
    return (yp.reshape(batch, seq, d), ys.reshape(db, dseq, d),
            stack(outs_p, 0), stack(outs_p, 1), stack(outs_p, 2), stack(outs_p, 3),
            stack(outs_s, 0), stack(outs_s, 1), stack(outs_s, 2), stack(outs_s, 3))
```

```python
import functools
import math

import jax
import jax.numpy as jnp
from jax import lax
from jax.experimental import pallas as pl
from jax.experimental.pallas import tpu as pltpu

F32 = jnp.float32
BF16 = jnp.bfloat16

QK_NOPE_DIM = 128
QK_ROPE_DIM = 64
V_HEAD_DIM = 128
MLA_HEADS = 16
Q_LORA_RANK = 896
KV_LORA_RANK = 512
RWKV_HEAD_DIM = 64
RWKV_HEADS = 32
RWKV_DIM = RWKV_HEADS * RWKV_HEAD_DIM
DECAY_LORA = 128
ICLR_LORA = 128
GATE_LORA = 256
W_MLA_IN = Q_LORA_RANK + KV_LORA_RANK + QK_ROPE_DIM
PAGE_SIZE = 128
ROPE_THETA = 10000.0
SM_SCALE = (QK_NOPE_DIM + QK_ROPE_DIM) ** -0.5
DEPTH = 1
ALPHA = (2 * DEPTH) ** 0.25
LN_EPS = 1e-5
RMS_EPS = 1e-6
GN_EPS = 64e-5

V7X_VMEM_BYTES = 64 * 1024 * 1024
V7X_LANES = 128
SUBLANES = 8
MIB = 1024 * 1024

MLA_COLS = 2048
MLA_KV_OFF = 1024

ROW_CHUNK = 64
N_CHUNK = 512
PREP_TM = 128
PAGES_PER_STEP = 32
WKV_HG = 4


def _params(sem, vmem_bytes):
    limit = min(int(vmem_bytes) + 8 * MIB, V7X_VMEM_BYTES - 6 * MIB)
    return pltpu.CompilerParams(dimension_semantics=sem, vmem_limit_bytes=limit)


def _dot(a, b):
    return jnp.dot(a, b, preferred_element_type=F32)


def _dot_nt(a, b):
    return lax.dot_general(a, b, (((1,), (1,)), ((), ())), preferred_element_type=F32)


def _dot_tn(a, b):
    return lax.dot_general(a, b, (((0,), (0,)), ((), ())), preferred_element_type=F32)


def _split_bf16(x, parts):
    out = []
    rem = x
    for _ in range(parts):
        h = rem.astype(BF16)
        out.append(h)
        rem = rem - h.astype(F32)
    return out


def _layernorm_rows(z, g, b):
    mu = jnp.mean(z, axis=-1, keepdims=True)
    d = z - mu
    var = jnp.mean(d * d, axis=-1, keepdims=True)
    return d * lax.rsqrt(var + LN_EPS) * g + b


def _for_row_chunks(n_rows, fn, rows=ROW_CHUNK):
    def body(i, carry):
        fn(pl.ds(pl.multiple_of(i * rows, rows), rows))
        return carry

    lax.fori_loop(0, n_rows // rows, body, 0)


def _rows_of(ref, rs):
    mr = ref.shape[0]
    if mr == 1:
        return ref[...]
    return ref[pl.ds(pl.multiple_of(rs.start % mr, rs.size), rs.size), :]


def _modulate_into(h_scr, x_ref, sh_ref, sc_ref):
    def chunk(rs):
        h_scr[rs, :] = (x_ref[rs, :] * (1.0 + _rows_of(sc_ref, rs)) + _rows_of(sh_ref, rs)).astype(BF16)

    _for_row_chunks(h_scr.shape[0], chunk)


def _residual_layernorm(o_ref, x_ref, gate_ref, lng_ref, lnb_ref, gate_scale):
    def chunk(rs):
        z = ALPHA * x_ref[rs, :] + gate_scale * _rows_of(gate_ref, rs) * o_ref[rs, :]
        o_ref[rs, :] = _layernorm_rows(z, lng_ref[...], lnb_ref[...])

    _for_row_chunks(o_ref.shape[0], chunk)


def _ada_kernel(c_ref, w_ref, b_ref, o_ref):
    c = c_ref[...]
    a = (c * jax.nn.sigmoid(c)).astype(BF16)
    o_ref[...] = _dot(a, w_ref[...].astype(BF16)) + b_ref[...]


def _ada(c_all, w_ada, b_ada, tn=512):
    m, k = c_all.shape
    n = w_ada.shape[1]
    need = 2 * (k * tn * 4) + k * tn * 2 + 2 * m * k * 4 + 4 * m * tn * 4
    return pl.pallas_call(
        _ada_kernel,
        grid=(n // tn,),
        in_specs=[
            pl.BlockSpec((m, k), lambda j: (0, 0)),
            pl.BlockSpec((k, tn), lambda j: (0, j)),
            pl.BlockSpec((1, tn), lambda j: (0, j)),
        ],
        out_specs=pl.BlockSpec((m, tn), lambda j: (0, j)),
        out_shape=jax.ShapeDtypeStruct((m, n), F32),
        compiler_params=_params(("parallel",), need),
        name="ada",
    )(c_all, w_ada, b_ada.reshape(1, n))


def _ffn_kernel(x_ref, sh_ref, sc_ref, g_ref, wg_ref, wu_ref, wo_ref, lng_ref, lnb_ref, o_ref, *rest, nj):
    h_scr = rest[-1]
    j = pl.program_id(1)

    @pl.when(j == 0)
    def _():
        _modulate_into(h_scr, x_ref, sh_ref, sc_ref)
        o_ref[...] = jnp.zeros_like(o_ref)

    wg, wu, wo = wg_ref[...], wu_ref[...], wo_ref[...]
    if len(rest) > 1:
        wg, wu, wo = wg.astype(BF16), wu.astype(BF16), wo.astype(BF16)
        rest[0][...] = wg
        rest[1][...] = wu
        rest[2][...] = wo
    h = h_scr[...]
    gate = _dot(h, wg)
    up = _dot(h, wu)
    act = (gate * jax.nn.sigmoid(gate) * up).astype(BF16)
    for n0 in range(0, o_ref.shape[1], N_CHUNK):
        o_ref[:, n0:n0 + N_CHUNK] += _dot(act, wo[:, n0:n0 + N_CHUNK])

    @pl.when(j == nj - 1)
    def _():
        _residual_layernorm(o_ref, x_ref, g_ref, lng_ref, lnb_ref, 0.5)


def _ffn(x, sh, sc, g, w_in, w_out, ln_g, ln_b, *, tm, tf, tiles_per_group, emit_bf16_weights=False):
    r, d = x.shape
    ff = w_out.shape[0]
    nj = ff // tf
    mr = sh.shape[1]
    w_gate, w_up = w_in if isinstance(w_in, tuple) else (w_in, w_in)
    up_off = 0 if isinstance(w_in, tuple) else nj
    wbytes = w_out.dtype.itemsize
    mod_spec = pl.BlockSpec((None, mr, d), lambda i, j: (i // tiles_per_group, 0, 0))
    out_specs = [pl.BlockSpec((tm, d), lambda i, j: (i, 0))]
    out_shape = [jax.ShapeDtypeStruct((r, d), F32)]
    need = 4 * tm * d * 4 + tm * d * 2 + 2 * 3 * d * tf * wbytes + 6 * mr * d * 4 + 6 * tm * tf * 4
    if emit_bf16_weights:
        assert r == tm and w_out.dtype == F32
        out_specs += [pl.BlockSpec((d, tf), lambda i, j: (0, j)), pl.BlockSpec((d, tf), lambda i, j: (0, j)),
                      pl.BlockSpec((tf, d), lambda i, j: (j, 0))]
        out_shape += [jax.ShapeDtypeStruct((d, ff), BF16), jax.ShapeDtypeStruct((d, ff), BF16),
                      jax.ShapeDtypeStruct((ff, d), BF16)]
        need += 3 * 3 * d * tf * 2
    outs = pl.pallas_call(
        functools.partial(_ffn_kernel, nj=nj),
        grid=(r // tm, nj),
        in_specs=[
            pl.BlockSpec((tm, d), lambda i, j: (i, 0)),
            mod_spec, mod_spec, mod_spec,
            pl.BlockSpec((d, tf), lambda i, j: (0, j)),
            pl.BlockSpec((d, tf), lambda i, j: (0, up_off + j)),
            pl.BlockSpec((tf, d), lambda i, j: (j, 0)),
            pl.BlockSpec((1, d), lambda i, j: (0, 0)),
            pl.BlockSpec((1, d), lambda i, j: (0, 0)),
        ],
        out_specs=out_specs,
        out_shape=out_shape,
        scratch_shapes=[pltpu.VMEM((tm, d), BF16)],
        compiler_params=_params(("parallel", "arbitrary"), need),
        name="ffn",
    )(x, sh, sc, g, w_gate, w_up, w_out, ln_g.reshape(1, d), ln_b.reshape(1, d))
    return (outs[0], (outs[1], outs[2]), outs[3]) if emit_bf16_weights else outs[0]


def _mm_mod_kernel(a_ref, sh_ref, sc_ref, w_ref, o_ref, h_scr):
    @pl.when(pl.program_id(1) == 0)
    def _():
        _modulate_into(h_scr, a_ref, sh_ref, sc_ref)

    o_ref[...] = _dot(h_scr[...], w_ref[...]).astype(o_ref.dtype)


def _mm_rms_kernel(a_ref, g_ref, w_ref, o_ref, h_scr):
    @pl.when(pl.program_id(1) == 0)
    def _():
        a = a_ref[...]
        ms = jnp.mean(a * a, axis=-1, keepdims=True)
        h_scr[...] = (a * lax.rsqrt(ms + RMS_EPS) * g_ref[...]).astype(BF16)

    o_ref[...] = _dot(h_scr[...], w_ref[...]).astype(o_ref.dtype)


def _mm_plain_kernel(a_ref, w_ref, o_ref, h_scr):
    @pl.when(pl.program_id(1) == 0)
    def _():
        h_scr[...] = a_ref[...].astype(BF16)

    o_ref[...] = _dot(h_scr[...], w_ref[...]).astype(o_ref.dtype)


def _mm(a, w, *, tm, tn, k=None, a_colblk=0, mod=None, rms_g=None, tiles_per_group=1, out_dtype=F32, name="mm"):
    r = a.shape[0]
    k = a.shape[1] if k is None else k
    n = w.shape[1]
    a_spec = pl.BlockSpec((tm, k), lambda i, j: (i, a_colblk))
    w_spec = pl.BlockSpec((k, tn), lambda i, j: (0, j))
    need = 2 * tm * k * a.dtype.itemsize + tm * k * 2 + 2 * k * tn * 2 + 4 * tm * tn * 4
    if mod is not None:
        sh, sc = mod
        mr = sh.shape[1]
        mod_spec = pl.BlockSpec((None, mr, k), lambda i, j: (i // tiles_per_group, 0, 0))
        kern, ins, args = _mm_mod_kernel, [a_spec, mod_spec, mod_spec, w_spec], (a, sh, sc, w)
        need += 4 * mr * k * 4
    elif rms_g is not None:
        kern, ins, args = _mm_rms_kernel, [a_spec, pl.BlockSpec((1, k), lambda i, j: (0, 0)), w_spec], (a, rms_g.reshape(1, k), w)
    else:
        kern, ins, args = _mm_plain_kernel, [a_spec, w_spec], (a, w)
    return pl.pallas_call(
        kern,
        grid=(r // tm, n // tn),
        in_specs=ins,
        out_specs=pl.BlockSpec((tm, tn), lambda i, j: (i, j)),
        out_shape=jax.ShapeDtypeStruct((r, n), out_dtype),
        scratch_shapes=[pltpu.VMEM((tm, k), BF16)],
        compiler_params=_params(("parallel", "arbitrary"), need),
        name=name,
    )(*args)


def _bmm_kernel(a_ref, w_ref, o_ref):
    o_ref[...] = _dot(a_ref[...].astype(BF16), w_ref[...]).astype(o_ref.dtype)


def _bmm(a, w, *, tm, out_dtype=F32, name="bmm"):
    r = a.shape[0]
    h, ka, nb = w.shape
    need = 2 * tm * ka * 4 + 2 * ka * nb * 2 + 2 * tm * nb * 4
    return pl.pallas_call(
        _bmm_kernel,
        grid=(r // tm, h),
        in_specs=[
            pl.BlockSpec((tm, ka), lambda i, hh: (i, hh)),
            pl.BlockSpec((None, ka, nb), lambda i, hh: (hh, 0, 0)),
        ],
        out_specs=pl.BlockSpec((tm, nb), lambda i, hh: (i, hh)),
        out_shape=jax.ShapeDtypeStruct((r, h * nb), out_dtype),
        compiler_params=_params(("parallel", "parallel"), need),
        name=name,
    )(a, w)


def _oproj_kernel(x_ref, g_ref, a1_ref, a2_ref, w1_ref, w2_ref, lng_ref, lnb_ref, o_ref, *, nk):
    kk = pl.program_id(1)

    @pl.when(kk == 0)
    def _():
        o_ref[...] = jnp.zeros_like(o_ref)

    a1 = a1_ref[...].astype(BF16)
    a2 = a2_ref[...].astype(BF16)
    for n0 in range(0, o_ref.shape[1], N_CHUNK):
        cols = slice(n0, n0 + N_CHUNK)
        o_ref[:, cols] += _dot(a1, w1_ref[:, cols]) + _dot(a2, w2_ref[:, cols])

    @pl.when(kk == nk - 1)
    def _():
        _residual_layernorm(o_ref, x_ref, g_ref, lng_ref, lnb_ref, 1.0)


def _oproj(x, g, attn, rwkv, w_o, ln_g, ln_b, *, tm, tk, tiles_per_group):
    r, d = x.shape
    half = attn.shape[1]
    nk = half // tk
    mr = g.shape[1]
    need = 4 * tm * d * 4 + 2 * mr * d * 4 + 4 * tm * tk * 4 + 4 * tk * d * 2
    return pl.pallas_call(
        functools.partial(_oproj_kernel, nk=nk),
        grid=(r // tm, nk),
        in_specs=[
            pl.BlockSpec((tm, d), lambda i, kk: (i, 0)),
            pl.BlockSpec((None, mr, d), lambda i, kk: (i // tiles_per_group, 0, 0)),
            pl.BlockSpec((tm, tk), lambda i, kk: (i, kk)),
            pl.BlockSpec((tm, tk), lambda i, kk: (i, kk)),
            pl.BlockSpec((tk, d), lambda i, kk: (kk, 0)),
            pl.BlockSpec((tk, d), lambda i, kk: (nk + kk, 0)),
            pl.BlockSpec((1, d), lambda i, kk: (0, 0)),
            pl.BlockSpec((1, d), lambda i, kk: (0, 0)),
        ],
        out_specs=pl.BlockSpec((tm, d), lambda i, kk: (i, 0)),
        out_shape=jax.ShapeDtypeStruct((r, d), F32),
        compiler_params=_params(("parallel", "arbitrary"), need),
        name="oproj",
    )(x, g, attn, rwkv, w_o, w_o, ln_g.reshape(1, d), ln_b.reshape(1, d))


def _lat_kernel(m_ref, g_ref, cos_ref, sin_ref, ckv_ref, kr_ref, kr2_ref):
    m = m_ref[...]
    kv = m[:, :KV_LORA_RANK]
    ms = jnp.mean(kv * kv, axis=-1, keepdims=True)
    ckv_ref[...] = kv * lax.rsqrt(ms + RMS_EPS) * g_ref[...]
    a2 = m[:, KV_LORA_RANK:KV_LORA_RANK + V7X_LANES]
    b2 = m[:, KV_LORA_RANK + V7X_LANES:KV_LORA_RANK + 2 * V7X_LANES]
    kr2 = a2 * cos_ref[...] + b2 * sin_ref[...]
    kr2_ref[...] = kr2
    kr_ref[...] = kr2[:, :QK_ROPE_DIM]


def _lat_prep(mla, g_kv, cos4, sin4, *, tm):
    r = mla.shape[0]
    nb = cos4.shape[0] // tm
    wblk = MLA_COLS - MLA_KV_OFF
    need = 2 * tm * wblk * 4 + 2 * tm * (KV_LORA_RANK + 3 * V7X_LANES + 2 * V7X_LANES) * 4
    return pl.pallas_call(
        _lat_kernel,
        grid=(r // tm,),
        in_specs=[
            pl.BlockSpec((tm, wblk), lambda i: (i, MLA_KV_OFF // wblk)),
            pl.BlockSpec((1, KV_LORA_RANK), lambda i: (0, 0)),
            pl.BlockSpec((tm, V7X_LANES), lambda i: (i % nb, 0)),
            pl.BlockSpec((tm, V7X_LANES), lambda i: (i % nb, 0)),
        ],
        out_specs=[
            pl.BlockSpec((tm, KV_LORA_RANK), lambda i: (i, 0)),
            pl.BlockSpec((tm, QK_ROPE_DIM), lambda i: (i, 0)),
            pl.BlockSpec((tm, V7X_LANES), lambda i: (i, 0)),
        ],
        out_shape=[
            jax.ShapeDtypeStruct((r, KV_LORA_RANK), F32),
            jax.ShapeDtypeStruct((r, QK_ROPE_DIM), F32),
            jax.ShapeDtypeStruct((r, V7X_LANES), F32),
        ],
        compiler_params=_params(("parallel",), need),
        name="lat_prep",
    )(mla, g_kv.reshape(1, KV_LORA_RANK), cos4, sin4)


def _ropeq_kernel(a_ref, b_ref, cos_ref, sin_ref, o_ref):
    reps = a_ref.shape[1] // V7X_LANES
    c = jnp.concatenate([cos_ref[...]] * reps, axis=1)
    s = jnp.concatenate([sin_ref[...]] * reps, axis=1)
    o_ref[...] = (a_ref[...] * c + b_ref[...] * s).astype(o_ref.dtype)


def _rope_q(qall, cos4, sin4, *, tm):
    r = qall.shape[0]
    w = MLA_HEADS * QK_ROPE_DIM
    nb = cos4.shape[0] // tm
    need = 4 * tm * w * 4 + 2 * tm * w * 2 + 4 * tm * w * 4
    return pl.pallas_call(
        _ropeq_kernel,
        grid=(r // tm,),
        in_specs=[
            pl.BlockSpec((tm, w), lambda i: (i, 2)),
            pl.BlockSpec((tm, w), lambda i: (i, 3)),
            pl.BlockSpec((tm, V7X_LANES), lambda i: (i % nb, 0)),
            pl.BlockSpec((tm, V7X_LANES), lambda i: (i % nb, 0)),
        ],
        out_specs=pl.BlockSpec((tm, w), lambda i: (i, 0)),
        out_shape=jax.ShapeDtypeStruct((r, w), F32),
        compiler_params=_params(("parallel",), need),
        name="rope_q",
    )(qall, qall, cos4, sin4)


def _flash_kernel(qn_ref, qr_ref, kn_ref, kr_ref, v_ref, o_ref, m_scr, l_scr, acc_scr, *, tq, tk):
    qi = pl.program_id(2)
    ki = pl.program_id(3)

    @pl.when(ki == 0)
    def _():
        m_scr[...] = jnp.full_like(m_scr, -jnp.inf)
        l_scr[...] = jnp.zeros_like(l_scr)
        acc_scr[...] = jnp.zeros_like(acc_scr)

    def step(on_diagonal):
        qn = (qn_ref[...] * SM_SCALE).astype(BF16)
        qr = qr_ref[...] * SM_SCALE
        kn = kn_ref[...].astype(BF16)
        kr2 = kr_ref[...].astype(BF16)
        v = v_ref[...].astype(BF16)
        lane = lax.broadcasted_iota(jnp.int32, (1, V7X_LANES), 1)
        if on_diagonal:
            causal = lax.broadcasted_iota(jnp.int32, (tq, 1), 0) >= lax.broadcasted_iota(jnp.int32, (1, tk), 1)
        heads = range(2)
        hs = [slice(e * QK_NOPE_DIM, (e + 1) * QK_NOPE_DIM) for e in heads]
        qre = [jnp.where((lane >= QK_ROPE_DIM) if e else (lane < QK_ROPE_DIM), qr, 0.0).astype(BF16) for e in heads]
        s = [_dot_nt(qn[:, hs[e]], kn[:, hs[e]]) + _dot_nt(qre[e], kr2) for e in heads]
        if on_diagonal:
            s = [jnp.where(causal, s[e], -jnp.inf) for e in heads]
        m_prev = [m_scr[e] for e in heads]
        m_new = [jnp.maximum(m_prev[e], jnp.max(s[e], axis=-1, keepdims=True)) for e in heads]
        corr = [jnp.exp(m_prev[e] - m_new[e]) for e in heads]
        p = [jnp.exp(s[e] - m_new[e]) for e in heads]
        l_new = [l_scr[e] * corr[e] + jnp.sum(p[e], axis=-1, keepdims=True) for e in heads]
        acc_new = [acc_scr[e] * corr[e] + _dot(p[e].astype(BF16), v[:, hs[e]]) for e in heads]
        for e in heads:
            if on_diagonal:
                o_ref[:, e * V_HEAD_DIM:(e + 1) * V_HEAD_DIM] = acc_new[e] / l_new[e]
            else:
                l_scr[e] = l_new[e]
                acc_scr[e] = acc_new[e]
                m_scr[e] = m_new[e]

    @pl.when(ki < qi)
    def _():
        step(False)

    @pl.when(ki == qi)
    def _():
        step(True)


def _flash(qall, q_rope, kvup, kr2, *, batch, seq, tq):
    tk = tq
    nq = seq // tq
    r = qall.shape[0]
    hp = MLA_HEADS // 2
    need = 2 * (tq * 256 * 4 + tq * 128 * 2 + 2 * tk * 256 * 4 + tk * 128 * 4 + tq * 256 * 4) + 4 * tq * 128 * 4 + 8 * tq * tk * 4
    return pl.pallas_call(
        functools.partial(_flash_kernel, tq=tq, tk=tk),
        grid=(batch, hp, nq, nq),
        in_specs=[
            pl.BlockSpec((tq, 2 * QK_NOPE_DIM), lambda b, h, qi, ki: (b * nq + qi, h)),
            pl.BlockSpec((tq, V7X_LANES), lambda b, h, qi, ki: (b * nq + qi, h)),
            pl.BlockSpec((tk, 2 * QK_NOPE_DIM), lambda b, h, qi, ki: (b * nq + jnp.minimum(ki, qi), h)),
            pl.BlockSpec((tk, V7X_LANES), lambda b, h, qi, ki: (b * nq + jnp.minimum(ki, qi), 0)),
            pl.BlockSpec((tk, 2 * V_HEAD_DIM), lambda b, h, qi, ki: (b * nq + jnp.minimum(ki, qi), hp + h)),
        ],
        out_specs=pl.BlockSpec((tq, 2 * V_HEAD_DIM), lambda b, h, qi, ki: (b * nq + qi, h)),
        out_shape=jax.ShapeDtypeStruct((r, MLA_HEADS * V_HEAD_DIM), F32),
        scratch_shapes=[
            pltpu.VMEM((2, tq, 1), F32),
            pltpu.VMEM((2, tq, 1), F32),
            pltpu.VMEM((2, tq, V_HEAD_DIM), F32),
        ],
        compiler_params=_params(("parallel", "parallel", "parallel", "arbitrary"), need),
        name="flash",
    )(qall, q_rope, kvup, kr2, kvup)


def _paged_kernel(pt_ref, ql_ref, qr_ref, cn_ref, krn_ref, lat_hbm, krt_hbm, o_ref,
                  lat_buf, krt_buf, sem, m_scr, l_scr, acc_scr, *, npg, nchunk, new_len, heads):
    c = pl.program_id(1)
    step = pl.program_id(0) * nchunk + c
    n_steps = pl.num_programs(0) * nchunk
    slot = step % 2

    def page_copies(step_, slot_):
        cps = []
        for i in range(npg):
            page = pt_ref[step_ * npg + i]
            cps.append(pltpu.make_async_copy(lat_hbm.at[0, page], lat_buf.at[slot_, i], sem.at[0, slot_]))
            cps.append(pltpu.make_async_copy(krt_hbm.at[0, page], krt_buf.at[slot_, i], sem.at[1, slot_]))
        return cps

    @pl.when(step == 0)
    def _():
        for cp in page_copies(step, slot):
            cp.start()

    @pl.when(step + 1 < n_steps)
    def _():
        for cp in page_copies(step + 1, 1 - slot):
            cp.start()

    @pl.when(c == 0)
    def _():
        m_scr[...] = jnp.full_like(m_scr, -jnp.inf)
        l_scr[...] = jnp.zeros_like(l_scr)
        acc_scr[...] = jnp.zeros_like(acc_scr)

    ql = ql_ref[...]
    qr = qr_ref[...].astype(BF16)

    def update(s, vals):
        m_prev = m_scr[...]
        m_new = jnp.maximum(m_prev, jnp.max(s, axis=-1, keepdims=True))
        corr = jnp.exp(m_prev - m_new)
        p = jnp.exp(s - m_new)
        l_scr[...] = l_scr[...] * corr + jnp.sum(p, axis=-1, keepdims=True)
        w = s.shape[1] // len(vals)
        pv = _dot(p[:, :w].astype(BF16), vals[0])
        for i in range(1, len(vals)):
            pv += _dot(p[:, i * w:(i + 1) * w].astype(BF16), vals[i])
        acc_scr[...] = acc_scr[...] * corr + pv
        m_scr[...] = m_new

    for cp in page_copies(step, slot):
        cp.wait()
    kls = [lat_buf[slot, i].astype(BF16) for i in range(npg)]
    s = jnp.concatenate(
        [_dot_nt(ql, kls[i]) + _dot(qr, krt_buf[slot, i].astype(BF16)) for i in range(npg)], axis=1)
    update(s * SM_SCALE, kls)

    @pl.when(c == nchunk - 1)
    def _():
        kn = cn_ref[...].astype(BF16)
        s2 = (_dot_nt(ql, kn) + _dot_nt(qr, krn_ref[...].astype(BF16))) * SM_SCALE
        rows = s2.shape[0]
        t_row = lax.broadcasted_iota(jnp.int32, (rows, 1), 0) // heads
        t_col = lax.broadcasted_iota(jnp.int32, (1, s2.shape[1]), 1)
        ok = (t_col <= t_row) & (t_col < new_len)
        update(jnp.where(ok, s2, -jnp.inf), [kn])
        o_ref[...] = acc_scr[...] / l_scr[...]


def _paged(page_table, q_lat, q_rope, ckv_new, kr_new, cache_lat, cache_kr_t, *, npg, new_len):
    db, rows, lat = q_lat.shape
    n_pages = page_table.shape[1]
    nchunk = n_pages // npg
    npad = ckv_new.shape[1]
    assert n_pages == nchunk * npg
    pt_flat = page_table.reshape(-1)
    in_specs = [
        pl.BlockSpec((None, rows, lat), lambda b, c, pt: (b, 0, 0)),
        pl.BlockSpec((None, rows, QK_ROPE_DIM), lambda b, c, pt: (b, 0, 0)),
        pl.BlockSpec((None, npad, lat), lambda b, c, pt: (b, 0, 0)),
        pl.BlockSpec((None, npad, QK_ROPE_DIM), lambda b, c, pt: (b, 0, 0)),
        pl.BlockSpec(memory_space=pl.ANY),
        pl.BlockSpec(memory_space=pl.ANY),
    ]
    need = (2 * npg * PAGE_SIZE * (lat + QK_ROPE_DIM) * 4 + npg * PAGE_SIZE * lat * 2
            + 6 * rows * npg * PAGE_SIZE * 4 + 8 * rows * lat * 4)
    grid_spec = pltpu.PrefetchScalarGridSpec(
        num_scalar_prefetch=1,
        grid=(db, nchunk),
        in_specs=in_specs,
        out_specs=pl.BlockSpec((None, rows, lat), lambda b, c, pt: (b, 0, 0)),
        scratch_shapes=[
            pltpu.VMEM((2, npg, PAGE_SIZE, lat), F32),
            pltpu.VMEM((2, npg, QK_ROPE_DIM, PAGE_SIZE), F32),
            pltpu.SemaphoreType.DMA((2, 2)),
            pltpu.VMEM((rows, 1), F32),
            pltpu.VMEM((rows, 1), F32),
            pltpu.VMEM((rows, lat), F32),
        ],
    )
    return pl.pallas_call(
        functools.partial(_paged_kernel, npg=npg, nchunk=nchunk, new_len=new_len, heads=MLA_HEADS),
        grid_spec=grid_spec,
        out_shape=jax.ShapeDtypeStruct((db, rows, lat), F32),
        compiler_params=_params(("arbitrary", "arbitrary"), need),
        name="paged",
    )(pt_flat, q_lat, q_rope, ckv_new, kr_new, cache_lat, cache_kr_t)


def _prep_kernel(u_ref, tail_ref, s0_ref, mu_ref, w0_ref, a0_ref, kk_ref, ka_ref, wd_ref, wi_ref, wg_ref,
                 r_ref, k_ref, v_ref, kkr_ref, a_ref, lw_ref, g_ref, *, seq_len):
    tm = u_ref.shape[0]
    u = u_ref[...]
    row = lax.broadcasted_iota(jnp.int32, (tm, 1), 0)
    prev = jnp.where(row == 0, tail_ref[SUBLANES - 1:SUBLANES, :], pltpu.roll(u, 1, 0))
    starts = ((pl.program_id(0) * tm + row) & (seq_len - 1)) == 0
    prev = jnp.where(starts, s0_ref[...], prev)
    um = u + (prev - u) * mu_ref[...]
    n = RWKV_DIM
    k = um[:, n:2 * n]
    dw = um[:, 3 * n:3 * n + DECAY_LORA]
    da = um[:, 3 * n + DECAY_LORA:3 * n + DECAY_LORA + ICLR_LORA]
    dg = um[:, 3 * n + DECAY_LORA + ICLR_LORA:]
    z = -(w0_ref[...] + _dot(jnp.tanh(dw).astype(BF16), wd_ref[...]))
    softplus = jnp.maximum(z, 0.0) + jnp.log1p(jnp.exp(-jnp.abs(z)))
    lw_ref[...] = -jnp.exp(-softplus - 0.5)
    a = jax.nn.sigmoid(a0_ref[...] + _dot(da.astype(BF16), wi_ref[...]))
    a_ref[...] = a
    g_ref[...] = _dot(jax.nn.sigmoid(dg).astype(BF16), wg_ref[...])
    r_ref[...] = um[:, :n]
    v_ref[...] = um[:, 2 * n:3 * n]
    kkr_ref[...] = k * kk_ref[...]
    k_ref[...] = k * (1.0 + (a - 1.0) * ka_ref[...])


def _rwkv_prep(u, shift0, mu, w0, a0, k_k, k_a, w_decay, w_iclr, w_gate, *, tm, seq_len):
    r, wu = u.shape
    n = RWKV_DIM
    assert seq_len & (seq_len - 1) == 0
    tiles_per_group = max(seq_len // tm, 1)
    mr = shift0.shape[1]
    row = lambda width: pl.BlockSpec((1, width), lambda i: (0, 0))
    full = lambda arr: pl.BlockSpec(arr.shape, lambda i: (0, 0))
    out_spec = pl.BlockSpec((tm, n), lambda i: (i, 0))
    need = 2 * tm * wu * 4 + 2 * mr * wu * 4 + 2 * 7 * tm * n * 4 + 6 * tm * wu * 4
    return pl.pallas_call(
        functools.partial(_prep_kernel, seq_len=seq_len),
        grid=(r // tm,),
        in_specs=[
            pl.BlockSpec((tm, wu), lambda i: (i, 0)),
            pl.BlockSpec((SUBLANES, wu), lambda i: (jnp.maximum(i * (tm // SUBLANES) - 1, 0), 0)),
            pl.BlockSpec((None, mr, wu), lambda i: (i // tiles_per_group, 0, 0)),
            row(wu), row(n), row(n), row(n), row(n),
            full(w_decay), full(w_iclr), full(w_gate),
        ],
        out_specs=[out_spec] * 7,
        out_shape=[jax.ShapeDtypeStruct((r, n), F32)] * 7,
        compiler_params=_params(("parallel",), need),
        name="rwkv_prep",
    )(u, u, shift0, mu.reshape(1, wu), w0.reshape(1, n), a0.reshape(1, n), k_k.reshape(1, n), k_a.reshape(1, n),
      w_decay, w_iclr, w_gate)


def _wkv_kernel(*refs, c_len, hg, gps, n_chunks, has_s0):
    if has_s0:
        (r_ref, k_ref, v_ref, kkr_ref, a_ref, lw_ref, g_ref, rk_ref, lng_ref, lnb_ref, s0_ref,
         o_ref, sout_ref, st_scr) = refs
    else:
        (r_ref, k_ref, v_ref, kkr_ref, a_ref, lw_ref, g_ref, rk_ref, lng_ref, lnb_ref,
         o_ref, sout_ref, st_scr) = refs
    n = RWKV_HEAD_DIM
    w = hg * n
    rw = hg * c_len
    shift = int(math.log2(n))
    ci = pl.program_id(2)

    lane_head = lax.broadcasted_iota(jnp.int32, (1, w), 1) >> shift
    key_head = lax.broadcasted_iota(jnp.int32, (w, 1), 0) >> shift
    same_head = key_head == lane_head
    row_head = lax.broadcasted_iota(jnp.int32, (rw, 1), 0) // c_len
    stack_mask = row_head == lane_head

    @pl.when(ci == 0)
    def _():
        for gi in range(gps):
            if has_s0:
                s0t = jnp.transpose(s0_ref[gi * hg:(gi + 1) * hg].reshape(w, n))
                st_scr[gi] = jnp.where(same_head, jnp.concatenate([s0t] * hg, axis=0), 0.0)
            else:
                st_scr[gi] = jnp.zeros((w, w), F32)

    new_states = _wkv_groups(gps, r_ref, k_ref, v_ref, kkr_ref, a_ref, lw_ref, g_ref, rk_ref, lng_ref, lnb_ref,
                             o_ref, st_scr, c_len=c_len, hg=hg, same_head=same_head, stack_mask=stack_mask)

    @pl.when(ci == n_chunks - 1)
    def _():
        for gi, st_new in enumerate(new_states):
            z = st_new[0:n]
            for e in range(1, hg):
                z = z + st_new[e * n:(e + 1) * n]
            sout_ref[gi * hg:(gi + 1) * hg] = jnp.transpose(z).reshape(hg, n, n)


def _wkv_groups(gps, r_ref, k_ref, v_ref, kkr_ref, a_ref, lw_ref, g_ref, rk_ref, lng_ref, lnb_ref, o_ref, st_scr, *,
                c_len, hg, same_head, stack_mask):
    n = RWKV_HEAD_DIM
    w = hg * n
    rw = hg * c_len
    groups = range(gps)
    lanes = [slice(gi * w, (gi + 1) * w) for gi in groups]
    each = lambda fn, *cols: [fn(*xs) for xs in zip(*cols)]

    seg_ones = jnp.where(same_head, 1.0, 0.0).astype(BF16)

    def seg_sum(x):
        hi, lo = _split_bf16(x, 2)
        return _dot(hi, seg_ones) + _dot(lo, seg_ones)

    def stack(x):
        return jnp.where(stack_mask, jnp.concatenate([x] * hg, axis=0), 0.0).astype(BF16)

    t_r = lax.broadcasted_iota(jnp.int32, (c_len, 1), 0)
    t_c = lax.broadcasted_iota(jnp.int32, (1, c_len), 1)
    tril = jnp.where(t_c <= t_r, 1.0, 0.0).astype(BF16)
    ri = lax.broadcasted_iota(jnp.int32, (rw, 1), 0)
    cj = lax.broadcasted_iota(jnp.int32, (1, rw), 1)
    strict = cj < ri
    incl = cj <= ri

    r = [r_ref[:, ls] for ls in lanes]
    k = [k_ref[:, ls] for ls in lanes]
    v = [v_ref[:, ls] for ls in lanes]
    a = [a_ref[:, ls] for ls in lanes]
    lw = [lw_ref[:, ls] for ls in lanes]
    kkr = [kkr_ref[:, ls] for ls in lanes]

    def cumsum(x):
        l1, l2, l3 = _split_bf16(x, 3)
        return _dot(tril, l1) + _dot(tril, l2) + _dot(tril, l3)

    cum = each(cumsum, lw)
    cum_end = [c[c_len - 1:c_len, :] for c in cum]
    kkn = each(lambda x: x / jnp.maximum(jnp.sqrt(seg_sum(x * x)), 1e-12), kkr)
    kka = each(lambda x, y: x * y, kkn, a)
    p_inv = each(lambda c: jnp.exp(-c), cum)
    p_tail = each(lambda ce, c: jnp.exp(ce - c), cum_end, cum)

    a_s = each(lambda c, l, x: stack(jnp.exp(c - l) * (-x)), cum, lw, kkn)
    r_s = each(lambda c, x: stack(jnp.exp(c) * x), cum, r)
    b_s = each(lambda p, x: stack(p * x), p_inv, kka)
    k_s = each(lambda p, x: stack(p * x), p_inv, k)
    v_s = each(stack, v)
    be_s = each(lambda p, x: stack(p * x), p_tail, kka)
    ke_s = each(lambda p, x: stack(p * x), p_tail, k)

    l_ab = each(lambda x, y: jnp.where(strict, _dot_nt(x, y), 0.0).astype(BF16), a_s, b_s)
    l_ak = each(lambda x, y: jnp.where(strict, _dot_nt(x, y), 0.0).astype(BF16), a_s, k_s)
    m_rb = each(lambda x, y: jnp.where(incl, _dot_nt(x, y), 0.0).astype(BF16), r_s, b_s)
    m_rk = each(lambda x, y: jnp.where(incl, _dot_nt(x, y), 0.0).astype(BF16), r_s, k_s)

    st = [st_scr[gi] for gi in groups]
    st_b = [s.astype(BF16) for s in st]
    x = each(lambda p, s, l, q: _dot(p, s) + _dot(l, q), a_s, st_b, l_ak, v_s)
    n_it = int(math.log2(c_len))
    li = l_ab
    for it in range(n_it):
        x = each(lambda xx, l: xx + _dot(l, xx.astype(BF16)), x, li)
        if it < n_it - 1:
            li = each(lambda l: _dot(l, l).astype(BF16), li)
    u_s = [xx.astype(BF16) for xx in x]

    def fold_heads(y_s):
        y = y_s[0:c_len]
        for e in range(1, hg):
            y = y + y_s[e * c_len:(e + 1) * c_len]
        return y

    y = each(lambda p, s, m1, u, m2, q: fold_heads(_dot(p, s) + _dot(m1, u) + _dot(m2, q)),
             r_s, st_b, m_rb, u_s, m_rk, v_s)

    def new_state(ce, s, b, u, kk_, q):
        pc_col = jnp.transpose(jnp.broadcast_to(jnp.exp(ce), (V7X_LANES, w)))[:, 0:1]
        return pc_col * s + _dot_tn(b, u) + _dot_tn(kk_, q)

    st_new = each(new_state, cum_end, st, be_s, u_s, ke_s, v_s)
    for gi in groups:
        st_scr[gi] = st_new[gi]

    inv_n = 1.0 / n
    mean = each(lambda yy: seg_sum(yy) * inv_n, y)
    d = each(lambda yy, m: yy - m, y, mean)
    var = each(lambda dd: seg_sum(dd * dd) * inv_n, d)
    bonus = each(lambda rr, kk_, ls, vv: seg_sum(rr * kk_ * rk_ref[:, ls]) * vv, r, k, lanes, v)
    for gi in groups:
        ls = lanes[gi]
        yn = d[gi] * lax.rsqrt(var[gi] + GN_EPS) * lng_ref[:, ls] + lnb_ref[:, ls]
        o_ref[:, ls] = (yn + bonus[gi]) * g_ref[:, ls]
    return st_new


def _wkv(r, k, v, kkr, a, lw, g, r_k, lnx_g, lnx_b, s0, *, batch, c_len, n_chunks, gps):
    rows, dim = r.shape
    hg = WKV_HG
    n = RWKV_HEAD_DIM
    w = hg * n
    wb = gps * w
    blk = pl.BlockSpec((c_len, wb), lambda b, gi, ci: (b * n_chunks + ci, gi))
    vec = pl.BlockSpec((1, wb), lambda b, gi, ci: (0, gi))
    st_spec = pl.BlockSpec((None, gps * hg, n, n), lambda b, gi, ci: (b, gi, 0, 0))
    in_specs = [blk] * 7 + [vec] * 3
    args = [r, k, v, kkr, a, lw, g, r_k.reshape(1, dim), lnx_g.reshape(1, dim), lnx_b.reshape(1, dim)]
    if s0 is not None:
        in_specs.append(st_spec)
        args.append(s0)
    rw = hg * c_len
    need = 2 * 8 * c_len * wb * 4 + gps * (5 * w * w * 4 + 16 * rw * w * 4 + 8 * rw * rw * 4)
    return pl.pallas_call(
        functools.partial(_wkv_kernel, c_len=c_len, hg=hg, gps=gps, n_chunks=n_chunks, has_s0=s0 is not None),
        grid=(batch, dim // wb, n_chunks),
        in_specs=in_specs,
        out_specs=[blk, st_spec],
        out_shape=[
            jax.ShapeDtypeStruct((rows, dim), F32),
            jax.ShapeDtypeStruct((batch, dim // n, n, n), F32),
        ],
        scratch_shapes=[pltpu.VMEM((gps, w, w), F32)],
        compiler_params=_params(("parallel", "parallel", "arbitrary"), need),
        name="wkv",
    )(*args)


def _rope_tables(pos):
    inv = ROPE_THETA ** (-jnp.arange(0, QK_ROPE_DIM, 2, dtype=F32) / QK_ROPE_DIM)
    ang = pos[:, None] * inv[None, :]
    cos, sin = jnp.cos(ang), jnp.sin(ang)
    return jnp.concatenate([cos] * 4, axis=1), jnp.concatenate([-sin, sin] * 2, axis=1)


def _swap_halves(w):
    half = w.shape[-1] // 2
    return jnp.concatenate([w[..., half:], w[..., :half]], axis=-1)


def _prepare_weights(w_in, w_uq, w_ukv, w_o, w_decay, w_iclr, w_gate):
    d = w_in.shape[0]
    rope = w_in[:, Q_LORA_RANK + KV_LORA_RANK:W_MLA_IN]
    rope_sw = _swap_halves(rope)
    zeros = lambda c: jnp.zeros((d, c), w_in.dtype)
    w_mla = jnp.concatenate(
        [w_in[:, :Q_LORA_RANK], zeros(MLA_KV_OFF - Q_LORA_RANK), w_in[:, Q_LORA_RANK:Q_LORA_RANK + KV_LORA_RANK],
         rope, rope, rope_sw, rope_sw, zeros(MLA_COLS - MLA_KV_OFF - KV_LORA_RANK - 4 * QK_ROPE_DIM)], axis=1).astype(BF16)
    w_rwkv = w_in[:, W_MLA_IN:].astype(BF16)
    uq = w_uq.reshape(Q_LORA_RANK, MLA_HEADS, QK_NOPE_DIM + QK_ROPE_DIM)
    uq_rope = uq[..., QK_NOPE_DIM:]
    w_q = jnp.concatenate(
        [uq[..., :QK_NOPE_DIM].reshape(Q_LORA_RANK, -1), uq_rope.reshape(Q_LORA_RANK, -1),
         _swap_halves(uq_rope).reshape(Q_LORA_RANK, -1)], axis=1).astype(BF16)
    ukv = w_ukv.reshape(KV_LORA_RANK, MLA_HEADS, QK_NOPE_DIM + V_HEAD_DIM)
    w_uk, w_uv = ukv[..., :QK_NOPE_DIM], ukv[..., QK_NOPE_DIM:]
    w_kvup = jnp.concatenate([w_uk.reshape(KV_LORA_RANK, -1), w_uv.reshape(KV_LORA_RANK, -1)], axis=1).astype(BF16)
    w_uk_t = jnp.transpose(w_uk, (1, 2, 0)).astype(BF16)
    w_uv_h = jnp.transpose(w_uv, (1, 0, 2)).astype(BF16)
    return dict(w_mla=w_mla, w_rwkv=w_rwkv, w_q=w_q, w_kvup=w_kvup, w_uk_t=w_uk_t, w_uv_h=w_uv_h,
                w_o=w_o.astype(BF16), w_decay=w_decay.astype(BF16), w_iclr=w_iclr.astype(BF16),
                w_gate=w_gate.astype(BF16))


def _group_layer(x, mods, tiles_per_group, tm, wts, p, *, cos4, sin4, ffn, attend, wkv_run, shift0, seq_len):
    sh2, sc2, g2 = mods
    x1 = ffn(0, x)
    mla = _mm(x1, wts["w_mla"], tm=tm, tn=512, mod=(sh2, sc2), tiles_per_group=tiles_per_group, name="proj_mla")
    u = _mm(x1, wts["w_rwkv"], tm=tm, tn=512, mod=(sh2, sc2), tiles_per_group=tiles_per_group, name="proj_rwkv")
    ckv, kr, kr2 = _lat_prep(mla, p["g_kv"], cos4, sin4, tm=tm)
    qall = _mm(mla, wts["w_q"], tm=tm, tn=512, k=Q_LORA_RANK, rms_g=p["g_q"], name="q_proj")
    q_rope = _rope_q(qall, cos4, sin4, tm=tm)
    attn = attend(qall, q_rope, ckv, kr, kr2)
    prep = _rwkv_prep(u, shift0, p["mu_shift"], p["w0"], p["a0"], p["k_k"], p["k_a"],
                      wts["w_decay"], wts["w_iclr"], wts["w_gate"], tm=PREP_TM, seq_len=seq_len)
    rwkv, s_new = wkv_run(prep)
    x2 = _oproj(x1, g2, attn, rwkv, wts["w_o"], p["ln_g"][1], p["ln_b"][1], tm=tm, tk=256,
                tiles_per_group=tiles_per_group)
    return ffn(1, x2), ckv, kr, s_new, u


def kernel(x_prompt, x_sample, c_prompt, c_sample, cache_kv_latent, cache_k_rope, state_wkv, state_shift, page_table, w_ada, b_ada, ln_g, ln_b, w_ffn1_in, w_ffn1_out, w_ffn2_in, w_ffn2_out, w_in, g_q, g_kv, w_uq, w_ukv, mu_shift, w0, w_decay, a0, w_iclr, w_gate, k_k, k_a, r_k, lnx_g, lnx_b, w_o):
    batch, seq, d = x_prompt.shape
    db, dseq, _ = x_sample.shape
    depth = w_ada.shape[0]
    assert depth == DEPTH == 1
    n_ada = w_ada.shape[2] // d
    past = page_table.shape[1] * PAGE_SIZE
    wu = state_shift.shape[2]

    c_all = jnp.concatenate([c_prompt, c_sample], axis=0)
    cos_p, sin_p = _rope_tables(jnp.arange(seq, dtype=F32) + 0)
    pos_s = jnp.tile(jnp.arange(dseq, dtype=F32) + past, db)
    cos_s, sin_s = _rope_tables(pos_s)

    yp = x_prompt.reshape(batch * seq, d)
    ys = x_sample.reshape(db * dseq, d)
    outs_p, outs_s = [], []
    for l in range(depth):
        p = dict(ln_g=ln_g[l], ln_b=ln_b[l], g_q=g_q[l], g_kv=g_kv[l], mu_shift=mu_shift[l], w0=w0[l], a0=a0[l],
                 k_k=k_k[l], k_a=k_a[l], r_k=r_k[l], lnx_g=lnx_g[l], lnx_b=lnx_b[l])
        wts = _prepare_weights(w_in[l], w_uq[l], w_ukv[l], w_o[l], w_decay[l], w_iclr[l], w_gate[l])
        ffn_w32 = ((w_ffn1_in[l], w_ffn1_out[l]), (w_ffn2_in[l], w_ffn2_out[l]))
        ffn_w16 = [None, None]
        ada = _ada(c_all, w_ada[l], b_ada[l]).reshape(batch + db, n_ada, d)
        mods_p = [ada[:batch, i][:, None, :] for i in range(n_ada)]
        mods_s = [ada[batch:, i][None] for i in range(n_ada)]
        per_token = lambda m: jnp.repeat(m[0], dseq, axis=0)[None]

        tm_s = db * dseq
        c_pad = 16
        t_major = lambda a: a.reshape(db, dseq, d).transpose(1, 0, 2).reshape(tm_s, d)
        b_major = lambda a: a.reshape(dseq, db, d).transpose(1, 0, 2).reshape(tm_s, d)

        def ffn_s(i, x):
            sh, sc, g = mods_s[6 * i:6 * i + 3]
            y, w_in16, w_out16 = _ffn(t_major(x), sh, sc, g, *ffn_w32[i], p["ln_g"][2 * i], p["ln_b"][2 * i],
                                      tm=tm_s, tf=128, tiles_per_group=1, emit_bf16_weights=True)
            ffn_w16[i] = (w_in16, w_out16)
            return b_major(y)

        def attend_s(qall, q_rope, ckv, kr, kr2):
            rows = dseq * MLA_HEADS
            q_lat = _bmm(qall, wts["w_uk_t"], tm=tm_s, out_dtype=BF16, name="q_absorb")
            pad = lambda t: jnp.pad(t.reshape(db, dseq, -1), ((0, 0), (0, c_pad - dseq), (0, 0)))
            o_lat = _paged(page_table, q_lat.reshape(db, rows, KV_LORA_RANK), q_rope.reshape(db, rows, QK_ROPE_DIM),
                           pad(ckv), pad(kr), cache_kv_latent, jnp.swapaxes(cache_k_rope, 2, 3),
                           npg=PAGES_PER_STEP, new_len=dseq)
            return _bmm(o_lat.reshape(db * dseq, MLA_HEADS * KV_LORA_RANK), wts["w_uv_h"], tm=tm_s, name="v_up")

        def wkv_s(prep):
            padded = [jnp.pad(t.reshape(db, dseq, -1), ((0, 0), (0, c_pad - dseq), (0, 0))).reshape(db * c_pad, -1)
                      for t in prep]
            out, s_new = _wkv(*padded, p["r_k"], p["lnx_g"], p["lnx_b"], state_wkv[l], batch=db, c_len=c_pad,
                              n_chunks=1, gps=RWKV_HEADS // WKV_HG)
            return out.reshape(db, c_pad, -1)[:, :dseq].reshape(db * dseq, -1), s_new

        shift_rows = jnp.repeat(state_shift[l], dseq, axis=0).reshape(db * dseq // PREP_TM, PREP_TM, wu)
        ys, ckv_s, kr_s, s_s, u_s = _group_layer(
            ys, [per_token(m) for m in mods_s[3:6]], 1, tm_s, wts, p, cos4=cos_s, sin4=sin_s, ffn=ffn_s,
            attend=attend_s, wkv_run=wkv_s, shift0=shift_rows, seq_len=dseq)
        outs_s.append((ckv_s.reshape(db, dseq, -1), kr_s.reshape(db, dseq, -1), s_s,
                       u_s.reshape(db, dseq, wu)[:, -1]))

        tm_p = 512

        def ffn_p(i, x):
            sh, sc, g = mods_p[6 * i:6 * i + 3]
            return _ffn(x, sh, sc, g, *ffn_w16[i], p["ln_g"][2 * i], p["ln_b"][2 * i],
                        tm=tm_p, tf=256, tiles_per_group=seq // tm_p)

        def attend_p(qall, q_rope, ckv, kr, kr2):
            kvup = _mm(ckv, wts["w_kvup"], tm=tm_p, tn=512, name="kv_up")
            return _flash(qall, q_rope, kvup, kr2, batch=batch, seq=seq, tq=512)

        def wkv_p(prep):
            return _wkv(*prep, p["r_k"], p["lnx_g"], p["lnx_b"], None, batch=batch, c_len=64, n_chunks=seq // 64,
                        gps=4)

        yp, ckv_p, kr_p, s_p, u_p = _group_layer(
            yp, mods_p[3:6], seq // tm_p, tm_p, wts, p, cos4=cos_p, sin4=sin_p, ffn=ffn_p, attend=attend_p,
            wkv_run=wkv_p, shift0=jnp.zeros((batch, 1, wu), F32), seq_len=seq)
        outs_p.append((ckv_p.reshape(batch, seq, -1), kr_p.reshape(batch, seq, -1), s_p,
                       u_p.reshape(batch, seq, wu)[:, -1]))

    stack = lambda outs, i: jnp.stack([o[i] for o in outs])
    return (yp.reshape(batch, seq, d), ys.reshape(db, dseq, d),
            stack(outs_p, 0), stack(outs_p, 1), stack(outs_p, 2), stack(outs_p, 3),
            stack(outs_s, 0), stack(outs_s, 1), stack(outs_s, 2), stack(outs_s, 3))
```

```python
import functools
import math

import jax
import jax.numpy as jnp
from jax import lax
from jax.experimental import pallas as pl
from jax.experimental.pallas import tpu as pltpu

F32 = jnp.float32
BF16 = jnp.bfloat16

QK_NOPE_DIM = 128
QK_ROPE_DIM = 64
V_HEAD_DIM = 128
MLA_HEADS = 16
Q_LORA_RANK = 896
KV_LORA_RANK = 512
RWKV_HEAD_DIM = 64
RWKV_HEADS = 32
RWKV_DIM = RWKV_HEADS * RWKV_HEAD_DIM
DECAY_LORA = 128
ICLR_LORA = 128
GATE_LORA = 256
W_MLA_IN = Q_LORA_RANK + KV_LORA_RANK + QK_ROPE_DIM
PAGE_SIZE = 128
ROPE_THETA = 10000.0
SM_SCALE = (QK_NOPE_DIM + QK_ROPE_DIM) ** -0.5
DEPTH = 1
ALPHA = (2 * DEPTH) ** 0.25
LN_EPS = 1e-5
RMS_EPS = 1e-6
GN_EPS = 64e-5

V7X_VMEM_BYTES = 64 * 1024 * 1024
V7X_LANES = 128
SUBLANES = 8
MIB = 1024 * 1024

MLA_COLS = 2048
MLA_KV_OFF = 1024

ROW_CHUNK = 64
N_CHUNK = 512
MM_TN = 512
FFN_TF = 256
PREP_TM = 128
PAGES_PER_STEP = 32
WKV_HG = 4


def _params(sem, vmem_bytes):
    limit = min(int(vmem_bytes) + 8 * MIB, V7X_VMEM_BYTES - 6 * MIB)
    return pltpu.CompilerParams(dimension_semantics=sem, vmem_limit_bytes=limit)


def _dot(a, b):
    return jnp.dot(a, b, preferred_element_type=F32)


def _dot_nt(a, b):
    return lax.dot_general(a, b, (((1,), (1,)), ((), ())), preferred_element_type=F32)


def _dot_tn(a, b):
    return lax.dot_general(a, b, (((0,), (0,)), ((), ())), preferred_element_type=F32)


def _split_bf16(x, parts):
    out = []
    rem = x
    for _ in range(parts):
        h = rem.astype(BF16)
        out.append(h)
        rem = rem - h.astype(F32)
    return out


def _layernorm_rows(z, g, b):
    mu = jnp.mean(z, axis=-1, keepdims=True)
    d = z - mu
    var = jnp.mean(d * d, axis=-1, keepdims=True)
    return d * lax.rsqrt(var + LN_EPS) * g + b


def _for_row_chunks(n_rows, fn, rows=ROW_CHUNK):
    def body(i, carry):
        fn(pl.ds(pl.multiple_of(i * rows, rows), rows))
        return carry

    lax.fori_loop(0, n_rows // rows, body, 0)


def _rows_of(ref, rs):
    mr = ref.shape[0]
    if mr == 1:
        return ref[...]
    return ref[pl.ds(pl.multiple_of(rs.start % mr, rs.size), rs.size), :]


def _modulate_into(h_scr, x_ref, sh_ref, sc_ref):
    def chunk(rs):
        h_scr[rs, :] = (x_ref[rs, :] * (1.0 + _rows_of(sc_ref, rs)) + _rows_of(sh_ref, rs)).astype(BF16)

    _for_row_chunks(h_scr.shape[0], chunk)


def _residual_layernorm(o_ref, x_ref, gate_ref, lng_ref, lnb_ref, gate_scale):
    def chunk(rs):
        z = ALPHA * x_ref[rs, :] + gate_scale * _rows_of(gate_ref, rs) * o_ref[rs, :]
        o_ref[rs, :] = _layernorm_rows(z, lng_ref[...], lnb_ref[...])

    _for_row_chunks(o_ref.shape[0], chunk)


def _ada_kernel(c_ref, w_ref, b_ref, o_ref):
    c = c_ref[...]
    a = (c * jax.nn.sigmoid(c)).astype(BF16)
    o_ref[...] = _dot(a, w_ref[...].astype(BF16)) + b_ref[...]


def _ada(c_all, w_ada, b_ada, tn=512):
    m, k = c_all.shape
    n = w_ada.shape[1]
    need = 2 * (k * tn * 4) + k * tn * 2 + 2 * m * k * 4 + 4 * m * tn * 4
    return pl.pallas_call(
        _ada_kernel,
        grid=(n // tn,),
        in_specs=[
            pl.BlockSpec((m, k), lambda j: (0, 0)),
            pl.BlockSpec((k, tn), lambda j: (0, j)),
            pl.BlockSpec((1, tn), lambda j: (0, j)),
        ],
        out_specs=pl.BlockSpec((m, tn), lambda j: (0, j)),
        out_shape=jax.ShapeDtypeStruct((m, n), F32),
        compiler_params=_params(("parallel",), need),
        name="ada",
    )(c_all, w_ada, b_ada.reshape(1, n))


def _ffn_kernel(x_ref, sh_ref, sc_ref, g_ref, wg_ref, wu_ref, wo_ref, lng_ref, lnb_ref, o_ref, *rest, nj):
    h_scr = rest[-1]
    j = pl.program_id(1)

    @pl.when(j == 0)
    def _():
        _modulate_into(h_scr, x_ref, sh_ref, sc_ref)
        o_ref[...] = jnp.zeros_like(o_ref)

    wg, wu, wo = wg_ref[...], wu_ref[...], wo_ref[...]
    if len(rest) > 1:
        wg, wu, wo = wg.astype(BF16), wu.astype(BF16), wo.astype(BF16)
        rest[0][...] = wg
        rest[1][...] = wu
        rest[2][...] = wo
    h = h_scr[...]
    gate = _dot(h, wg)
    up = _dot(h, wu)
    act = (gate * jax.nn.sigmoid(gate) * up).astype(BF16)
    for n0 in range(0, o_ref.shape[1], N_CHUNK):
        o_ref[:, n0:n0 + N_CHUNK] += _dot(act, wo[:, n0:n0 + N_CHUNK])

    @pl.when(j == nj - 1)
    def _():
        _residual_layernorm(o_ref, x_ref, g_ref, lng_ref, lnb_ref, 0.5)


def _ffn(x, sh, sc, g, w_in, w_out, ln_g, ln_b, *, tm, tf, tiles_per_group, emit_bf16_weights=False):
    r, d = x.shape
    ff = w_out.shape[0]
    nj = ff // tf
    mr = sh.shape[1]
    if isinstance(w_in, tuple):
        w_gate, w_up = w_in
        assert w_gate.shape[2] == tf
        gate_spec = up_spec = pl.BlockSpec((None, d, tf), lambda i, j: (j, 0, 0))
    else:
        w_gate = w_up = w_in
        gate_spec = pl.BlockSpec((d, tf), lambda i, j: (0, j))
        up_spec = pl.BlockSpec((d, tf), lambda i, j: (0, nj + j))
    wbytes = w_out.dtype.itemsize
    mod_spec = pl.BlockSpec((None, mr, d), lambda i, j: (i // tiles_per_group, 0, 0))
    out_specs = [pl.BlockSpec((tm, d), lambda i, j: (i, 0))]
    out_shape = [jax.ShapeDtypeStruct((r, d), F32)]
    need = 4 * tm * d * 4 + tm * d * 2 + 2 * 3 * d * tf * wbytes + 6 * mr * d * 4 + 6 * tm * tf * 4
    if emit_bf16_weights:
        assert r == tm and w_out.dtype == F32 and FFN_TF % tf == 0
        per_tile = FFN_TF // tf
        emit_spec = pl.BlockSpec((None, d, tf), lambda i, j: (j // per_tile, 0, j % per_tile))
        out_specs += [emit_spec, emit_spec, pl.BlockSpec((tf, d), lambda i, j: (j, 0))]
        out_shape += [jax.ShapeDtypeStruct((ff // FFN_TF, d, FFN_TF), BF16)] * 2 + [jax.ShapeDtypeStruct((ff, d), BF16)]
        need += 3 * 3 * d * tf * 2
    outs = pl.pallas_call(
        functools.partial(_ffn_kernel, nj=nj),
        grid=(r // tm, nj),
        in_specs=[
            pl.BlockSpec((tm, d), lambda i, j: (i, 0)),
            mod_spec, mod_spec, mod_spec,
            gate_spec, up_spec,
            pl.BlockSpec((tf, d), lambda i, j: (j, 0)),
            pl.BlockSpec((1, d), lambda i, j: (0, 0)),
            pl.BlockSpec((1, d), lambda i, j: (0, 0)),
        ],
        out_specs=out_specs,
        out_shape=out_shape,
        scratch_shapes=[pltpu.VMEM((tm, d), BF16)],
        compiler_params=_params(("parallel", "arbitrary"), need),
        name="ffn",
    )(x, sh, sc, g, w_gate, w_up, w_out, ln_g.reshape(1, d), ln_b.reshape(1, d))
    return (outs[0], (outs[1], outs[2]), outs[3]) if emit_bf16_weights else outs[0]


def _mm_mod_kernel(a_ref, sh_ref, sc_ref, w_ref, o_ref, h_scr):
    @pl.when(pl.program_id(1) == 0)
    def _():
        _modulate_into(h_scr, a_ref, sh_ref, sc_ref)

    o_ref[...] = _dot(h_scr[...], w_ref[...]).astype(o_ref.dtype)


def _mm_rms_kernel(a_ref, g_ref, w_ref, o_ref, h_scr):
    @pl.when(pl.program_id(1) == 0)
    def _():
        a = a_ref[...]
        ms = jnp.mean(a * a, axis=-1, keepdims=True)
        h_scr[...] = (a * lax.rsqrt(ms + RMS_EPS) * g_ref[...]).astype(BF16)

    o_ref[...] = _dot(h_scr[...], w_ref[...]).astype(o_ref.dtype)


def _mm_plain_kernel(a_ref, w_ref, o_ref, h_scr):
    @pl.when(pl.program_id(1) == 0)
    def _():
        h_scr[...] = a_ref[...].astype(BF16)

    o_ref[...] = _dot(h_scr[...], w_ref[...]).astype(o_ref.dtype)


def _tile_major(w, tn):
    k, n = w.shape
    return w.reshape(k, n // tn, tn).transpose(1, 0, 2)


def _mm(a, w, *, tm, k=None, a_colblk=0, mod=None, rms_g=None, tiles_per_group=1, out_dtype=F32, name="mm"):
    r = a.shape[0]
    k = a.shape[1] if k is None else k
    tn = w.shape[2]
    n = w.shape[0] * tn
    a_spec = pl.BlockSpec((tm, k), lambda i, j: (i, a_colblk))
    w_spec = pl.BlockSpec((None, k, tn), lambda i, j: (j, 0, 0))
    need = 2 * tm * k * a.dtype.itemsize + tm * k * 2 + 2 * k * tn * 2 + 4 * tm * tn * 4
    if mod is not None:
        sh, sc = mod
        mr = sh.shape[1]
        mod_spec = pl.BlockSpec((None, mr, k), lambda i, j: (i // tiles_per_group, 0, 0))
        kern, ins, args = _mm_mod_kernel, [a_spec, mod_spec, mod_spec, w_spec], (a, sh, sc, w)
        need += 4 * mr * k * 4
    elif rms_g is not None:
        kern, ins, args = _mm_rms_kernel, [a_spec, pl.BlockSpec((1, k), lambda i, j: (0, 0)), w_spec], (a, rms_g.reshape(1, k), w)
    else:
        kern, ins, args = _mm_plain_kernel, [a_spec, w_spec], (a, w)
    return pl.pallas_call(
        kern,
        grid=(r // tm, n // tn),
        in_specs=ins,
        out_specs=pl.BlockSpec((tm, tn), lambda i, j: (i, j)),
        out_shape=jax.ShapeDtypeStruct((r, n), out_dtype),
        scratch_shapes=[pltpu.VMEM((tm, k), BF16)],
        compiler_params=_params(("parallel", "arbitrary"), need),
        name=name,
    )(*args)


def _bmm_kernel(a_ref, w_ref, o_ref):
    o_ref[...] = _dot(a_ref[...].astype(BF16), w_ref[...]).astype(o_ref.dtype)


def _bmm(a, w, *, tm, out_dtype=F32, name="bmm"):
    r = a.shape[0]
    h, ka, nb = w.shape
    need = 2 * tm * ka * 4 + 2 * ka * nb * 2 + 2 * tm * nb * 4
    return pl.pallas_call(
        _bmm_kernel,
        grid=(r // tm, h),
        in_specs=[
            pl.BlockSpec((tm, ka), lambda i, hh: (i, hh)),
            pl.BlockSpec((None, ka, nb), lambda i, hh: (hh, 0, 0)),
        ],
        out_specs=pl.BlockSpec((tm, nb), lambda i, hh: (i, hh)),
        out_shape=jax.ShapeDtypeStruct((r, h * nb), out_dtype),
        compiler_params=_params(("parallel", "parallel"), need),
        name=name,
    )(a, w)


def _oproj_kernel(x_ref, g_ref, a1_ref, a2_ref, w1_ref, w2_ref, lng_ref, lnb_ref, o_ref, *, nk):
    kk = pl.program_id(1)

    @pl.when(kk == 0)
    def _():
        o_ref[...] = jnp.zeros_like(o_ref)

    a1 = a1_ref[...].astype(BF16)
    a2 = a2_ref[...].astype(BF16)
    for n0 in range(0, o_ref.shape[1], N_CHUNK):
        cols = slice(n0, n0 + N_CHUNK)
        o_ref[:, cols] += _dot(a1, w1_ref[:, cols]) + _dot(a2, w2_ref[:, cols])

    @pl.when(kk == nk - 1)
    def _():
        _residual_layernorm(o_ref, x_ref, g_ref, lng_ref, lnb_ref, 1.0)


def _oproj(x, g, attn, rwkv, w_o, ln_g, ln_b, *, tm, tk, tiles_per_group):
    r, d = x.shape
    half = attn.shape[1]
    nk = half // tk
    mr = g.shape[1]
    need = 4 * tm * d * 4 + 2 * mr * d * 4 + 4 * tm * tk * 4 + 4 * tk * d * 2
    return pl.pallas_call(
        functools.partial(_oproj_kernel, nk=nk),
        grid=(r // tm, nk),
        in_specs=[
            pl.BlockSpec((tm, d), lambda i, kk: (i, 0)),
            pl.BlockSpec((None, mr, d), lambda i, kk: (i // tiles_per_group, 0, 0)),
            pl.BlockSpec((tm, tk), lambda i, kk: (i, kk)),
            pl.BlockSpec((tm, tk), lambda i, kk: (i, kk)),
            pl.BlockSpec((tk, d), lambda i, kk: (kk, 0)),
            pl.BlockSpec((tk, d), lambda i, kk: (nk + kk, 0)),
            pl.BlockSpec((1, d), lambda i, kk: (0, 0)),
            pl.BlockSpec((1, d), lambda i, kk: (0, 0)),
        ],
        out_specs=pl.BlockSpec((tm, d), lambda i, kk: (i, 0)),
        out_shape=jax.ShapeDtypeStruct((r, d), F32),
        compiler_params=_params(("parallel", "arbitrary"), need),
        name="oproj",
    )(x, g, attn, rwkv, w_o, w_o, ln_g.reshape(1, d), ln_b.reshape(1, d))


def _lat_kernel(m_ref, g_ref, cos_ref, sin_ref, ckv_ref, kr_ref, kr2_ref):
    m = m_ref[...]
    kv = m[:, :KV_LORA_RANK]
    ms = jnp.mean(kv * kv, axis=-1, keepdims=True)
    ckv_ref[...] = kv * lax.rsqrt(ms + RMS_EPS) * g_ref[...]
    a2 = m[:, KV_LORA_RANK:KV_LORA_RANK + V7X_LANES]
    b2 = m[:, KV_LORA_RANK + V7X_LANES:KV_LORA_RANK + 2 * V7X_LANES]
    kr2 = a2 * cos_ref[...] + b2 * sin_ref[...]
    kr2_ref[...] = kr2
    kr_ref[...] = kr2[:, :QK_ROPE_DIM]


def _lat_prep(mla, g_kv, cos4, sin4, *, tm):
    r = mla.shape[0]
    nb = cos4.shape[0] // tm
    wblk = MLA_COLS - MLA_KV_OFF
    need = 2 * tm * wblk * 4 + 2 * tm * (KV_LORA_RANK + 3 * V7X_LANES + 2 * V7X_LANES) * 4
    return pl.pallas_call(
        _lat_kernel,
        grid=(r // tm,),
        in_specs=[
            pl.BlockSpec((tm, wblk), lambda i: (i, MLA_KV_OFF // wblk)),
            pl.BlockSpec((1, KV_LORA_RANK), lambda i: (0, 0)),
            pl.BlockSpec((tm, V7X_LANES), lambda i: (i % nb, 0)),
            pl.BlockSpec((tm, V7X_LANES), lambda i: (i % nb, 0)),
        ],
        out_specs=[
            pl.BlockSpec((tm, KV_LORA_RANK), lambda i: (i, 0)),
            pl.BlockSpec((tm, QK_ROPE_DIM), lambda i: (i, 0)),
            pl.BlockSpec((tm, V7X_LANES), lambda i: (i, 0)),
        ],
        out_shape=[
            jax.ShapeDtypeStruct((r, KV_LORA_RANK), F32),
            jax.ShapeDtypeStruct((r, QK_ROPE_DIM), F32),
            jax.ShapeDtypeStruct((r, V7X_LANES), F32),
        ],
        compiler_params=_params(("parallel",), need),
        name="lat_prep",
    )(mla, g_kv.reshape(1, KV_LORA_RANK), cos4, sin4)


def _ropeq_kernel(a_ref, b_ref, cos_ref, sin_ref, o_ref):
    reps = a_ref.shape[1] // V7X_LANES
    c = jnp.concatenate([cos_ref[...]] * reps, axis=1)
    s = jnp.concatenate([sin_ref[...]] * reps, axis=1)
    o_ref[...] = (a_ref[...] * c + b_ref[...] * s).astype(o_ref.dtype)


def _rope_q(qall, cos4, sin4, *, tm):
    r = qall.shape[0]
    w = MLA_HEADS * QK_ROPE_DIM
    nb = cos4.shape[0] // tm
    need = 4 * tm * w * 4 + 2 * tm * w * 2 + 4 * tm * w * 4
    return pl.pallas_call(
        _ropeq_kernel,
        grid=(r // tm,),
        in_specs=[
            pl.BlockSpec((tm, w), lambda i: (i, 2)),
            pl.BlockSpec((tm, w), lambda i: (i, 3)),
            pl.BlockSpec((tm, V7X_LANES), lambda i: (i % nb, 0)),
            pl.BlockSpec((tm, V7X_LANES), lambda i: (i % nb, 0)),
        ],
        out_specs=pl.BlockSpec((tm, w), lambda i: (i, 0)),
        out_shape=jax.ShapeDtypeStruct((r, w), F32),
        compiler_params=_params(("parallel",), need),
        name="rope_q",
    )(qall, qall, cos4, sin4)


def _flash_kernel(qn_ref, qr_ref, kn_ref, kr_ref, v_ref, o_ref, m_scr, l_scr, acc_scr, *, tq, tk):
    qi = pl.program_id(2)
    ki = pl.program_id(3)

    @pl.when(ki == 0)
    def _():
        m_scr[...] = jnp.full_like(m_scr, -jnp.inf)
        l_scr[...] = jnp.zeros_like(l_scr)
        acc_scr[...] = jnp.zeros_like(acc_scr)

    def step(on_diagonal):
        qn = (qn_ref[...] * SM_SCALE).astype(BF16)
        qr = qr_ref[...] * SM_SCALE
        kn = kn_ref[...].astype(BF16)
        kr2 = kr_ref[...].astype(BF16)
        v = v_ref[...].astype(BF16)
        lane = lax.broadcasted_iota(jnp.int32, (1, V7X_LANES), 1)
        if on_diagonal:
            causal = lax.broadcasted_iota(jnp.int32, (tq, 1), 0) >= lax.broadcasted_iota(jnp.int32, (1, tk), 1)
        heads = range(2)
        hs = [slice(e * QK_NOPE_DIM, (e + 1) * QK_NOPE_DIM) for e in heads]
        qre = [jnp.where((lane >= QK_ROPE_DIM) if e else (lane < QK_ROPE_DIM), qr, 0.0).astype(BF16) for e in heads]
        s = [_dot_nt(qn[:, hs[e]], kn[:, hs[e]]) + _dot_nt(qre[e], kr2) for e in heads]
        if on_diagonal:
            s = [jnp.where(causal, s[e], -jnp.inf) for e in heads]
        m_prev = [m_scr[e] for e in heads]
        m_new = [jnp.maximum(m_prev[e], jnp.max(s[e], axis=-1, keepdims=True)) for e in heads]
        corr = [jnp.exp(m_prev[e] - m_new[e]) for e in heads]
        p = [jnp.exp(s[e] - m_new[e]) for e in heads]
        l_new = [l_scr[e] * corr[e] + jnp.sum(p[e], axis=-1, keepdims=True) for e in heads]
        acc_new = [acc_scr[e] * corr[e] + _dot(p[e].astype(BF16), v[:, hs[e]]) for e in heads]
        for e in heads:
            if on_diagonal:
                o_ref[:, e * V_HEAD_DIM:(e + 1) * V_HEAD_DIM] = acc_new[e] / l_new[e]
            else:
                l_scr[e] = l_new[e]
                acc_scr[e] = acc_new[e]
                m_scr[e] = m_new[e]

    @pl.when(ki < qi)
    def _():
        step(False)

    @pl.when(ki == qi)
    def _():
        step(True)


def _flash(qall, q_rope, kvup, kr2, *, batch, seq, tq):
    tk = tq
    nq = seq // tq
    r = qall.shape[0]
    hp = MLA_HEADS // 2
    need = 2 * (tq * 256 * 4 + tq * 128 * 2 + 2 * tk * 256 * 4 + tk * 128 * 4 + tq * 256 * 4) + 4 * tq * 128 * 4 + 8 * tq * tk * 4
    return pl.pallas_call(
        functools.partial(_flash_kernel, tq=tq, tk=tk),
        grid=(batch, hp, nq, nq),
        in_specs=[
            pl.BlockSpec((tq, 2 * QK_NOPE_DIM), lambda b, h, qi, ki: (b * nq + qi, h)),
            pl.BlockSpec((tq, V7X_LANES), lambda b, h, qi, ki: (b * nq + qi, h)),
            pl.BlockSpec((tk, 2 * QK_NOPE_DIM), lambda b, h, qi, ki: (b * nq + jnp.minimum(ki, qi), h)),
            pl.BlockSpec((tk, V7X_LANES), lambda b, h, qi, ki: (b * nq + jnp.minimum(ki, qi), 0)),
            pl.BlockSpec((tk, 2 * V_HEAD_DIM), lambda b, h, qi, ki: (b * nq + jnp.minimum(ki, qi), hp + h)),
        ],
        out_specs=pl.BlockSpec((tq, 2 * V_HEAD_DIM), lambda b, h, qi, ki: (b * nq + qi, h)),
        out_shape=jax.ShapeDtypeStruct((r, MLA_HEADS * V_HEAD_DIM), F32),
        scratch_shapes=[
            pltpu.VMEM((2, tq, 1), F32),
            pltpu.VMEM((2, tq, 1), F32),
            pltpu.VMEM((2, tq, V_HEAD_DIM), F32),
        ],
        compiler_params=_params(("parallel", "parallel", "parallel", "arbitrary"), need),
        name="flash",
    )(qall, q_rope, kvup, kr2, kvup)


def _paged_kernel(pt_ref, ql_ref, qr_ref, cn_ref, krn_ref, lat_hbm, krt_hbm, o_ref,
                  lat_buf, krt_buf, sem, m_scr, l_scr, acc_scr, *, npg, nchunk, new_len, heads):
    c = pl.program_id(1)
    step = pl.program_id(0) * nchunk + c
    n_steps = pl.num_programs(0) * nchunk
    slot = step % 2

    def page_copies(step_, slot_):
        cps = []
        for i in range(npg):
            page = pt_ref[step_ * npg + i]
            cps.append(pltpu.make_async_copy(lat_hbm.at[0, page], lat_buf.at[slot_, i], sem.at[0, slot_]))
            cps.append(pltpu.make_async_copy(krt_hbm.at[0, page], krt_buf.at[slot_, i], sem.at[1, slot_]))
        return cps

    @pl.when(step == 0)
    def _():
        for cp in page_copies(step, slot):
            cp.start()

    @pl.when(step + 1 < n_steps)
    def _():
        for cp in page_copies(step + 1, 1 - slot):
            cp.start()

    @pl.when(c == 0)
    def _():
        m_scr[...] = jnp.full_like(m_scr, -jnp.inf)
        l_scr[...] = jnp.zeros_like(l_scr)
        acc_scr[...] = jnp.zeros_like(acc_scr)

    ql = ql_ref[...]
    qr = qr_ref[...].astype(BF16)

    def update(s, vals):
        m_prev = m_scr[...]
        m_new = jnp.maximum(m_prev, jnp.max(s, axis=-1, keepdims=True))
        corr = jnp.exp(m_prev - m_new)
        p = jnp.exp(s - m_new)
        l_scr[...] = l_scr[...] * corr + jnp.sum(p, axis=-1, keepdims=True)
        w = s.shape[1] // len(vals)
        pv = _dot(p[:, :w].astype(BF16), vals[0])
        for i in range(1, len(vals)):
            pv += _dot(p[:, i * w:(i + 1) * w].astype(BF16), vals[i])
        acc_scr[...] = acc_scr[...] * corr + pv
        m_scr[...] = m_new

    for cp in page_copies(step, slot):
        cp.wait()
    kls = [lat_buf[slot, i].astype(BF16) for i in range(npg)]
    s = jnp.concatenate(
        [_dot_nt(ql, kls[i]) + _dot(qr, krt_buf[slot, i].astype(BF16)) for i in range(npg)], axis=1)
    update(s * SM_SCALE, kls)

    @pl.when(c == nchunk - 1)
    def _():
        kn = cn_ref[...].astype(BF16)
        s2 = (_dot_nt(ql, kn) + _dot_nt(qr, krn_ref[...].astype(BF16))) * SM_SCALE
        rows = s2.shape[0]
        t_row = lax.broadcasted_iota(jnp.int32, (rows, 1), 0) // heads
        t_col = lax.broadcasted_iota(jnp.int32, (1, s2.shape[1]), 1)
        ok = (t_col <= t_row) & (t_col < new_len)
        update(jnp.where(ok, s2, -jnp.inf), [kn])
        o_ref[...] = acc_scr[...] / l_scr[...]


def _paged(page_table, q_lat, q_rope, ckv_new, kr_new, cache_lat, cache_kr_t, *, npg, new_len):
    db, rows, lat = q_lat.shape
    n_pages = page_table.shape[1]
    nchunk = n_pages // npg
    npad = ckv_new.shape[1]
    assert n_pages == nchunk * npg
    pt_flat = page_table.reshape(-1)
    in_specs = [
        pl.BlockSpec((None, rows, lat), lambda b, c, pt: (b, 0, 0)),
        pl.BlockSpec((None, rows, QK_ROPE_DIM), lambda b, c, pt: (b, 0, 0)),
        pl.BlockSpec((None, npad, lat), lambda b, c, pt: (b, 0, 0)),
        pl.BlockSpec((None, npad, QK_ROPE_DIM), lambda b, c, pt: (b, 0, 0)),
        pl.BlockSpec(memory_space=pl.ANY),
        pl.BlockSpec(memory_space=pl.ANY),
    ]
    need = (2 * npg * PAGE_SIZE * (lat + QK_ROPE_DIM) * 4 + npg * PAGE_SIZE * lat * 2
            + 6 * rows * npg * PAGE_SIZE * 4 + 8 * rows * lat * 4)
    grid_spec = pltpu.PrefetchScalarGridSpec(
        num_scalar_prefetch=1,
        grid=(db, nchunk),
        in_specs=in_specs,
        out_specs=pl.BlockSpec((None, rows, lat), lambda b, c, pt: (b, 0, 0)),
        scratch_shapes=[
            pltpu.VMEM((2, npg, PAGE_SIZE, lat), F32),
            pltpu.VMEM((2, npg, QK_ROPE_DIM, PAGE_SIZE), F32),
            pltpu.SemaphoreType.DMA((2, 2)),
            pltpu.VMEM((rows, 1), F32),
            pltpu.VMEM((rows, 1), F32),
            pltpu.VMEM((rows, lat), F32),
        ],
    )
    return pl.pallas_call(
        functools.partial(_paged_kernel, npg=npg, nchunk=nchunk, new_len=new_len, heads=MLA_HEADS),
        grid_spec=grid_spec,
        out_shape=jax.ShapeDtypeStruct((db, rows, lat), F32),
        compiler_params=_params(("arbitrary", "arbitrary"), need),
        name="paged",
    )(pt_flat, q_lat, q_rope, ckv_new, kr_new, cache_lat, cache_kr_t)


def _prep_kernel(u_ref, tail_ref, s0_ref, mu_ref, w0_ref, a0_ref, kk_ref, ka_ref, wd_ref, wi_ref, wg_ref,
                 r_ref, k_ref, v_ref, kkr_ref, a_ref, lw_ref, g_ref, *, seq_len):
    tm = u_ref.shape[0]
    u = u_ref[...]
    row = lax.broadcasted_iota(jnp.int32, (tm, 1), 0)
    prev = jnp.where(row == 0, tail_ref[SUBLANES - 1:SUBLANES, :], pltpu.roll(u, 1, 0))
    starts = ((pl.program_id(0) * tm + row) & (seq_len - 1)) == 0
    prev = jnp.where(starts, s0_ref[...], prev)
    um = u + (prev - u) * mu_ref[...]
    n = RWKV_DIM
    k = um[:, n:2 * n]
    dw = um[:, 3 * n:3 * n + DECAY_LORA]
    da = um[:, 3 * n + DECAY_LORA:3 * n + DECAY_LORA + ICLR_LORA]
    dg = um[:, 3 * n + DECAY_LORA + ICLR_LORA:]
    z = -(w0_ref[...] + _dot(jnp.tanh(dw).astype(BF16), wd_ref[...]))
    softplus = jnp.maximum(z, 0.0) + jnp.log1p(jnp.exp(-jnp.abs(z)))
    lw_ref[...] = -jnp.exp(-softplus - 0.5)
    a = jax.nn.sigmoid(a0_ref[...] + _dot(da.astype(BF16), wi_ref[...]))
    a_ref[...] = a
    g_ref[...] = _dot(jax.nn.sigmoid(dg).astype(BF16), wg_ref[...])
    r_ref[...] = um[:, :n]
    v_ref[...] = um[:, 2 * n:3 * n]
    kkr_ref[...] = k * kk_ref[...]
    k_ref[...] = k * (1.0 + (a - 1.0) * ka_ref[...])


def _rwkv_prep(u, shift0, mu, w0, a0, k_k, k_a, w_decay, w_iclr, w_gate, *, tm, seq_len):
    r, wu = u.shape
    n = RWKV_DIM
    assert seq_len & (seq_len - 1) == 0
    tiles_per_group = max(seq_len // tm, 1)
    mr = shift0.shape[1]
    row = lambda width: pl.BlockSpec((1, width), lambda i: (0, 0))
    full = lambda arr: pl.BlockSpec(arr.shape, lambda i: (0, 0))
    out_spec = pl.BlockSpec((tm, n), lambda i: (i, 0))
    need = 2 * tm * wu * 4 + 2 * mr * wu * 4 + 2 * 7 * tm * n * 4 + 6 * tm * wu * 4
    return pl.pallas_call(
        functools.partial(_prep_kernel, seq_len=seq_len),
        grid=(r // tm,),
        in_specs=[
            pl.BlockSpec((tm, wu), lambda i: (i, 0)),
            pl.BlockSpec((SUBLANES, wu), lambda i: (jnp.maximum(i * (tm // SUBLANES) - 1, 0), 0)),
            pl.BlockSpec((None, mr, wu), lambda i: (i // tiles_per_group, 0, 0)),
            row(wu), row(n), row(n), row(n), row(n),
            full(w_decay), full(w_iclr), full(w_gate),
        ],
        out_specs=[out_spec] * 7,
        out_shape=[jax.ShapeDtypeStruct((r, n), F32)] * 7,
        compiler_params=_params(("parallel",), need),
        name="rwkv_prep",
    )(u, u, shift0, mu.reshape(1, wu), w0.reshape(1, n), a0.reshape(1, n), k_k.reshape(1, n), k_a.reshape(1, n),
      w_decay, w_iclr, w_gate)


def _wkv_kernel(*refs, c_len, hg, gps, n_chunks, has_s0):
    if has_s0:
        (r_ref, k_ref, v_ref, kkr_ref, a_ref, lw_ref, g_ref, rk_ref, lng_ref, lnb_ref, s0_ref,
         o_ref, sout_ref, st_scr) = refs
    else:
        (r_ref, k_ref, v_ref, kkr_ref, a_ref, lw_ref, g_ref, rk_ref, lng_ref, lnb_ref,
         o_ref, sout_ref, st_scr) = refs
    n = RWKV_HEAD_DIM
    w = hg * n
    rw = hg * c_len
    shift = int(math.log2(n))
    ci = pl.program_id(2)

    lane_head = lax.broadcasted_iota(jnp.int32, (1, w), 1) >> shift
    key_head = lax.broadcasted_iota(jnp.int32, (w, 1), 0) >> shift
    same_head = key_head == lane_head
    row_head = lax.broadcasted_iota(jnp.int32, (rw, 1), 0) // c_len
    stack_mask = row_head == lane_head

    @pl.when(ci == 0)
    def _():
        for gi in range(gps):
            if has_s0:
                s0t = jnp.transpose(s0_ref[gi * hg:(gi + 1) * hg].reshape(w, n))
                st_scr[gi] = jnp.where(same_head, jnp.concatenate([s0t] * hg, axis=0), 0.0)
            else:
                st_scr[gi] = jnp.zeros((w, w), F32)

    new_states = _wkv_groups(gps, r_ref, k_ref, v_ref, kkr_ref, a_ref, lw_ref, g_ref, rk_ref, lng_ref, lnb_ref,
                             o_ref, st_scr, c_len=c_len, hg=hg, same_head=same_head, stack_mask=stack_mask)

    @pl.when(ci == n_chunks - 1)
    def _():
        for gi, st_new in enumerate(new_states):
            z = st_new[0:n]
            for e in range(1, hg):
                z = z + st_new[e * n:(e + 1) * n]
            sout_ref[gi * hg:(gi + 1) * hg] = jnp.transpose(z).reshape(hg, n, n)


def _wkv_groups(gps, r_ref, k_ref, v_ref, kkr_ref, a_ref, lw_ref, g_ref, rk_ref, lng_ref, lnb_ref, o_ref, st_scr, *,
                c_len, hg, same_head, stack_mask):
    n = RWKV_HEAD_DIM
    w = hg * n
    rw = hg * c_len
    groups = range(gps)
    lanes = [slice(gi * w, (gi + 1) * w) for gi in groups]
    each = lambda fn, *cols: [fn(*xs) for xs in zip(*cols)]

    seg_ones = jnp.where(same_head, 1.0, 0.0).astype(BF16)

    def seg_sum(x):
        hi, lo = _split_bf16(x, 2)
        return _dot(hi, seg_ones) + _dot(lo, seg_ones)

    def stack(x):
        return jnp.where(stack_mask, jnp.concatenate([x] * hg, axis=0), 0.0).astype(BF16)

    t_r = lax.broadcasted_iota(jnp.int32, (c_len, 1), 0)
    t_c = lax.broadcasted_iota(jnp.int32, (1, c_len), 1)
    tril = jnp.where(t_c <= t_r, 1.0, 0.0).astype(BF16)
    ri = lax.broadcasted_iota(jnp.int32, (rw, 1), 0)
    cj = lax.broadcasted_iota(jnp.int32, (1, rw), 1)
    strict = cj < ri
    incl = cj <= ri

    r = [r_ref[:, ls] for ls in lanes]
    k = [k_ref[:, ls] for ls in lanes]
    v = [v_ref[:, ls] for ls in lanes]
    a = [a_ref[:, ls] for ls in lanes]
    lw = [lw_ref[:, ls] for ls in lanes]
    kkr = [kkr_ref[:, ls] for ls in lanes]

    def cumsum(x):
        l1, l2, l3 = _split_bf16(x, 3)
        return _dot(tril, l1) + _dot(tril, l2) + _dot(tril, l3)

    cum = each(cumsum, lw)
    cum_end = [c[c_len - 1:c_len, :] for c in cum]
    kkn = each(lambda x: x / jnp.maximum(jnp.sqrt(seg_sum(x * x)), 1e-12), kkr)
    kka = each(lambda x, y: x * y, kkn, a)
    p_inv = each(lambda c: jnp.exp(-c), cum)
    p_tail = each(lambda ce, c: jnp.exp(ce - c), cum_end, cum)

    a_s = each(lambda c, l, x: stack(jnp.exp(c - l) * (-x)), cum, lw, kkn)
    r_s = each(lambda c, x: stack(jnp.exp(c) * x), cum, r)
    b_s = each(lambda p, x: stack(p * x), p_inv, kka)
    k_s = each(lambda p, x: stack(p * x), p_inv, k)
    v_s = each(stack, v)
    be_s = each(lambda p, x: stack(p * x), p_tail, kka)
    ke_s = each(lambda p, x: stack(p * x), p_tail, k)

    l_ab = each(lambda x, y: jnp.where(strict, _dot_nt(x, y), 0.0).astype(BF16), a_s, b_s)
    l_ak = each(lambda x, y: jnp.where(strict, _dot_nt(x, y), 0.0).astype(BF16), a_s, k_s)
    m_rb = each(lambda x, y: jnp.where(incl, _dot_nt(x, y), 0.0).astype(BF16), r_s, b_s)
    m_rk = each(lambda x, y: jnp.where(incl, _dot_nt(x, y), 0.0).astype(BF16), r_s, k_s)

    st = [st_scr[gi] for gi in groups]
    st_b = [s.astype(BF16) for s in st]
    x = each(lambda p, s, l, q: _dot(p, s) + _dot(l, q), a_s, st_b, l_ak, v_s)
    n_it = int(math.log2(c_len))
    li = l_ab
    for it in range(n_it):
        x = each(lambda xx, l: xx + _dot(l, xx.astype(BF16)), x, li)
        if it < n_it - 1:
            li = each(lambda l: _dot(l, l).astype(BF16), li)
    u_s = [xx.astype(BF16) for xx in x]

    def fold_heads(y_s):
        y = y_s[0:c_len]
        for e in range(1, hg):
            y = y + y_s[e * c_len:(e + 1) * c_len]
        return y

    y = each(lambda p, s, m1, u, m2, q: fold_heads(_dot(p, s) + _dot(m1, u) + _dot(m2, q)),
             r_s, st_b, m_rb, u_s, m_rk, v_s)

    def new_state(ce, s, b, u, kk_, q):
        pc_col = jnp.transpose(jnp.broadcast_to(jnp.exp(ce), (V7X_LANES, w)))[:, 0:1]
        return pc_col * s + _dot_tn(b, u) + _dot_tn(kk_, q)

    st_new = each(new_state, cum_end, st, be_s, u_s, ke_s, v_s)
    for gi in groups:
        st_scr[gi] = st_new[gi]

    inv_n = 1.0 / n
    mean = each(lambda yy: seg_sum(yy) * inv_n, y)
    d = each(lambda yy, m: yy - m, y, mean)
    var = each(lambda dd: seg_sum(dd * dd) * inv_n, d)
    bonus = each(lambda rr, kk_, ls, vv: seg_sum(rr * kk_ * rk_ref[:, ls]) * vv, r, k, lanes, v)
    for gi in groups:
        ls = lanes[gi]
        yn = d[gi] * lax.rsqrt(var[gi] + GN_EPS) * lng_ref[:, ls] + lnb_ref[:, ls]
        o_ref[:, ls] = (yn + bonus[gi]) * g_ref[:, ls]
    return st_new


def _wkv(r, k, v, kkr, a, lw, g, r_k, lnx_g, lnx_b, s0, *, batch, c_len, n_chunks, gps):
    rows, dim = r.shape
    hg = WKV_HG
    n = RWKV_HEAD_DIM
    w = hg * n
    wb = gps * w
    blk = pl.BlockSpec((c_len, wb), lambda b, gi, ci: (b * n_chunks + ci, gi))
    vec = pl.BlockSpec((1, wb), lambda b, gi, ci: (0, gi))
    st_spec = pl.BlockSpec((None, gps * hg, n, n), lambda b, gi, ci: (b, gi, 0, 0))
    in_specs = [blk] * 7 + [vec] * 3
    args = [r, k, v, kkr, a, lw, g, r_k.reshape(1, dim), lnx_g.reshape(1, dim), lnx_b.reshape(1, dim)]
    if s0 is not None:
        in_specs.append(st_spec)
        args.append(s0)
    rw = hg * c_len
    need = 2 * 8 * c_len * wb * 4 + gps * (5 * w * w * 4 + 16 * rw * w * 4 + 8 * rw * rw * 4)
    return pl.pallas_call(
        functools.partial(_wkv_kernel, c_len=c_len, hg=hg, gps=gps, n_chunks=n_chunks, has_s0=s0 is not None),
        grid=(batch, dim // wb, n_chunks),
        in_specs=in_specs,
        out_specs=[blk, st_spec],
        out_shape=[
            jax.ShapeDtypeStruct((rows, dim), F32),
            jax.ShapeDtypeStruct((batch, dim // n, n, n), F32),
        ],
        scratch_shapes=[pltpu.VMEM((gps, w, w), F32)],
        compiler_params=_params(("parallel", "parallel", "arbitrary"), need),
        name="wkv",
    )(*args)


def _wkv_lanes_kernel(r_ref, k_ref, v_ref, kkr_ref, a_ref, lw_ref, g_ref, rk_ref, lng_ref, lnb_ref, s0_ref,
                      o_ref, sout_ref, y_scr, *, steps):
    n = RWKV_HEAD_DIM
    decay, kkn, kka, kt, rt = [], [], [], [], []
    for t in range(steps):
        kkr = kkr_ref[t]
        norm = jnp.sqrt(jnp.sum(kkr * kkr, axis=0, keepdims=True))
        kk = kkr / jnp.maximum(norm, 1e-12)
        decay.append(jnp.exp(lw_ref[t]))
        kkn.append(kk)
        kka.append(kk * a_ref[t])
        kt.append(k_ref[t])
        rt.append(r_ref[t])

    def row(vi, carry):
        s = s0_ref[vi]
        for t in range(steps):
            sa = jnp.sum(s * kkn[t], axis=0, keepdims=True)
            s = s * decay[t] - sa * kka[t] + v_ref[t, pl.ds(vi, 1), :] * kt[t]
            y_scr[t, pl.ds(vi, 1), :] = jnp.sum(s * rt[t], axis=0, keepdims=True)
        sout_ref[vi] = s
        return carry

    lax.fori_loop(0, n, row, 0)

    for t in range(steps):
        y = y_scr[t]
        mean = jnp.mean(y, axis=0, keepdims=True)
        d = y - mean
        var = jnp.mean(d * d, axis=0, keepdims=True)
        yn = d * lax.rsqrt(var + GN_EPS) * lng_ref[...] + lnb_ref[...]
        bonus = jnp.sum(rt[t] * kt[t] * rk_ref[...], axis=0, keepdims=True) * v_ref[t]
        o_ref[t] = (yn + bonus) * g_ref[t]


def _wkv_lanes(r, k, v, kkr, a, lw, g, r_k, lnx_g, lnx_b, s0):
    steps, dim, nb = r.shape
    n = RWKV_HEAD_DIM
    heads = dim // n
    blk = pl.BlockSpec((steps, n, nb), lambda h: (0, h, 0))
    par = pl.BlockSpec((n, nb), lambda h: (h, 0))
    st = pl.BlockSpec((None, n, n, nb), lambda h: (h, 0, 0, 0))
    need = 2 * 8 * steps * n * nb * 4 + 4 * n * n * nb * 4 + 32 * n * nb * 4
    return pl.pallas_call(
        functools.partial(_wkv_lanes_kernel, steps=steps),
        grid=(heads,),
        in_specs=[blk] * 7 + [par] * 3 + [st],
        out_specs=[blk, st],
        out_shape=[jax.ShapeDtypeStruct((steps, dim, nb), F32), jax.ShapeDtypeStruct((heads, n, n, nb), F32)],
        scratch_shapes=[pltpu.VMEM((steps, n, nb), F32)],
        compiler_params=_params(("parallel",), need),
        name="wkv_lanes",
    )(r, k, v, kkr, a, lw, g, r_k, lnx_g, lnx_b, s0)


def _rope_tables(pos):
    inv = ROPE_THETA ** (-jnp.arange(0, QK_ROPE_DIM, 2, dtype=F32) / QK_ROPE_DIM)
    ang = pos[:, None] * inv[None, :]
    cos, sin = jnp.cos(ang), jnp.sin(ang)
    return jnp.concatenate([cos] * 4, axis=1), jnp.concatenate([-sin, sin] * 2, axis=1)


def _swap_halves(w):
    half = w.shape[-1] // 2
    return jnp.concatenate([w[..., half:], w[..., :half]], axis=-1)


def _prepare_weights(w_in, w_uq, w_ukv, w_o, w_decay, w_iclr, w_gate):
    d = w_in.shape[0]
    rope = w_in[:, Q_LORA_RANK + KV_LORA_RANK:W_MLA_IN]
    rope_sw = _swap_halves(rope)
    zeros = lambda c: jnp.zeros((d, c), w_in.dtype)
    w_mla = jnp.concatenate(
        [w_in[:, :Q_LORA_RANK], zeros(MLA_KV_OFF - Q_LORA_RANK), w_in[:, Q_LORA_RANK:Q_LORA_RANK + KV_LORA_RANK],
         rope, rope, rope_sw, rope_sw, zeros(MLA_COLS - MLA_KV_OFF - KV_LORA_RANK - 4 * QK_ROPE_DIM)], axis=1).astype(BF16)
    w_rwkv = w_in[:, W_MLA_IN:].astype(BF16)
    uq = w_uq.reshape(Q_LORA_RANK, MLA_HEADS, QK_NOPE_DIM + QK_ROPE_DIM)
    uq_rope = uq[..., QK_NOPE_DIM:]
    w_q = jnp.concatenate(
        [uq[..., :QK_NOPE_DIM].reshape(Q_LORA_RANK, -1), uq_rope.reshape(Q_LORA_RANK, -1),
         _swap_halves(uq_rope).reshape(Q_LORA_RANK, -1)], axis=1).astype(BF16)
    ukv = w_ukv.reshape(KV_LORA_RANK, MLA_HEADS, QK_NOPE_DIM + V_HEAD_DIM)
    w_uk, w_uv = ukv[..., :QK_NOPE_DIM], ukv[..., QK_NOPE_DIM:]
    w_kvup = jnp.concatenate([w_uk.reshape(KV_LORA_RANK, -1), w_uv.reshape(KV_LORA_RANK, -1)], axis=1).astype(BF16)
    w_uk_t = jnp.transpose(w_uk, (1, 2, 0)).astype(BF16)
    w_uv_h = jnp.transpose(w_uv, (1, 0, 2)).astype(BF16)
    return dict(w_mla=_tile_major(w_mla, MM_TN), w_rwkv=_tile_major(w_rwkv, MM_TN), w_q=_tile_major(w_q, MM_TN),
                w_kvup=_tile_major(w_kvup, MM_TN), w_uk_t=w_uk_t, w_uv_h=w_uv_h,
                w_o=w_o.astype(BF16), w_decay=w_decay.astype(BF16), w_iclr=w_iclr.astype(BF16),
                w_gate=w_gate.astype(BF16))


def _group_layer(x, mods, tiles_per_group, tm, wts, p, *, cos4, sin4, ffn, attend, wkv_run, shift0, seq_len):
    sh2, sc2, g2 = mods
    x1 = ffn(0, x)
    mla = _mm(x1, wts["w_mla"], tm=tm, mod=(sh2, sc2), tiles_per_group=tiles_per_group, name="proj_mla")
    u = _mm(x1, wts["w_rwkv"], tm=tm, mod=(sh2, sc2), tiles_per_group=tiles_per_group, name="proj_rwkv")
    ckv, kr, kr2 = _lat_prep(mla, p["g_kv"], cos4, sin4, tm=tm)
    qall = _mm(mla, wts["w_q"], tm=tm, k=Q_LORA_RANK, rms_g=p["g_q"], name="q_proj")
    q_rope = _rope_q(qall, cos4, sin4, tm=tm)
    attn = attend(qall, q_rope, ckv, kr, kr2)
    prep = _rwkv_prep(u, shift0, p["mu_shift"], p["w0"], p["a0"], p["k_k"], p["k_a"],
                      wts["w_decay"], wts["w_iclr"], wts["w_gate"], tm=PREP_TM, seq_len=seq_len)
    rwkv, s_new = wkv_run(prep)
    x2 = _oproj(x1, g2, attn, rwkv, wts["w_o"], p["ln_g"][1], p["ln_b"][1], tm=tm, tk=256,
                tiles_per_group=tiles_per_group)
    return ffn(1, x2), ckv, kr, s_new, u


def kernel(x_prompt, x_sample, c_prompt, c_sample, cache_kv_latent, cache_k_rope, state_wkv, state_shift, page_table, w_ada, b_ada, ln_g, ln_b, w_ffn1_in, w_ffn1_out, w_ffn2_in, w_ffn2_out, w_in, g_q, g_kv, w_uq, w_ukv, mu_shift, w0, w_decay, a0, w_iclr, w_gate, k_k, k_a, r_k, lnx_g, lnx_b, w_o):
    batch, seq, d = x_prompt.shape
    db, dseq, _ = x_sample.shape
    depth = w_ada.shape[0]
    assert depth == DEPTH == 1
    n_ada = w_ada.shape[2] // d
    past = page_table.shape[1] * PAGE_SIZE
    wu = state_shift.shape[2]

    c_all = jnp.concatenate([c_prompt, c_sample], axis=0)
    cos_p, sin_p = _rope_tables(jnp.arange(seq, dtype=F32) + 0)
    pos_s = jnp.tile(jnp.arange(dseq, dtype=F32) + past, db)
    cos_s, sin_s = _rope_tables(pos_s)

    yp = x_prompt.reshape(batch * seq, d)
    ys = x_sample.reshape(db * dseq, d)
    outs_p, outs_s = [], []
    for l in range(depth):
        p = dict(ln_g=ln_g[l], ln_b=ln_b[l], g_q=g_q[l], g_kv=g_kv[l], mu_shift=mu_shift[l], w0=w0[l], a0=a0[l],
                 k_k=k_k[l], k_a=k_a[l], r_k=r_k[l], lnx_g=lnx_g[l], lnx_b=lnx_b[l])
        wts = _prepare_weights(w_in[l], w_uq[l], w_ukv[l], w_o[l], w_decay[l], w_iclr[l], w_gate[l])
        ffn_w32 = ((w_ffn1_in[l], w_ffn1_out[l]), (w_ffn2_in[l], w_ffn2_out[l]))
        ffn_w16 = [None, None]
        ada = _ada(c_all, w_ada[l], b_ada[l]).reshape(batch + db, n_ada, d)
        mods_p = [ada[:batch, i][:, None, :] for i in range(n_ada)]
        mods_s = [ada[batch:, i][None] for i in range(n_ada)]
        per_token = lambda m: jnp.repeat(m[0], dseq, axis=0)[None]

        tm_s = db * dseq
        c_pad = 16
        t_major = lambda a: a.reshape(db, dseq, d).transpose(1, 0, 2).reshape(tm_s, d)
        b_major = lambda a: a.reshape(dseq, db, d).transpose(1, 0, 2).reshape(tm_s, d)

        def ffn_s(i, x):
            sh, sc, g = mods_s[6 * i:6 * i + 3]
            y, w_in16, w_out16 = _ffn(t_major(x), sh, sc, g, *ffn_w32[i], p["ln_g"][2 * i], p["ln_b"][2 * i],
                                      tm=tm_s, tf=128, tiles_per_group=1, emit_bf16_weights=True)
            ffn_w16[i] = (w_in16, w_out16)
            return b_major(y)

        def attend_s(qall, q_rope, ckv, kr, kr2):
            rows = dseq * MLA_HEADS
            q_lat = _bmm(qall, wts["w_uk_t"], tm=tm_s, out_dtype=BF16, name="q_absorb")
            pad = lambda t: jnp.pad(t.reshape(db, dseq, -1), ((0, 0), (0, c_pad - dseq), (0, 0)))
            o_lat = _paged(page_table, q_lat.reshape(db, rows, KV_LORA_RANK), q_rope.reshape(db, rows, QK_ROPE_DIM),
                           pad(ckv), pad(kr), cache_kv_latent, jnp.swapaxes(cache_k_rope, 2, 3),
                           npg=PAGES_PER_STEP, new_len=dseq)
            return _bmm(o_lat.reshape(db * dseq, MLA_HEADS * KV_LORA_RANK), wts["w_uv_h"], tm=tm_s, name="v_up")

        def wkv_s(prep):
            lanes = lambda a: a.reshape(db, dseq, -1).transpose(1, 2, 0)
            over_batch = lambda v: jnp.broadcast_to(v.reshape(-1, 1), (v.size, db))
            out, s_new = _wkv_lanes(*[lanes(t) for t in prep], over_batch(p["r_k"]), over_batch(p["lnx_g"]),
                                    over_batch(p["lnx_b"]), jnp.transpose(state_wkv[l], (1, 2, 3, 0)))
            return out.transpose(2, 0, 1).reshape(db * dseq, -1), jnp.transpose(s_new, (3, 0, 1, 2))

        shift_rows = jnp.repeat(state_shift[l], dseq, axis=0).reshape(db * dseq // PREP_TM, PREP_TM, wu)
        ys, ckv_s, kr_s, s_s, u_s = _group_layer(
            ys, [per_token(m) for m in mods_s[3:6]], 1, tm_s, wts, p, cos4=cos_s, sin4=sin_s, ffn=ffn_s,
            attend=attend_s, wkv_run=wkv_s, shift0=shift_rows, seq_len=dseq)
        outs_s.append((ckv_s.reshape(db, dseq, -1), kr_s.reshape(db, dseq, -1), s_s,
                       u_s.reshape(db, dseq, wu)[:, -1]))

        tm_p = 512

        def ffn_p(i, x):
            sh, sc, g = mods_p[6 * i:6 * i + 3]
            return _ffn(x, sh, sc, g, *ffn_w16[i], p["ln_g"][2 * i], p["ln_b"][2 * i],
                        tm=tm_p, tf=FFN_TF, tiles_per_group=seq // tm_p)

        def attend_p(qall, q_rope, ckv, kr, kr2):
            kvup = _mm(ckv, wts["w_kvup"], tm=tm_p, name="kv_up")
            return _flash(qall, q_rope, kvup, kr2, batch=batch, seq=seq, tq=512)

        def wkv_p(prep):
            return _wkv(*prep, p["r_k"], p["lnx_g"], p["lnx_b"], None, batch=batch, c_len=64, n_chunks=seq // 64,
                        gps=4)

        yp, ckv_p, kr_p, s_p, u_p = _group_layer(
            yp, mods_p[3:6], seq // tm_p, tm_p, wts, p, cos4=cos_p, sin4=sin_p, ffn=ffn_p, attend=attend_p,
            wkv_run=wkv_p, shift0=jnp.zeros((batch, 1, wu), F32), seq_len=seq)
        outs_p.append((ckv_p.reshape(batch, seq, -1), kr_p.reshape(batch, seq, -1), s_p,
                       u_p.reshape(batch, seq, wu)[:, -1]))

    stack = lambda outs, i: jnp.stack([o[i] for o in outs])
    return (yp.reshape(batch, seq, d), ys.reshape(db, dseq, d),
            stack(outs_p, 0), stack(outs_p, 1), stack(outs_p, 2), stack(outs_p, 3),
            stack(outs_s, 0), stack(outs_s, 1), stack(outs_s, 2), stack(outs_s, 3))
```

```python
import functools
import math

import jax
import jax.numpy as jnp
from jax import lax
from jax.experimental import pallas as pl
from jax.experimental.pallas import tpu as pltpu

F32 = jnp.float32
BF16 = jnp.bfloat16

QK_NOPE_DIM = 128
QK_ROPE_DIM = 64
V_HEAD_DIM = 128
MLA_HEADS = 16
Q_LORA_RANK = 896
KV_LORA_RANK = 512
RWKV_HEAD_DIM = 64
RWKV_HEADS = 32
RWKV_DIM = RWKV_HEADS * RWKV_HEAD_DIM
DECAY_LORA = 128
ICLR_LORA = 128
GATE_LORA = 256
W_MLA_IN = Q_LORA_RANK + KV_LORA_RANK + QK_ROPE_DIM
PAGE_SIZE = 128
ROPE_THETA = 10000.0
SM_SCALE = (QK_NOPE_DIM + QK_ROPE_DIM) ** -0.5
DEPTH = 1
ALPHA = (2 * DEPTH) ** 0.25
LN_EPS = 1e-5
RMS_EPS = 1e-6
GN_EPS = 64e-5

V7X_VMEM_BYTES = 64 * 1024 * 1024
V7X_LANES = 128
SUBLANES = 8
MIB = 1024 * 1024

MLA_COLS = 2048
MLA_KV_OFF = 1024

ROW_CHUNK = 64
N_CHUNK = 512
MM_TN = 512
FFN_TF = 256
PREP_TM = 128
PAGES_PER_STEP = 32
WKV_HG = 4


def _params(sem, vmem_bytes):
    limit = min(int(vmem_bytes) + 8 * MIB, V7X_VMEM_BYTES - 6 * MIB)
    return pltpu.CompilerParams(dimension_semantics=sem, vmem_limit_bytes=limit)


def _dot(a, b):
    return jnp.dot(a, b, preferred_element_type=F32)


def _dot_nt(a, b):
    return lax.dot_general(a, b, (((1,), (1,)), ((), ())), preferred_element_type=F32)


def _dot_tn(a, b):
    return lax.dot_general(a, b, (((0,), (0,)), ((), ())), preferred_element_type=F32)


def _split_bf16(x, parts):
    out = []
    rem = x
    for _ in range(parts):
        h = rem.astype(BF16)
        out.append(h)
        rem = rem - h.astype(F32)
    return out


def _layernorm_rows(z, g, b):
    mu = jnp.mean(z, axis=-1, keepdims=True)
    d = z - mu
    var = jnp.mean(d * d, axis=-1, keepdims=True)
    return d * lax.rsqrt(var + LN_EPS) * g + b


def _for_row_chunks(n_rows, fn, rows=ROW_CHUNK):
    def body(i, carry):
        fn(pl.ds(pl.multiple_of(i * rows, rows), rows))
        return carry

    lax.fori_loop(0, n_rows // rows, body, 0)


def _rows_of(ref, rs):
    mr = ref.shape[0]
    if mr == 1:
        return ref[...]
    return ref[pl.ds(pl.multiple_of(rs.start % mr, rs.size), rs.size), :]


def _modulate_into(h_scr, x_ref, sh_ref, sc_ref):
    def chunk(rs):
        h_scr[rs, :] = (x_ref[rs, :] * (1.0 + _rows_of(sc_ref, rs)) + _rows_of(sh_ref, rs)).astype(BF16)

    _for_row_chunks(h_scr.shape[0], chunk)


def _residual_layernorm(o_ref, x_ref, gate_ref, lng_ref, lnb_ref, gate_scale):
    def chunk(rs):
        z = ALPHA * x_ref[rs, :] + gate_scale * _rows_of(gate_ref, rs) * o_ref[rs, :]
        o_ref[rs, :] = _layernorm_rows(z, lng_ref[...], lnb_ref[...])

    _for_row_chunks(o_ref.shape[0], chunk)


def _ada_kernel(c_ref, w_ref, b_ref, o_ref):
    c = c_ref[...]
    a = (c * jax.nn.sigmoid(c)).astype(BF16)
    o_ref[...] = _dot(a, w_ref[...].astype(BF16)) + b_ref[...]


def _ada(c_all, w_ada, b_ada, tn=512):
    m, k = c_all.shape
    n = w_ada.shape[1]
    need = 2 * (k * tn * 4) + k * tn * 2 + 2 * m * k * 4 + 4 * m * tn * 4
    return pl.pallas_call(
        _ada_kernel,
        grid=(n // tn,),
        in_specs=[
            pl.BlockSpec((m, k), lambda j: (0, 0)),
            pl.BlockSpec((k, tn), lambda j: (0, j)),
            pl.BlockSpec((1, tn), lambda j: (0, j)),
        ],
        out_specs=pl.BlockSpec((m, tn), lambda j: (0, j)),
        out_shape=jax.ShapeDtypeStruct((m, n), F32),
        compiler_params=_params(("parallel",), need),
        name="ada",
    )(c_all, w_ada, b_ada.reshape(1, n))


def _ffn_kernel(x_ref, sh_ref, sc_ref, g_ref, wg_ref, wu_ref, wo_ref, lng_ref, lnb_ref, o_ref, *rest, nj):
    h_scr = rest[-1]
    j = pl.program_id(1)

    @pl.when(j == 0)
    def _():
        _modulate_into(h_scr, x_ref, sh_ref, sc_ref)
        o_ref[...] = jnp.zeros_like(o_ref)

    wg, wu, wo = wg_ref[...], wu_ref[...], wo_ref[...]
    if len(rest) > 1:
        wg, wu, wo = wg.astype(BF16), wu.astype(BF16), wo.astype(BF16)
        rest[0][...] = wg
        rest[1][...] = wu
        rest[2][...] = wo
    h = h_scr[...]
    gate = _dot(h, wg)
    up = _dot(h, wu)
    act = (gate * jax.nn.sigmoid(gate) * up).astype(BF16)
    for n0 in range(0, o_ref.shape[1], N_CHUNK):
        o_ref[:, n0:n0 + N_CHUNK] += _dot(act, wo[:, n0:n0 + N_CHUNK])

    @pl.when(j == nj - 1)
    def _():
        _residual_layernorm(o_ref, x_ref, g_ref, lng_ref, lnb_ref, 0.5)


def _ffn(x, sh, sc, g, w_in, w_out, ln_g, ln_b, *, tm, tf, tiles_per_group, emit_bf16_weights=False):
    r, d = x.shape
    ff = w_out.shape[0]
    nj = ff // tf
    mr = sh.shape[1]
    if isinstance(w_in, tuple):
        w_gate, w_up = w_in
        assert w_gate.shape[2] == tf
        gate_spec = up_spec = pl.BlockSpec((None, d, tf), lambda i, j: (j, 0, 0))
    else:
        w_gate = w_up = w_in
        gate_spec = pl.BlockSpec((d, tf), lambda i, j: (0, j))
        up_spec = pl.BlockSpec((d, tf), lambda i, j: (0, nj + j))
    wbytes = w_out.dtype.itemsize
    mod_spec = pl.BlockSpec((None, mr, d), lambda i, j: (i // tiles_per_group, 0, 0))
    out_specs = [pl.BlockSpec((tm, d), lambda i, j: (i, 0))]
    out_shape = [jax.ShapeDtypeStruct((r, d), F32)]
    need = 4 * tm * d * 4 + tm * d * 2 + 2 * 3 * d * tf * wbytes + 6 * mr * d * 4 + 6 * tm * tf * 4
    if emit_bf16_weights:
        assert r == tm and w_out.dtype == F32 and FFN_TF % tf == 0
        per_tile = FFN_TF // tf
        emit_spec = pl.BlockSpec((None, d, tf), lambda i, j: (j // per_tile, 0, j % per_tile))
        out_specs += [emit_spec, emit_spec, pl.BlockSpec((tf, d), lambda i, j: (j, 0))]
        out_shape += [jax.ShapeDtypeStruct((ff // FFN_TF, d, FFN_TF), BF16)] * 2 + [jax.ShapeDtypeStruct((ff, d), BF16)]
        need += 3 * 3 * d * tf * 2
    outs = pl.pallas_call(
        functools.partial(_ffn_kernel, nj=nj),
        grid=(r // tm, nj),
        in_specs=[
            pl.BlockSpec((tm, d), lambda i, j: (i, 0)),
            mod_spec, mod_spec, mod_spec,
            gate_spec, up_spec,
            pl.BlockSpec((tf, d), lambda i, j: (j, 0)),
            pl.BlockSpec((1, d), lambda i, j: (0, 0)),
            pl.BlockSpec((1, d), lambda i, j: (0, 0)),
        ],
        out_specs=out_specs,
        out_shape=out_shape,
        scratch_shapes=[pltpu.VMEM((tm, d), BF16)],
        compiler_params=_params(("parallel", "arbitrary"), need),
        name="ffn",
    )(x, sh, sc, g, w_gate, w_up, w_out, ln_g.reshape(1, d), ln_b.reshape(1, d))
    return (outs[0], (outs[1], outs[2]), outs[3]) if emit_bf16_weights else outs[0]


def _mm_mod_kernel(a_ref, sh_ref, sc_ref, w_ref, o_ref, h_scr):
    @pl.when(pl.program_id(1) == 0)
    def _():
        _modulate_into(h_scr, a_ref, sh_ref, sc_ref)

    o_ref[...] = _dot(h_scr[...], w_ref[...]).astype(o_ref.dtype)


def _mm_rms_kernel(a_ref, g_ref, w_ref, o_ref, h_scr):
    @pl.when(pl.program_id(1) == 0)
    def _():
        a = a_ref[...]
        ms = jnp.mean(a * a, axis=-1, keepdims=True)
        h_scr[...] = (a * lax.rsqrt(ms + RMS_EPS) * g_ref[...]).astype(BF16)

    o_ref[...] = _dot(h_scr[...], w_ref[...]).astype(o_ref.dtype)


def _mm_plain_kernel(a_ref, w_ref, o_ref, h_scr):
    @pl.when(pl.program_id(1) == 0)
    def _():
        h_scr[...] = a_ref[...].astype(BF16)

    o_ref[...] = _dot(h_scr[...], w_ref[...]).astype(o_ref.dtype)


def _tile_major(w, tn):
    k, n = w.shape
    return w.reshape(k, n // tn, tn).transpose(1, 0, 2)


def _mm(a, w, *, tm, k=None, a_colblk=0, mod=None, rms_g=None, tiles_per_group=1, out_dtype=F32, name="mm"):
    r = a.shape[0]
    k = a.shape[1] if k is None else k
    tn = w.shape[2]
    n = w.shape[0] * tn
    a_spec = pl.BlockSpec((tm, k), lambda i, j: (i, a_colblk))
    w_spec = pl.BlockSpec((None, k, tn), lambda i, j: (j, 0, 0))
    need = 2 * tm * k * a.dtype.itemsize + tm * k * 2 + 2 * k * tn * 2 + 4 * tm * tn * 4
    if mod is not None:
        sh, sc = mod
        mr = sh.shape[1]
        mod_spec = pl.BlockSpec((None, mr, k), lambda i, j: (i // tiles_per_group, 0, 0))
        kern, ins, args = _mm_mod_kernel, [a_spec, mod_spec, mod_spec, w_spec], (a, sh, sc, w)
        need += 4 * mr * k * 4
    elif rms_g is not None:
        kern, ins, args = _mm_rms_kernel, [a_spec, pl.BlockSpec((1, k), lambda i, j: (0, 0)), w_spec], (a, rms_g.reshape(1, k), w)
    else:
        kern, ins, args = _mm_plain_kernel, [a_spec, w_spec], (a, w)
    return pl.pallas_call(
        kern,
        grid=(r // tm, n // tn),
        in_specs=ins,
        out_specs=pl.BlockSpec((tm, tn), lambda i, j: (i, j)),
        out_shape=jax.ShapeDtypeStruct((r, n), out_dtype),
        scratch_shapes=[pltpu.VMEM((tm, k), BF16)],
        compiler_params=_params(("parallel", "arbitrary"), need),
        name=name,
    )(*args)


def _bmm_kernel(a_ref, w_ref, o_ref):
    o_ref[...] = _dot(a_ref[...].astype(BF16), w_ref[...]).astype(o_ref.dtype)


def _bmm(a, w, *, tm, out_dtype=F32, name="bmm"):
    r = a.shape[0]
    h, ka, nb = w.shape
    need = 2 * tm * ka * 4 + 2 * ka * nb * 2 + 2 * tm * nb * 4
    return pl.pallas_call(
        _bmm_kernel,
        grid=(r // tm, h),
        in_specs=[
            pl.BlockSpec((tm, ka), lambda i, hh: (i, hh)),
            pl.BlockSpec((None, ka, nb), lambda i, hh: (hh, 0, 0)),
        ],
        out_specs=pl.BlockSpec((tm, nb), lambda i, hh: (i, hh)),
        out_shape=jax.ShapeDtypeStruct((r, h * nb), out_dtype),
        compiler_params=_params(("parallel", "parallel"), need),
        name=name,
    )(a, w)


def _oproj_kernel(x_ref, g_ref, a1_ref, a2_ref, w1_ref, w2_ref, lng_ref, lnb_ref, o_ref, *, nk):
    kk = pl.program_id(1)

    @pl.when(kk == 0)
    def _():
        o_ref[...] = jnp.zeros_like(o_ref)

    a1 = a1_ref[...].astype(BF16)
    a2 = a2_ref[...].astype(BF16)
    for n0 in range(0, o_ref.shape[1], N_CHUNK):
        cols = slice(n0, n0 + N_CHUNK)
        o_ref[:, cols] += _dot(a1, w1_ref[:, cols]) + _dot(a2, w2_ref[:, cols])

    @pl.when(kk == nk - 1)
    def _():
        _residual_layernorm(o_ref, x_ref, g_ref, lng_ref, lnb_ref, 1.0)


def _oproj(x, g, attn, rwkv, w_o, ln_g, ln_b, *, tm, tk, tiles_per_group):
    r, d = x.shape
    half = attn.shape[1]
    nk = half // tk
    mr = g.shape[1]
    need = 4 * tm * d * 4 + 2 * mr * d * 4 + 4 * tm * tk * 4 + 4 * tk * d * 2
    return pl.pallas_call(
        functools.partial(_oproj_kernel, nk=nk),
        grid=(r // tm, nk),
        in_specs=[
            pl.BlockSpec((tm, d), lambda i, kk: (i, 0)),
            pl.BlockSpec((None, mr, d), lambda i, kk: (i // tiles_per_group, 0, 0)),
            pl.BlockSpec((tm, tk), lambda i, kk: (i, kk)),
            pl.BlockSpec((tm, tk), lambda i, kk: (i, kk)),
            pl.BlockSpec((tk, d), lambda i, kk: (kk, 0)),
            pl.BlockSpec((tk, d), lambda i, kk: (nk + kk, 0)),
            pl.BlockSpec((1, d), lambda i, kk: (0, 0)),
            pl.BlockSpec((1, d), lambda i, kk: (0, 0)),
        ],
        out_specs=pl.BlockSpec((tm, d), lambda i, kk: (i, 0)),
        out_shape=jax.ShapeDtypeStruct((r, d), F32),
        compiler_params=_params(("parallel", "arbitrary"), need),
        name="oproj",
    )(x, g, attn, rwkv, w_o, w_o, ln_g.reshape(1, d), ln_b.reshape(1, d))


def _lat_kernel(m_ref, g_ref, cos_ref, sin_ref, ckv_ref, kr_ref, kr2_ref):
    m = m_ref[...]
    kv = m[:, :KV_LORA_RANK]
    ms = jnp.mean(kv * kv, axis=-1, keepdims=True)
    ckv_ref[...] = kv * lax.rsqrt(ms + RMS_EPS) * g_ref[...]
    a2 = m[:, KV_LORA_RANK:KV_LORA_RANK + V7X_LANES]
    b2 = m[:, KV_LORA_RANK + V7X_LANES:KV_LORA_RANK + 2 * V7X_LANES]
    kr2 = a2 * cos_ref[...] + b2 * sin_ref[...]
    kr2_ref[...] = kr2
    kr_ref[...] = kr2[:, :QK_ROPE_DIM]


def _lat_prep(mla, g_kv, cos4, sin4, *, tm):
    r = mla.shape[0]
    nb = cos4.shape[0] // tm
    wblk = MLA_COLS - MLA_KV_OFF
    need = 2 * tm * wblk * 4 + 2 * tm * (KV_LORA_RANK + 3 * V7X_LANES + 2 * V7X_LANES) * 4
    return pl.pallas_call(
        _lat_kernel,
        grid=(r // tm,),
        in_specs=[
            pl.BlockSpec((tm, wblk), lambda i: (i, MLA_KV_OFF // wblk)),
            pl.BlockSpec((1, KV_LORA_RANK), lambda i: (0, 0)),
            pl.BlockSpec((tm, V7X_LANES), lambda i: (i % nb, 0)),
            pl.BlockSpec((tm, V7X_LANES), lambda i: (i % nb, 0)),
        ],
        out_specs=[
            pl.BlockSpec((tm, KV_LORA_RANK), lambda i: (i, 0)),
            pl.BlockSpec((tm, QK_ROPE_DIM), lambda i: (i, 0)),
            pl.BlockSpec((tm, V7X_LANES), lambda i: (i, 0)),
        ],
        out_shape=[
            jax.ShapeDtypeStruct((r, KV_LORA_RANK), F32),
            jax.ShapeDtypeStruct((r, QK_ROPE_DIM), F32),
            jax.ShapeDtypeStruct((r, V7X_LANES), F32),
        ],
        compiler_params=_params(("parallel",), need),
        name="lat_prep",
    )(mla, g_kv.reshape(1, KV_LORA_RANK), cos4, sin4)


def _ropeq_kernel(a_ref, b_ref, cos_ref, sin_ref, o_ref):
    reps = a_ref.shape[1] // V7X_LANES
    c = jnp.concatenate([cos_ref[...]] * reps, axis=1)
    s = jnp.concatenate([sin_ref[...]] * reps, axis=1)
    o_ref[...] = (a_ref[...] * c + b_ref[...] * s).astype(o_ref.dtype)


def _rope_q(qall, cos4, sin4, *, tm):
    r = qall.shape[0]
    w = MLA_HEADS * QK_ROPE_DIM
    nb = cos4.shape[0] // tm
    need = 4 * tm * w * 4 + 2 * tm * w * 2 + 4 * tm * w * 4
    return pl.pallas_call(
        _ropeq_kernel,
        grid=(r // tm,),
        in_specs=[
            pl.BlockSpec((tm, w), lambda i: (i, 2)),
            pl.BlockSpec((tm, w), lambda i: (i, 3)),
            pl.BlockSpec((tm, V7X_LANES), lambda i: (i % nb, 0)),
            pl.BlockSpec((tm, V7X_LANES), lambda i: (i % nb, 0)),
        ],
        out_specs=pl.BlockSpec((tm, w), lambda i: (i, 0)),
        out_shape=jax.ShapeDtypeStruct((r, w), F32),
        compiler_params=_params(("parallel",), need),
        name="rope_q",
    )(qall, qall, cos4, sin4)


def _flash_kernel(qt_ref, kt_ref, qn_ref, qr_ref, kn_ref, kr_ref, v_ref, o_ref, m_scr, l_scr, acc_scr, *, tq, tk):
    qi = qt_ref[pl.program_id(2)]
    ki = kt_ref[pl.program_id(2)]

    @pl.when(ki == 0)
    def _():
        m_scr[...] = jnp.full_like(m_scr, -jnp.inf)
        l_scr[...] = jnp.zeros_like(l_scr)
        acc_scr[...] = jnp.zeros_like(acc_scr)

    def step(on_diagonal):
        qn = (qn_ref[...] * SM_SCALE).astype(BF16)
        qr = qr_ref[...] * SM_SCALE
        kn = kn_ref[...].astype(BF16)
        kr2 = kr_ref[...].astype(BF16)
        v = v_ref[...].astype(BF16)
        lane = lax.broadcasted_iota(jnp.int32, (1, V7X_LANES), 1)
        if on_diagonal:
            causal = lax.broadcasted_iota(jnp.int32, (tq, 1), 0) >= lax.broadcasted_iota(jnp.int32, (1, tk), 1)
        heads = range(2)
        hs = [slice(e * QK_NOPE_DIM, (e + 1) * QK_NOPE_DIM) for e in heads]
        qre = [jnp.where((lane >= QK_ROPE_DIM) if e else (lane < QK_ROPE_DIM), qr, 0.0).astype(BF16) for e in heads]
        s = [_dot_nt(qn[:, hs[e]], kn[:, hs[e]]) + _dot_nt(qre[e], kr2) for e in heads]
        if on_diagonal:
            s = [jnp.where(causal, s[e], -jnp.inf) for e in heads]
        m_prev = [m_scr[e] for e in heads]
        m_new = [jnp.maximum(m_prev[e], jnp.max(s[e], axis=-1, keepdims=True)) for e in heads]
        corr = [jnp.exp(m_prev[e] - m_new[e]) for e in heads]
        p = [jnp.exp(s[e] - m_new[e]) for e in heads]
        l_new = [l_scr[e] * corr[e] + jnp.sum(p[e], axis=-1, keepdims=True) for e in heads]
        acc_new = [acc_scr[e] * corr[e] + _dot(p[e].astype(BF16), v[:, hs[e]]) for e in heads]
        for e in heads:
            if on_diagonal:
                o_ref[:, e * V_HEAD_DIM:(e + 1) * V_HEAD_DIM] = acc_new[e] / l_new[e]
            else:
                l_scr[e] = l_new[e]
                acc_scr[e] = acc_new[e]
                m_scr[e] = m_new[e]

    @pl.when(ki < qi)
    def _():
        step(False)

    @pl.when(ki == qi)
    def _():
        step(True)


def _flash(qall, q_rope, kvup, kr2, *, batch, seq, tq):
    tk = tq
    nq = seq // tq
    r = qall.shape[0]
    hp = MLA_HEADS // 2
    need = 2 * (tq * 256 * 4 + tq * 128 * 2 + 2 * tk * 256 * 4 + tk * 128 * 4 + tq * 256 * 4) + 4 * tq * 128 * 4 + 8 * tq * tk * 4
    pairs = [(qi, ki) for qi in range(nq) for ki in range(qi + 1)]
    q_tab = jnp.asarray([qk[0] for qk in pairs], jnp.int32)
    k_tab = jnp.asarray([qk[1] for qk in pairs], jnp.int32)
    grid_spec = pltpu.PrefetchScalarGridSpec(
        num_scalar_prefetch=2,
        grid=(batch, hp, len(pairs)),
        in_specs=[
            pl.BlockSpec((tq, 2 * QK_NOPE_DIM), lambda b, h, p, qt, kt: (b * nq + qt[p], h)),
            pl.BlockSpec((tq, V7X_LANES), lambda b, h, p, qt, kt: (b * nq + qt[p], h)),
            pl.BlockSpec((tk, 2 * QK_NOPE_DIM), lambda b, h, p, qt, kt: (b * nq + kt[p], h)),
            pl.BlockSpec((tk, V7X_LANES), lambda b, h, p, qt, kt: (b * nq + kt[p], 0)),
            pl.BlockSpec((tk, 2 * V_HEAD_DIM), lambda b, h, p, qt, kt: (b * nq + kt[p], hp + h)),
        ],
        out_specs=pl.BlockSpec((tq, 2 * V_HEAD_DIM), lambda b, h, p, qt, kt: (b * nq + qt[p], h)),
        scratch_shapes=[
            pltpu.VMEM((2, tq, 1), F32),
            pltpu.VMEM((2, tq, 1), F32),
            pltpu.VMEM((2, tq, V_HEAD_DIM), F32),
        ],
    )
    return pl.pallas_call(
        functools.partial(_flash_kernel, tq=tq, tk=tk),
        grid_spec=grid_spec,
        out_shape=jax.ShapeDtypeStruct((r, MLA_HEADS * V_HEAD_DIM), F32),
        compiler_params=_params(("parallel", "parallel", "arbitrary"), need),
        name="flash",
    )(q_tab, k_tab, qall, q_rope, kvup, kr2, kvup)


def _paged_kernel(pt_ref, ql_ref, qr_ref, cn_ref, krn_ref, lat_hbm, krt_hbm, o_ref,
                  lat_buf, krt_buf, sem, m_scr, l_scr, acc_scr, *, npg, nchunk, new_len, heads):
    c = pl.program_id(1)
    step = pl.program_id(0) * nchunk + c
    n_steps = pl.num_programs(0) * nchunk
    slot = step % 2

    def page_copies(step_, slot_):
        cps = []
        for i in range(npg):
            page = pt_ref[step_ * npg + i]
            cps.append(pltpu.make_async_copy(lat_hbm.at[0, page], lat_buf.at[slot_, i], sem.at[0, slot_]))
            cps.append(pltpu.make_async_copy(krt_hbm.at[0, page], krt_buf.at[slot_, i], sem.at[1, slot_]))
        return cps

    @pl.when(step == 0)
    def _():
        for cp in page_copies(step, slot):
            cp.start()

    @pl.when(step + 1 < n_steps)
    def _():
        for cp in page_copies(step + 1, 1 - slot):
            cp.start()

    @pl.when(c == 0)
    def _():
        m_scr[...] = jnp.full_like(m_scr, -jnp.inf)
        l_scr[...] = jnp.zeros_like(l_scr)
        acc_scr[...] = jnp.zeros_like(acc_scr)

    ql = ql_ref[...]
    qr = qr_ref[...].astype(BF16)

    def update(state, s, vals):
        m_prev, l_prev, acc_prev = state
        m_new = jnp.maximum(m_prev, jnp.max(s, axis=-1, keepdims=True))
        corr = jnp.exp(m_prev - m_new)
        p = jnp.exp(s - m_new)
        w = s.shape[1] // len(vals)
        pv = _dot(p[:, :w].astype(BF16), vals[0])
        for i in range(1, len(vals)):
            pv += _dot(p[:, i * w:(i + 1) * w].astype(BF16), vals[i])
        return m_new, l_prev * corr + jnp.sum(p, axis=-1, keepdims=True), acc_prev * corr + pv

    for cp in page_copies(step, slot):
        cp.wait()
    kls = [lat_buf[slot, i].astype(BF16) for i in range(npg)]

    def scores(lo, hi):
        return jnp.concatenate(
            [_dot_nt(ql, kls[i]) + _dot(qr, krt_buf[slot, i].astype(BF16)) for i in range(lo, hi)], axis=1) * SM_SCALE

    half = npg // 2
    s_lo, s_hi = scores(0, half), scores(half, npg)
    state = update((m_scr[...], l_scr[...], acc_scr[...]), s_lo, kls[:half])
    state = update(state, s_hi, kls[half:])
    m_scr[...], l_scr[...], acc_scr[...] = state

    @pl.when(c == nchunk - 1)
    def _():
        kn = cn_ref[...].astype(BF16)
        s2 = (_dot_nt(ql, kn) + _dot_nt(qr, krn_ref[...].astype(BF16))) * SM_SCALE
        rows = s2.shape[0]
        t_row = lax.broadcasted_iota(jnp.int32, (rows, 1), 0) // heads
        t_col = lax.broadcasted_iota(jnp.int32, (1, s2.shape[1]), 1)
        ok = (t_col <= t_row) & (t_col < new_len)
        _, l_fin, acc_fin = update(state, jnp.where(ok, s2, -jnp.inf), [kn])
        o_ref[...] = acc_fin / l_fin


def _paged(page_table, q_lat, q_rope, ckv_new, kr_new, cache_lat, cache_kr_t, *, npg, new_len):
    db, rows, lat = q_lat.shape
    n_pages = page_table.shape[1]
    nchunk = n_pages // npg
    npad = ckv_new.shape[1]
    assert n_pages == nchunk * npg
    pt_flat = page_table.reshape(-1)
    in_specs = [
        pl.BlockSpec((None, rows, lat), lambda b, c, pt: (b, 0, 0)),
        pl.BlockSpec((None, rows, QK_ROPE_DIM), lambda b, c, pt: (b, 0, 0)),
        pl.BlockSpec((None, npad, lat), lambda b, c, pt: (b, 0, 0)),
        pl.BlockSpec((None, npad, QK_ROPE_DIM), lambda b, c, pt: (b, 0, 0)),
        pl.BlockSpec(memory_space=pl.ANY),
        pl.BlockSpec(memory_space=pl.ANY),
    ]
    need = (2 * npg * PAGE_SIZE * (lat + QK_ROPE_DIM) * 4 + npg * PAGE_SIZE * lat * 2
            + 6 * rows * npg * PAGE_SIZE * 4 + 8 * rows * lat * 4)
    grid_spec = pltpu.PrefetchScalarGridSpec(
        num_scalar_prefetch=1,
        grid=(db, nchunk),
        in_specs=in_specs,
        out_specs=pl.BlockSpec((None, rows, lat), lambda b, c, pt: (b, 0, 0)),
        scratch_shapes=[
            pltpu.VMEM((2, npg, PAGE_SIZE, lat), F32),
            pltpu.VMEM((2, npg, QK_ROPE_DIM, PAGE_SIZE), F32),
            pltpu.SemaphoreType.DMA((2, 2)),
            pltpu.VMEM((rows, 1), F32),
            pltpu.VMEM((rows, 1), F32),
            pltpu.VMEM((rows, lat), F32),
        ],
    )
    return pl.pallas_call(
        functools.partial(_paged_kernel, npg=npg, nchunk=nchunk, new_len=new_len, heads=MLA_HEADS),
        grid_spec=grid_spec,
        out_shape=jax.ShapeDtypeStruct((db, rows, lat), F32),
        compiler_params=_params(("arbitrary", "arbitrary"), need),
        name="paged",
    )(pt_flat, q_lat, q_rope, ckv_new, kr_new, cache_lat, cache_kr_t)


def _prep_kernel(u_ref, tail_ref, s0_ref, mu_ref, w0_ref, a0_ref, kk_ref, ka_ref, wd_ref, wi_ref, wg_ref,
                 r_ref, k_ref, v_ref, kkr_ref, a_ref, lw_ref, g_ref, *, seq_len):
    tm = u_ref.shape[0]
    u = u_ref[...]
    row = lax.broadcasted_iota(jnp.int32, (tm, 1), 0)
    prev = jnp.where(row == 0, tail_ref[SUBLANES - 1:SUBLANES, :], pltpu.roll(u, 1, 0))
    starts = ((pl.program_id(0) * tm + row) & (seq_len - 1)) == 0
    prev = jnp.where(starts, s0_ref[...], prev)
    um = u + (prev - u) * mu_ref[...]
    n = RWKV_DIM
    k = um[:, n:2 * n]
    dw = um[:, 3 * n:3 * n + DECAY_LORA]
    da = um[:, 3 * n + DECAY_LORA:3 * n + DECAY_LORA + ICLR_LORA]
    dg = um[:, 3 * n + DECAY_LORA + ICLR_LORA:]
    z = -(w0_ref[...] + _dot(jnp.tanh(dw).astype(BF16), wd_ref[...]))
    softplus = jnp.maximum(z, 0.0) + jnp.log1p(jnp.exp(-jnp.abs(z)))
    lw_ref[...] = -jnp.exp(-softplus - 0.5)
    a = jax.nn.sigmoid(a0_ref[...] + _dot(da.astype(BF16), wi_ref[...]))
    a_ref[...] = a
    g_ref[...] = _dot(jax.nn.sigmoid(dg).astype(BF16), wg_ref[...])
    r_ref[...] = um[:, :n]
    v_ref[...] = um[:, 2 * n:3 * n]
    kkr_ref[...] = k * kk_ref[...]
    k_ref[...] = k * (1.0 + (a - 1.0) * ka_ref[...])


def _rwkv_prep(u, shift0, mu, w0, a0, k_k, k_a, w_decay, w_iclr, w_gate, *, tm, seq_len):
    r, wu = u.shape
    n = RWKV_DIM
    assert seq_len & (seq_len - 1) == 0
    tiles_per_group = max(seq_len // tm, 1)
    mr = shift0.shape[1]
    row = lambda width: pl.BlockSpec((1, width), lambda i: (0, 0))
    full = lambda arr: pl.BlockSpec(arr.shape, lambda i: (0, 0))
    out_spec = pl.BlockSpec((tm, n), lambda i: (i, 0))
    need = 2 * tm * wu * 4 + 2 * mr * wu * 4 + 2 * 7 * tm * n * 4 + 6 * tm * wu * 4
    return pl.pallas_call(
        functools.partial(_prep_kernel, seq_len=seq_len),
        grid=(r // tm,),
        in_specs=[
            pl.BlockSpec((tm, wu), lambda i: (i, 0)),
            pl.BlockSpec((SUBLANES, wu), lambda i: (jnp.maximum(i * (tm // SUBLANES) - 1, 0), 0)),
            pl.BlockSpec((None, mr, wu), lambda i: (i // tiles_per_group, 0, 0)),
            row(wu), row(n), row(n), row(n), row(n),
            full(w_decay), full(w_iclr), full(w_gate),
        ],
        out_specs=[out_spec] * 7,
        out_shape=[jax.ShapeDtypeStruct((r, n), F32)] * 7,
        compiler_params=_params(("parallel",), need),
        name="rwkv_prep",
    )(u, u, shift0, mu.reshape(1, wu), w0.reshape(1, n), a0.reshape(1, n), k_k.reshape(1, n), k_a.reshape(1, n),
      w_decay, w_iclr, w_gate)


def _wkv_kernel(*refs, c_len, hg, gps, n_chunks, has_s0):
    if has_s0:
        (r_ref, k_ref, v_ref, kkr_ref, a_ref, lw_ref, g_ref, rk_ref, lng_ref, lnb_ref, s0_ref,
         o_ref, sout_ref, st_scr) = refs
    else:
        (r_ref, k_ref, v_ref, kkr_ref, a_ref, lw_ref, g_ref, rk_ref, lng_ref, lnb_ref,
         o_ref, sout_ref, st_scr) = refs
    n = RWKV_HEAD_DIM
    w = hg * n
    rw = hg * c_len
    shift = int(math.log2(n))
    ci = pl.program_id(2)

    lane_head = lax.broadcasted_iota(jnp.int32, (1, w), 1) >> shift
    key_head = lax.broadcasted_iota(jnp.int32, (w, 1), 0) >> shift
    same_head = key_head == lane_head
    row_head = lax.broadcasted_iota(jnp.int32, (rw, 1), 0) // c_len
    stack_mask = row_head == lane_head

    @pl.when(ci == 0)
    def _():
        for gi in range(gps):
            if has_s0:
                s0t = jnp.transpose(s0_ref[gi * hg:(gi + 1) * hg].reshape(w, n))
                st_scr[gi] = jnp.where(same_head, jnp.concatenate([s0t] * hg, axis=0), 0.0)
            else:
                st_scr[gi] = jnp.zeros((w, w), F32)

    new_states = _wkv_groups(gps, r_ref, k_ref, v_ref, kkr_ref, a_ref, lw_ref, g_ref, rk_ref, lng_ref, lnb_ref,
                             o_ref, st_scr, c_len=c_len, hg=hg, same_head=same_head, stack_mask=stack_mask)

    @pl.when(ci == n_chunks - 1)
    def _():
        for gi, st_new in enumerate(new_states):
            z = st_new[0:n]
            for e in range(1, hg):
                z = z + st_new[e * n:(e + 1) * n]
            sout_ref[gi * hg:(gi + 1) * hg] = jnp.transpose(z).reshape(hg, n, n)


def _wkv_groups(gps, r_ref, k_ref, v_ref, kkr_ref, a_ref, lw_ref, g_ref, rk_ref, lng_ref, lnb_ref, o_ref, st_scr, *,
                c_len, hg, same_head, stack_mask):
    n = RWKV_HEAD_DIM
    w = hg * n
    rw = hg * c_len
    groups = range(gps)
    lanes = [slice(gi * w, (gi + 1) * w) for gi in groups]
    each = lambda fn, *cols: [fn(*xs) for xs in zip(*cols)]

    seg_ones = jnp.where(same_head, 1.0, 0.0).astype(BF16)

    def seg_sum(x):
        hi, lo = _split_bf16(x, 2)
        return _dot(hi, seg_ones) + _dot(lo, seg_ones)

    def stack(x):
        return jnp.where(stack_mask, jnp.concatenate([x] * hg, axis=0), 0.0).astype(BF16)

    t_r = lax.broadcasted_iota(jnp.int32, (c_len, 1), 0)
    t_c = lax.broadcasted_iota(jnp.int32, (1, c_len), 1)
    tril = jnp.where(t_c <= t_r, 1.0, 0.0).astype(BF16)
    ri = lax.broadcasted_iota(jnp.int32, (rw, 1), 0)
    cj = lax.broadcasted_iota(jnp.int32, (1, rw), 1)
    strict = cj < ri
    incl = cj <= ri

    r = [r_ref[:, ls] for ls in lanes]
    k = [k_ref[:, ls] for ls in lanes]
    v = [v_ref[:, ls] for ls in lanes]
    a = [a_ref[:, ls] for ls in lanes]
    lw = [lw_ref[:, ls] for ls in lanes]
    kkr = [kkr_ref[:, ls] for ls in lanes]

    def cumsum(x):
        l1, l2, l3 = _split_bf16(x, 3)
        return _dot(tril, l1) + _dot(tril, l2) + _dot(tril, l3)

    cum = each(cumsum, lw)
    cum_end = [c[c_len - 1:c_len, :] for c in cum]
    kkn = each(lambda x: x / jnp.maximum(jnp.sqrt(seg_sum(x * x)), 1e-12), kkr)
    kka = each(lambda x, y: x * y, kkn, a)
    p_inv = each(lambda c: jnp.exp(-c), cum)
    p_tail = each(lambda ce, c: jnp.exp(ce - c), cum_end, cum)

    a_s = each(lambda c, l, x: stack(jnp.exp(c - l) * (-x)), cum, lw, kkn)
    r_s = each(lambda c, x: stack(jnp.exp(c) * x), cum, r)
    b_s = each(lambda p, x: stack(p * x), p_inv, kka)
    k_s = each(lambda p, x: stack(p * x), p_inv, k)
    v_s = each(stack, v)
    be_s = each(lambda p, x: stack(p * x), p_tail, kka)
    ke_s = each(lambda p, x: stack(p * x), p_tail, k)

    l_ab = each(lambda x, y: jnp.where(strict, _dot_nt(x, y), 0.0).astype(BF16), a_s, b_s)
    l_ak = each(lambda x, y: jnp.where(strict, _dot_nt(x, y), 0.0).astype(BF16), a_s, k_s)
    m_rb = each(lambda x, y: jnp.where(incl, _dot_nt(x, y), 0.0).astype(BF16), r_s, b_s)
    m_rk = each(lambda x, y: jnp.where(incl, _dot_nt(x, y), 0.0).astype(BF16), r_s, k_s)

    st = [st_scr[gi] for gi in groups]
    st_b = [s.astype(BF16) for s in st]
    x = each(lambda p, s, l, q: _dot(p, s) + _dot(l, q), a_s, st_b, l_ak, v_s)
    n_it = int(math.log2(c_len))
    li = l_ab
    for it in range(n_it):
        x = each(lambda xx, l: xx + _dot(l, xx.astype(BF16)), x, li)
        if it < n_it - 1:
            li = each(lambda l: _dot(l, l).astype(BF16), li)
    u_s = [xx.astype(BF16) for xx in x]

    def fold_heads(y_s):
        y = y_s[0:c_len]
        for e in range(1, hg):
            y = y + y_s[e * c_len:(e + 1) * c_len]
        return y

    y = each(lambda p, s, m1, u, m2, q: fold_heads(_dot(p, s) + _dot(m1, u) + _dot(m2, q)),
             r_s, st_b, m_rb, u_s, m_rk, v_s)

    def new_state(ce, s, b, u, kk_, q):
        pc_col = jnp.transpose(jnp.broadcast_to(jnp.exp(ce), (V7X_LANES, w)))[:, 0:1]
        return pc_col * s + _dot_tn(b, u) + _dot_tn(kk_, q)

    st_new = each(new_state, cum_end, st, be_s, u_s, ke_s, v_s)
    for gi in groups:
        st_scr[gi] = st_new[gi]

    inv_n = 1.0 / n
    mean = each(lambda yy: seg_sum(yy) * inv_n, y)
    d = each(lambda yy, m: yy - m, y, mean)
    var = each(lambda dd: seg_sum(dd * dd) * inv_n, d)
    bonus = each(lambda rr, kk_, ls, vv: seg_sum(rr * kk_ * rk_ref[:, ls]) * vv, r, k, lanes, v)
    for gi in groups:
        ls = lanes[gi]
        yn = d[gi] * lax.rsqrt(var[gi] + GN_EPS) * lng_ref[:, ls] + lnb_ref[:, ls]
        o_ref[:, ls] = (yn + bonus[gi]) * g_ref[:, ls]
    return st_new


def _wkv(r, k, v, kkr, a, lw, g, r_k, lnx_g, lnx_b, s0, *, batch, c_len, n_chunks, gps):
    rows, dim = r.shape
    hg = WKV_HG
    n = RWKV_HEAD_DIM
    w = hg * n
    wb = gps * w
    blk = pl.BlockSpec((c_len, wb), lambda b, gi, ci: (b * n_chunks + ci, gi))
    vec = pl.BlockSpec((1, wb), lambda b, gi, ci: (0, gi))
    st_spec = pl.BlockSpec((None, gps * hg, n, n), lambda b, gi, ci: (b, gi, 0, 0))
    in_specs = [blk] * 7 + [vec] * 3
    args = [r, k, v, kkr, a, lw, g, r_k.reshape(1, dim), lnx_g.reshape(1, dim), lnx_b.reshape(1, dim)]
    if s0 is not None:
        in_specs.append(st_spec)
        args.append(s0)
    rw = hg * c_len
    need = 2 * 8 * c_len * wb * 4 + gps * (5 * w * w * 4 + 16 * rw * w * 4 + 8 * rw * rw * 4)
    return pl.pallas_call(
        functools.partial(_wkv_kernel, c_len=c_len, hg=hg, gps=gps, n_chunks=n_chunks, has_s0=s0 is not None),
        grid=(batch, dim // wb, n_chunks),
        in_specs=in_specs,
        out_specs=[blk, st_spec],
        out_shape=[
            jax.ShapeDtypeStruct((rows, dim), F32),
            jax.ShapeDtypeStruct((batch, dim // n, n, n), F32),
        ],
        scratch_shapes=[pltpu.VMEM((gps, w, w), F32)],
        compiler_params=_params(("parallel", "parallel", "arbitrary"), need),
        name="wkv",
    )(*args)


def _wkv_lanes_kernel(r_ref, k_ref, v_ref, kkr_ref, a_ref, lw_ref, g_ref, rk_ref, lng_ref, lnb_ref, s0_ref,
                      o_ref, sout_ref, y_scr, *, steps):
    n = RWKV_HEAD_DIM
    decay, kkn, kka, kt, rt = [], [], [], [], []
    for t in range(steps):
        kkr = kkr_ref[t]
        norm = jnp.sqrt(jnp.sum(kkr * kkr, axis=0, keepdims=True))
        kk = kkr / jnp.maximum(norm, 1e-12)
        decay.append(jnp.exp(lw_ref[t]))
        kkn.append(kk)
        kka.append(kk * a_ref[t])
        kt.append(k_ref[t])
        rt.append(r_ref[t])

    def row(vi, carry):
        s = s0_ref[vi]
        for t in range(steps):
            sa = jnp.sum(s * kkn[t], axis=0, keepdims=True)
            s = s * decay[t] - sa * kka[t] + v_ref[t, pl.ds(vi, 1), :] * kt[t]
            y_scr[t, pl.ds(vi, 1), :] = jnp.sum(s * rt[t], axis=0, keepdims=True)
        sout_ref[vi] = s
        return carry

    lax.fori_loop(0, n, row, 0)

    for t in range(steps):
        y = y_scr[t]
        mean = jnp.mean(y, axis=0, keepdims=True)
        d = y - mean
        var = jnp.mean(d * d, axis=0, keepdims=True)
        yn = d * lax.rsqrt(var + GN_EPS) * lng_ref[...] + lnb_ref[...]
        bonus = jnp.sum(rt[t] * kt[t] * rk_ref[...], axis=0, keepdims=True) * v_ref[t]
        o_ref[t] = (yn + bonus) * g_ref[t]


def _wkv_lanes(r, k, v, kkr, a, lw, g, r_k, lnx_g, lnx_b, s0):
    steps, dim, nb = r.shape
    n = RWKV_HEAD_DIM
    heads = dim // n
    blk = pl.BlockSpec((steps, n, nb), lambda h: (0, h, 0))
    par = pl.BlockSpec((n, nb), lambda h: (h, 0))
    st = pl.BlockSpec((None, n, n, nb), lambda h: (h, 0, 0, 0))
    need = 2 * 8 * steps * n * nb * 4 + 4 * n * n * nb * 4 + 32 * n * nb * 4
    return pl.pallas_call(
        functools.partial(_wkv_lanes_kernel, steps=steps),
        grid=(heads,),
        in_specs=[blk] * 7 + [par] * 3 + [st],
        out_specs=[blk, st],
        out_shape=[jax.ShapeDtypeStruct((steps, dim, nb), F32), jax.ShapeDtypeStruct((heads, n, n, nb), F32)],
        scratch_shapes=[pltpu.VMEM((steps, n, nb), F32)],
        compiler_params=_params(("parallel",), need),
        name="wkv_lanes",
    )(r, k, v, kkr, a, lw, g, r_k, lnx_g, lnx_b, s0)


def _rope_tables(pos):
    inv = ROPE_THETA ** (-jnp.arange(0, QK_ROPE_DIM, 2, dtype=F32) / QK_ROPE_DIM)
    ang = pos[:, None] * inv[None, :]
    cos, sin = jnp.cos(ang), jnp.sin(ang)
    return jnp.concatenate([cos] * 4, axis=1), jnp.concatenate([-sin, sin] * 2, axis=1)


def _swap_halves(w):
    half = w.shape[-1] // 2
    return jnp.concatenate([w[..., half:], w[..., :half]], axis=-1)


def _prepare_weights(w_in, w_uq, w_ukv, w_o, w_decay, w_iclr, w_gate):
    d = w_in.shape[0]
    rope = w_in[:, Q_LORA_RANK + KV_LORA_RANK:W_MLA_IN]
    rope_sw = _swap_halves(rope)
    zeros = lambda c: jnp.zeros((d, c), w_in.dtype)
    w_mla = jnp.concatenate(
        [w_in[:, :Q_LORA_RANK], zeros(MLA_KV_OFF - Q_LORA_RANK), w_in[:, Q_LORA_RANK:Q_LORA_RANK + KV_LORA_RANK],
         rope, rope, rope_sw, rope_sw, zeros(MLA_COLS - MLA_KV_OFF - KV_LORA_RANK - 4 * QK_ROPE_DIM)], axis=1).astype(BF16)
    w_rwkv = w_in[:, W_MLA_IN:].astype(BF16)
    uq = w_uq.reshape(Q_LORA_RANK, MLA_HEADS, QK_NOPE_DIM + QK_ROPE_DIM)
    uq_rope = uq[..., QK_NOPE_DIM:]
    w_q = jnp.concatenate(
        [uq[..., :QK_NOPE_DIM].reshape(Q_LORA_RANK, -1), uq_rope.reshape(Q_LORA_RANK, -1),
         _swap_halves(uq_rope).reshape(Q_LORA_RANK, -1)], axis=1).astype(BF16)
    ukv = w_ukv.reshape(KV_LORA_RANK, MLA_HEADS, QK_NOPE_DIM + V_HEAD_DIM)
    w_uk, w_uv = ukv[..., :QK_NOPE_DIM], ukv[..., QK_NOPE_DIM:]
    w_kvup = jnp.concatenate([w_uk.reshape(KV_LORA_RANK, -1), w_uv.reshape(KV_LORA_RANK, -1)], axis=1).astype(BF16)
    w_uk_t = jnp.transpose(w_uk, (1, 2, 0)).astype(BF16)
    w_uv_h = jnp.transpose(w_uv, (1, 0, 2)).astype(BF16)
    return dict(w_mla=_tile_major(w_mla, MM_TN), w_rwkv=_tile_major(w_rwkv, MM_TN), w_q=_tile_major(w_q, MM_TN),
                w_kvup=_tile_major(w_kvup, MM_TN), w_uk_t=w_uk_t, w_uv_h=w_uv_h,
                w_o=w_o.astype(BF16), w_decay=w_decay.astype(BF16), w_iclr=w_iclr.astype(BF16),
                w_gate=w_gate.astype(BF16))


def _group_layer(x, mods, tiles_per_group, tm, wts, p, *, cos4, sin4, ffn, attend, wkv_run, shift0, seq_len):
    sh2, sc2, g2 = mods
    x1 = ffn(0, x)
    mla = _mm(x1, wts["w_mla"], tm=tm, mod=(sh2, sc2), tiles_per_group=tiles_per_group, name="proj_mla")
    u = _mm(x1, wts["w_rwkv"], tm=tm, mod=(sh2, sc2), tiles_per_group=tiles_per_group, name="proj_rwkv")
    ckv, kr, kr2 = _lat_prep(mla, p["g_kv"], cos4, sin4, tm=tm)
    qall = _mm(mla, wts["w_q"], tm=tm, k=Q_LORA_RANK, rms_g=p["g_q"], name="q_proj")
    q_rope = _rope_q(qall, cos4, sin4, tm=tm)
    attn = attend(qall, q_rope, ckv, kr, kr2)
    prep = _rwkv_prep(u, shift0, p["mu_shift"], p["w0"], p["a0"], p["k_k"], p["k_a"],
                      wts["w_decay"], wts["w_iclr"], wts["w_gate"], tm=PREP_TM, seq_len=seq_len)
    rwkv, s_new = wkv_run(prep)
    x2 = _oproj(x1, g2, attn, rwkv, wts["w_o"], p["ln_g"][1], p["ln_b"][1], tm=tm, tk=512,
                tiles_per_group=tiles_per_group)
    return ffn(1, x2), ckv, kr, s_new, u


def kernel(x_prompt, x_sample, c_prompt, c_sample, cache_kv_latent, cache_k_rope, state_wkv, state_shift, page_table, w_ada, b_ada, ln_g, ln_b, w_ffn1_in, w_ffn1_out, w_ffn2_in, w_ffn2_out, w_in, g_q, g_kv, w_uq, w_ukv, mu_shift, w0, w_decay, a0, w_iclr, w_gate, k_k, k_a, r_k, lnx_g, lnx_b, w_o):
    batch, seq, d = x_prompt.shape
    db, dseq, _ = x_sample.shape
    depth = w_ada.shape[0]
    assert depth == DEPTH == 1
    n_ada = w_ada.shape[2] // d
    past = page_table.shape[1] * PAGE_SIZE
    wu = state_shift.shape[2]

    c_all = jnp.concatenate([c_prompt, c_sample], axis=0)
    cos_p, sin_p = _rope_tables(jnp.arange(seq, dtype=F32) + 0)
    pos_s = jnp.tile(jnp.arange(dseq, dtype=F32) + past, db)
    cos_s, sin_s = _rope_tables(pos_s)

    yp = x_prompt.reshape(batch * seq, d)
    ys = x_sample.reshape(db * dseq, d)
    outs_p, outs_s = [], []
    for l in range(depth):
        p = dict(ln_g=ln_g[l], ln_b=ln_b[l], g_q=g_q[l], g_kv=g_kv[l], mu_shift=mu_shift[l], w0=w0[l], a0=a0[l],
                 k_k=k_k[l], k_a=k_a[l], r_k=r_k[l], lnx_g=lnx_g[l], lnx_b=lnx_b[l])
        wts = _prepare_weights(w_in[l], w_uq[l], w_ukv[l], w_o[l], w_decay[l], w_iclr[l], w_gate[l])
        ffn_w32 = ((w_ffn1_in[l], w_ffn1_out[l]), (w_ffn2_in[l], w_ffn2_out[l]))
        ffn_w16 = [None, None]
        ada = _ada(c_all, w_ada[l], b_ada[l]).reshape(batch + db, n_ada, d)
        mods_p = [ada[:batch, i][:, None, :] for i in range(n_ada)]
        mods_s = [ada[batch:, i][None] for i in range(n_ada)]
        per_token = lambda m: jnp.repeat(m[0], dseq, axis=0)[None]

        tm_s = db * dseq
        c_pad = 16
        t_major = lambda a: a.reshape(db, dseq, d).transpose(1, 0, 2).reshape(tm_s, d)
        b_major = lambda a: a.reshape(dseq, db, d).transpose(1, 0, 2).reshape(tm_s, d)

        def ffn_s(i, x):
            sh, sc, g = mods_s[6 * i:6 * i + 3]
            y, w_in16, w_out16 = _ffn(t_major(x), sh, sc, g, *ffn_w32[i], p["ln_g"][2 * i], p["ln_b"][2 * i],
                                      tm=tm_s, tf=128, tiles_per_group=1, emit_bf16_weights=True)
            ffn_w16[i] = (w_in16, w_out16)
            return b_major(y)

        def attend_s(qall, q_rope, ckv, kr, kr2):
            rows = dseq * MLA_HEADS
            q_lat = _bmm(qall, wts["w_uk_t"], tm=tm_s, out_dtype=BF16, name="q_absorb")
            pad = lambda t: jnp.pad(t.reshape(db, dseq, -1), ((0, 0), (0, c_pad - dseq), (0, 0)))
            o_lat = _paged(page_table, q_lat.reshape(db, rows, KV_LORA_RANK), q_rope.reshape(db, rows, QK_ROPE_DIM),
                           pad(ckv), pad(kr), cache_kv_latent, jnp.swapaxes(cache_k_rope, 2, 3),
                           npg=PAGES_PER_STEP, new_len=dseq)
            return _bmm(o_lat.reshape(db * dseq, MLA_HEADS * KV_LORA_RANK), wts["w_uv_h"], tm=tm_s, name="v_up")

        def wkv_s(prep):
            lanes = lambda a: a.reshape(db, dseq, -1).transpose(1, 2, 0)
            over_batch = lambda v: jnp.broadcast_to(v.reshape(-1, 1), (v.size, db))
            out, s_new = _wkv_lanes(*[lanes(t) for t in prep], over_batch(p["r_k"]), over_batch(p["lnx_g"]),
                                    over_batch(p["lnx_b"]), jnp.transpose(state_wkv[l], (1, 2, 3, 0)))
            return out.transpose(2, 0, 1).reshape(db * dseq, -1), jnp.transpose(s_new, (3, 0, 1, 2))

        shift_rows = jnp.repeat(state_shift[l], dseq, axis=0).reshape(db * dseq // PREP_TM, PREP_TM, wu)
        ys, ckv_s, kr_s, s_s, u_s = _group_layer(
            ys, [per_token(m) for m in mods_s[3:6]], 1, tm_s, wts, p, cos4=cos_s, sin4=sin_s, ffn=ffn_s,
            attend=attend_s, wkv_run=wkv_s, shift0=shift_rows, seq_len=dseq)
        outs_s.append((ckv_s.reshape(db, dseq, -1), kr_s.reshape(db, dseq, -1), s_s,
                       u_s.reshape(db, dseq, wu)[:, -1]))

        tm_p = 512

        def ffn_p(i, x):
            sh, sc, g = mods_p[6 * i:6 * i + 3]
            return _ffn(x, sh, sc, g, *ffn_w16[i], p["ln_g"][2 * i], p["ln_b"][2 * i],
                        tm=tm_p, tf=FFN_TF, tiles_per_group=seq // tm_p)

        def attend_p(qall, q_rope, ckv, kr, kr2):
            kvup = _mm(ckv, wts["w_kvup"], tm=tm_p, name="kv_up")
            return _flash(qall, q_rope, kvup, kr2, batch=batch, seq=seq, tq=512)

        def wkv_p(prep):
            return _wkv(*prep, p["r_k"], p["lnx_g"], p["lnx_b"], None, batch=batch, c_len=64, n_chunks=seq // 64,
                        gps=8)

        yp, ckv_p, kr_p, s_p, u_p = _group_layer(
            yp, mods_p[3:6], seq // tm_p, tm_p, wts, p, cos4=cos_p, sin4=sin_p, ffn=ffn_p, attend=attend_p,
            wkv_run=wkv_p, shift0=jnp.zeros((batch, 1, wu), F32), seq_len=seq)
        outs_p.append((ckv_p.reshape(batch, seq, -1), kr_p.reshape(batch, seq, -1), s_p,
                       u_p.reshape(batch, seq, wu)[:, -1]))

    stack = lambda outs, i: jnp.stack([o[i] for o in outs])
    return (yp.reshape(batch, seq, d), ys.reshape(db, dseq, d),
            stack(outs_p, 0), stack(outs_p, 1), stack(outs_p, 2), stack(outs_p, 3),
            stack(outs_s, 0), stack(outs_s, 1), stack(outs_s, 2), stack(outs_s, 3))
```

```python
import functools
import math

import jax
import jax.numpy as jnp
from jax import lax
from jax.experimental import pallas as pl
from jax.experimental.pallas import tpu as pltpu

F32 = jnp.float32
BF16 = jnp.bfloat16

QK_NOPE_DIM = 128
QK_ROPE_DIM = 64
V_HEAD_DIM = 128
MLA_HEADS = 16
Q_LORA_RANK = 896
KV_LORA_RANK = 512
RWKV_HEAD_DIM = 64
RWKV_HEADS = 32
RWKV_DIM = RWKV_HEADS * RWKV_HEAD_DIM
DECAY_LORA = 128
ICLR_LORA = 128
GATE_LORA = 256
W_MLA_IN = Q_LORA_RANK + KV_LORA_RANK + QK_ROPE_DIM
PAGE_SIZE = 128
ROPE_THETA = 10000.0
SM_SCALE = (QK_NOPE_DIM + QK_ROPE_DIM) ** -0.5
DEPTH = 1
ALPHA = (2 * DEPTH) ** 0.25
LN_EPS = 1e-5
RMS_EPS = 1e-6
GN_EPS = 64e-5

V7X_VMEM_BYTES = 64 * 1024 * 1024
V7X_LANES = 128
SUBLANES = 8
MIB = 1024 * 1024

MLA_COLS = 2048
MLA_KV_OFF = 1024

ROW_CHUNK = 64
N_CHUNK = 512
MM_TN = 512
FFN_TF = 256
PREP_TM = 128
PAGES_PER_STEP = 32
WKV_HG = 4


def _params(sem, vmem_bytes):
    limit = min(int(vmem_bytes) + 8 * MIB, V7X_VMEM_BYTES - 6 * MIB)
    return pltpu.CompilerParams(dimension_semantics=sem, vmem_limit_bytes=limit)


def _dot(a, b):
    return jnp.dot(a, b, preferred_element_type=F32)


def _dot_nt(a, b):
    return lax.dot_general(a, b, (((1,), (1,)), ((), ())), preferred_element_type=F32)


def _dot_tn(a, b):
    return lax.dot_general(a, b, (((0,), (0,)), ((), ())), preferred_element_type=F32)


def _split_bf16(x, parts):
    out = []
    rem = x
    for _ in range(parts):
        h = rem.astype(BF16)
        out.append(h)
        rem = rem - h.astype(F32)
    return out


def _layernorm_rows(z, g, b):
    mu = jnp.mean(z, axis=-1, keepdims=True)
    d = z - mu
    var = jnp.mean(d * d, axis=-1, keepdims=True)
    return d * lax.rsqrt(var + LN_EPS) * g + b


def _for_row_chunks(n_rows, fn, rows=ROW_CHUNK):
    def body(i, carry):
        fn(pl.ds(pl.multiple_of(i * rows, rows), rows))
        return carry

    lax.fori_loop(0, n_rows // rows, body, 0)


def _rows_of(ref, rs):
    mr = ref.shape[0]
    if mr == 1:
        return ref[...]
    return ref[pl.ds(pl.multiple_of(rs.start % mr, rs.size), rs.size), :]


def _modulate_into(h_scr, x_ref, sh_ref, sc_ref):
    def chunk(rs):
        h_scr[rs, :] = (x_ref[rs, :] * (1.0 + _rows_of(sc_ref, rs)) + _rows_of(sh_ref, rs)).astype(BF16)

    _for_row_chunks(h_scr.shape[0], chunk)


def _residual_layernorm(o_ref, x_ref, gate_ref, lng_ref, lnb_ref, gate_scale):
    def chunk(rs):
        z = ALPHA * x_ref[rs, :] + gate_scale * _rows_of(gate_ref, rs) * o_ref[rs, :]
        o_ref[rs, :] = _layernorm_rows(z, lng_ref[...], lnb_ref[...])

    _for_row_chunks(o_ref.shape[0], chunk)


def _ada_kernel(cp_ref, cs_ref, w_ref, b_ref, op_ref, os_ref):
    w = w_ref[...].astype(BF16)
    for c_ref, o_ref in ((cp_ref, op_ref), (cs_ref, os_ref)):
        c = c_ref[...]
        o_ref[...] = _dot((c * jax.nn.sigmoid(c)).astype(BF16), w) + b_ref[...]


def _ada(c_prompt, c_sample, w_ada, b_ada, tn=512):
    mp, k = c_prompt.shape
    ms = c_sample.shape[0]
    n = w_ada.shape[1]
    n_ada = n // k
    per = k // tn
    need = 2 * (k * tn * 4) + k * tn * 2 + 2 * (mp + ms) * k * 4 + 4 * (mp + ms) * tn * 4
    return pl.pallas_call(
        _ada_kernel,
        grid=(n_ada, per),
        in_specs=[
            pl.BlockSpec((mp, k), lambda i, j: (0, 0)),
            pl.BlockSpec((ms, k), lambda i, j: (0, 0)),
            pl.BlockSpec((k, tn), lambda i, j: (0, i * per + j)),
            pl.BlockSpec((1, tn), lambda i, j: (0, i * per + j)),
        ],
        out_specs=[pl.BlockSpec((None, mp, tn), lambda i, j: (i, 0, j)),
                   pl.BlockSpec((None, ms, tn), lambda i, j: (i, 0, j))],
        out_shape=[jax.ShapeDtypeStruct((n_ada, mp, k), F32), jax.ShapeDtypeStruct((n_ada, ms, k), F32)],
        compiler_params=_params(("parallel", "parallel"), need),
        name="ada",
    )(c_prompt, c_sample, w_ada, b_ada.reshape(1, n))


def _ffn_kernel(x_ref, sh_ref, sc_ref, g_ref, wg_ref, wu_ref, wo_ref, lng_ref, lnb_ref, o_ref, *rest, nj):
    h_scr = rest[-1]
    j = pl.program_id(1)

    @pl.when(j == 0)
    def _():
        _modulate_into(h_scr, x_ref, sh_ref, sc_ref)
        o_ref[...] = jnp.zeros_like(o_ref)

    wg, wu, wo = wg_ref[...], wu_ref[...], wo_ref[...]
    if len(rest) > 1:
        wg, wu, wo = wg.astype(BF16), wu.astype(BF16), wo.astype(BF16)
        rest[0][...] = wg
        rest[1][...] = wu
        rest[2][...] = wo
    h = h_scr[...]
    gate = _dot(h, wg)
    up = _dot(h, wu)
    act = (gate * jax.nn.sigmoid(gate) * up).astype(BF16)
    for n0 in range(0, o_ref.shape[1], N_CHUNK):
        o_ref[:, n0:n0 + N_CHUNK] += _dot(act, wo[:, n0:n0 + N_CHUNK])

    @pl.when(j == nj - 1)
    def _():
        _residual_layernorm(o_ref, x_ref, g_ref, lng_ref, lnb_ref, 0.5)


def _ffn(x, sh, sc, g, w_in, w_out, ln_g, ln_b, *, tm, tf, tiles_per_group, emit_bf16_weights=False):
    r, d = x.shape
    ff = w_out.shape[0]
    nj = ff // tf
    mr = sh.shape[1]
    if isinstance(w_in, tuple):
        w_gate, w_up = w_in
        assert w_gate.shape[2] == tf
        gate_spec = up_spec = pl.BlockSpec((None, d, tf), lambda i, j: (j, 0, 0))
    else:
        w_gate = w_up = w_in
        gate_spec = pl.BlockSpec((d, tf), lambda i, j: (0, j))
        up_spec = pl.BlockSpec((d, tf), lambda i, j: (0, nj + j))
    wbytes = w_out.dtype.itemsize
    mod_spec = pl.BlockSpec((None, mr, d), lambda i, j: (i // tiles_per_group, 0, 0))
    out_specs = [pl.BlockSpec((tm, d), lambda i, j: (i, 0))]
    out_shape = [jax.ShapeDtypeStruct((r, d), F32)]
    need = 4 * tm * d * 4 + tm * d * 2 + 2 * 3 * d * tf * wbytes + 6 * mr * d * 4 + 6 * tm * tf * 4
    if emit_bf16_weights:
        assert r == tm and w_out.dtype == F32 and FFN_TF % tf == 0
        per_tile = FFN_TF // tf
        emit_spec = pl.BlockSpec((None, d, tf), lambda i, j: (j // per_tile, 0, j % per_tile))
        out_specs += [emit_spec, emit_spec, pl.BlockSpec((tf, d), lambda i, j: (j, 0))]
        out_shape += [jax.ShapeDtypeStruct((ff // FFN_TF, d, FFN_TF), BF16)] * 2 + [jax.ShapeDtypeStruct((ff, d), BF16)]
        need += 3 * 3 * d * tf * 2
    outs = pl.pallas_call(
        functools.partial(_ffn_kernel, nj=nj),
        grid=(r // tm, nj),
        in_specs=[
            pl.BlockSpec((tm, d), lambda i, j: (i, 0)),
            mod_spec, mod_spec, mod_spec,
            gate_spec, up_spec,
            pl.BlockSpec((tf, d), lambda i, j: (j, 0)),
            pl.BlockSpec((1, d), lambda i, j: (0, 0)),
            pl.BlockSpec((1, d), lambda i, j: (0, 0)),
        ],
        out_specs=out_specs,
        out_shape=out_shape,
        scratch_shapes=[pltpu.VMEM((tm, d), BF16)],
        compiler_params=_params(("parallel", "arbitrary"), need),
        name="ffn",
    )(x, sh, sc, g, w_gate, w_up, w_out, ln_g.reshape(1, d), ln_b.reshape(1, d))
    return (outs[0], (outs[1], outs[2]), outs[3]) if emit_bf16_weights else outs[0]


def _mm_mod_kernel(a_ref, sh_ref, sc_ref, w_ref, o_ref, h_scr):
    @pl.when(pl.program_id(1) == 0)
    def _():
        _modulate_into(h_scr, a_ref, sh_ref, sc_ref)

    o_ref[...] = _dot(h_scr[...], w_ref[...]).astype(o_ref.dtype)


def _mm_rms_kernel(a_ref, g_ref, w_ref, o_ref, h_scr):
    @pl.when(pl.program_id(1) == 0)
    def _():
        a = a_ref[...]
        ms = jnp.mean(a * a, axis=-1, keepdims=True)
        h_scr[...] = (a * lax.rsqrt(ms + RMS_EPS) * g_ref[...]).astype(BF16)

    o_ref[...] = _dot(h_scr[...], w_ref[...]).astype(o_ref.dtype)


def _mm_plain_kernel(a_ref, w_ref, o_ref, h_scr):
    @pl.when(pl.program_id(1) == 0)
    def _():
        h_scr[...] = a_ref[...].astype(BF16)

    o_ref[...] = _dot(h_scr[...], w_ref[...]).astype(o_ref.dtype)


def _tile_major(w, tn):
    k, n = w.shape
    return w.reshape(k, n // tn, tn).transpose(1, 0, 2)


def _mm(a, w, *, tm, k=None, a_colblk=0, mod=None, rms_g=None, tiles_per_group=1, out_dtype=F32, name="mm"):
    r = a.shape[0]
    k = a.shape[1] if k is None else k
    tn = w.shape[2]
    n = w.shape[0] * tn
    a_spec = pl.BlockSpec((tm, k), lambda i, j: (i, a_colblk))
    w_spec = pl.BlockSpec((None, k, tn), lambda i, j: (j, 0, 0))
    need = 2 * tm * k * a.dtype.itemsize + tm * k * 2 + 2 * k * tn * 2 + 4 * tm * tn * 4
    if mod is not None:
        sh, sc = mod
        mr = sh.shape[1]
        mod_spec = pl.BlockSpec((None, mr, k), lambda i, j: (i // tiles_per_group, 0, 0))
        kern, ins, args = _mm_mod_kernel, [a_spec, mod_spec, mod_spec, w_spec], (a, sh, sc, w)
        need += 4 * mr * k * 4
    elif rms_g is not None:
        kern, ins, args = _mm_rms_kernel, [a_spec, pl.BlockSpec((1, k), lambda i, j: (0, 0)), w_spec], (a, rms_g.reshape(1, k), w)
    else:
        kern, ins, args = _mm_plain_kernel, [a_spec, w_spec], (a, w)
    return pl.pallas_call(
        kern,
        grid=(r // tm, n // tn),
        in_specs=ins,
        out_specs=pl.BlockSpec((tm, tn), lambda i, j: (i, j)),
        out_shape=jax.ShapeDtypeStruct((r, n), out_dtype),
        scratch_shapes=[pltpu.VMEM((tm, k), BF16)],
        compiler_params=_params(("parallel", "arbitrary"), need),
        name=name,
    )(*args)


def _bmm_kernel(a_ref, w_ref, o_ref):
    o_ref[...] = _dot(a_ref[...].astype(BF16), w_ref[...]).astype(o_ref.dtype)


def _bmm(a, w, *, tm, out_dtype=F32, name="bmm"):
    r = a.shape[0]
    h, ka, nb = w.shape
    need = 2 * tm * ka * 4 + 2 * ka * nb * 2 + 2 * tm * nb * 4
    return pl.pallas_call(
        _bmm_kernel,
        grid=(r // tm, h),
        in_specs=[
            pl.BlockSpec((tm, ka), lambda i, hh: (i, hh)),
            pl.BlockSpec((None, ka, nb), lambda i, hh: (hh, 0, 0)),
        ],
        out_specs=pl.BlockSpec((tm, nb), lambda i, hh: (i, hh)),
        out_shape=jax.ShapeDtypeStruct((r, h * nb), out_dtype),
        compiler_params=_params(("parallel", "parallel"), need),
        name=name,
    )(a, w)


def _oproj_kernel(x_ref, g_ref, a1_ref, a2_ref, w1_ref, w2_ref, lng_ref, lnb_ref, o_ref, *, nk):
    kk = pl.program_id(1)

    @pl.when(kk == 0)
    def _():
        o_ref[...] = jnp.zeros_like(o_ref)

    a1 = a1_ref[...].astype(BF16)
    a2 = a2_ref[...].astype(BF16)
    for n0 in range(0, o_ref.shape[1], N_CHUNK):
        cols = slice(n0, n0 + N_CHUNK)
        o_ref[:, cols] += _dot(a1, w1_ref[:, cols]) + _dot(a2, w2_ref[:, cols])

    @pl.when(kk == nk - 1)
    def _():
        _residual_layernorm(o_ref, x_ref, g_ref, lng_ref, lnb_ref, 1.0)


def _oproj(x, g, attn, rwkv, w_o, ln_g, ln_b, *, tm, tk, tiles_per_group):
    r, d = x.shape
    half = attn.shape[1]
    nk = half // tk
    mr = g.shape[1]
    need = 4 * tm * d * 4 + 2 * mr * d * 4 + 4 * tm * tk * 4 + 4 * tk * d * 2
    return pl.pallas_call(
        functools.partial(_oproj_kernel, nk=nk),
        grid=(r // tm, nk),
        in_specs=[
            pl.BlockSpec((tm, d), lambda i, kk: (i, 0)),
            pl.BlockSpec((None, mr, d), lambda i, kk: (i // tiles_per_group, 0, 0)),
            pl.BlockSpec((tm, tk), lambda i, kk: (i, kk)),
            pl.BlockSpec((tm, tk), lambda i, kk: (i, kk)),
            pl.BlockSpec((tk, d), lambda i, kk: (kk, 0)),
            pl.BlockSpec((tk, d), lambda i, kk: (nk + kk, 0)),
            pl.BlockSpec((1, d), lambda i, kk: (0, 0)),
            pl.BlockSpec((1, d), lambda i, kk: (0, 0)),
        ],
        out_specs=pl.BlockSpec((tm, d), lambda i, kk: (i, 0)),
        out_shape=jax.ShapeDtypeStruct((r, d), F32),
        compiler_params=_params(("parallel", "arbitrary"), need),
        name="oproj",
    )(x, g, attn, rwkv, w_o, w_o, ln_g.reshape(1, d), ln_b.reshape(1, d))


def _lat_kernel(m_ref, g_ref, cos_ref, sin_ref, ckv_ref, kr_ref, kr2_ref):
    m = m_ref[...]
    kv = m[:, :KV_LORA_RANK]
    ms = jnp.mean(kv * kv, axis=-1, keepdims=True)
    ckv_ref[...] = kv * lax.rsqrt(ms + RMS_EPS) * g_ref[...]
    a2 = m[:, KV_LORA_RANK:KV_LORA_RANK + V7X_LANES]
    b2 = m[:, KV_LORA_RANK + V7X_LANES:KV_LORA_RANK + 2 * V7X_LANES]
    kr2 = a2 * cos_ref[...] + b2 * sin_ref[...]
    kr2_ref[...] = kr2
    kr_ref[...] = kr2[:, :QK_ROPE_DIM]


def _lat_prep(mla, g_kv, cos4, sin4, *, tm):
    r = mla.shape[0]
    nb = cos4.shape[0] // tm
    wblk = MLA_COLS - MLA_KV_OFF
    need = 2 * tm * wblk * 4 + 2 * tm * (KV_LORA_RANK + 3 * V7X_LANES + 2 * V7X_LANES) * 4
    return pl.pallas_call(
        _lat_kernel,
        grid=(r // tm,),
        in_specs=[
            pl.BlockSpec((tm, wblk), lambda i: (i, MLA_KV_OFF // wblk)),
            pl.BlockSpec((1, KV_LORA_RANK), lambda i: (0, 0)),
            pl.BlockSpec((tm, V7X_LANES), lambda i: (i % nb, 0)),
            pl.BlockSpec((tm, V7X_LANES), lambda i: (i % nb, 0)),
        ],
        out_specs=[
            pl.BlockSpec((tm, KV_LORA_RANK), lambda i: (i, 0)),
            pl.BlockSpec((tm, QK_ROPE_DIM), lambda i: (i, 0)),
            pl.BlockSpec((tm, V7X_LANES), lambda i: (i, 0)),
        ],
        out_shape=[
            jax.ShapeDtypeStruct((r, KV_LORA_RANK), F32),
            jax.ShapeDtypeStruct((r, QK_ROPE_DIM), F32),
            jax.ShapeDtypeStruct((r, V7X_LANES), F32),
        ],
        compiler_params=_params(("parallel",), need),
        name="lat_prep",
    )(mla, g_kv.reshape(1, KV_LORA_RANK), cos4, sin4)


def _ropeq_kernel(a_ref, b_ref, cos_ref, sin_ref, o_ref):
    reps = a_ref.shape[1] // V7X_LANES
    c = jnp.concatenate([cos_ref[...]] * reps, axis=1)
    s = jnp.concatenate([sin_ref[...]] * reps, axis=1)
    o_ref[...] = (a_ref[...] * c + b_ref[...] * s).astype(o_ref.dtype)


def _rope_q(qall, cos4, sin4, *, tm):
    r = qall.shape[0]
    w = MLA_HEADS * QK_ROPE_DIM
    nb = cos4.shape[0] // tm
    need = 4 * tm * w * 4 + 2 * tm * w * 2 + 4 * tm * w * 4
    return pl.pallas_call(
        _ropeq_kernel,
        grid=(r // tm,),
        in_specs=[
            pl.BlockSpec((tm, w), lambda i: (i, 2)),
            pl.BlockSpec((tm, w), lambda i: (i, 3)),
            pl.BlockSpec((tm, V7X_LANES), lambda i: (i % nb, 0)),
            pl.BlockSpec((tm, V7X_LANES), lambda i: (i % nb, 0)),
        ],
        out_specs=pl.BlockSpec((tm, w), lambda i: (i, 0)),
        out_shape=jax.ShapeDtypeStruct((r, w), F32),
        compiler_params=_params(("parallel",), need),
        name="rope_q",
    )(qall, qall, cos4, sin4)


def _flash_kernel(qt_ref, kt_ref, qn_ref, qr_ref, kn_ref, kr_ref, v_ref, o_ref, m_scr, l_scr, acc_scr, *, tq, tk):
    qi = qt_ref[pl.program_id(2)]
    ki = kt_ref[pl.program_id(2)]

    @pl.when(ki == 0)
    def _():
        m_scr[...] = jnp.full_like(m_scr, -jnp.inf)
        l_scr[...] = jnp.zeros_like(l_scr)
        acc_scr[...] = jnp.zeros_like(acc_scr)

    def step(on_diagonal):
        qn = (qn_ref[...] * SM_SCALE).astype(BF16)
        qr = qr_ref[...] * SM_SCALE
        kn = kn_ref[...].astype(BF16)
        kr2 = kr_ref[...].astype(BF16)
        v = v_ref[...].astype(BF16)
        lane = lax.broadcasted_iota(jnp.int32, (1, V7X_LANES), 1)
        if on_diagonal:
            causal = lax.broadcasted_iota(jnp.int32, (tq, 1), 0) >= lax.broadcasted_iota(jnp.int32, (1, tk), 1)
        heads = range(2)
        hs = [slice(e * QK_NOPE_DIM, (e + 1) * QK_NOPE_DIM) for e in heads]
        qre = [jnp.where((lane >= QK_ROPE_DIM) if e else (lane < QK_ROPE_DIM), qr, 0.0).astype(BF16) for e in heads]
        s = [_dot_nt(qn[:, hs[e]], kn[:, hs[e]]) + _dot_nt(qre[e], kr2) for e in heads]
        if on_diagonal:
            s = [jnp.where(causal, s[e], -jnp.inf) for e in heads]
        m_prev = [m_scr[e] for e in heads]
        m_new = [jnp.maximum(m_prev[e], jnp.max(s[e], axis=-1, keepdims=True)) for e in heads]
        corr = [jnp.exp(m_prev[e] - m_new[e]) for e in heads]
        p = [jnp.exp(s[e] - m_new[e]) for e in heads]
        l_new = [l_scr[e] * corr[e] + jnp.sum(p[e], axis=-1, keepdims=True) for e in heads]
        acc_new = [acc_scr[e] * corr[e] + _dot(p[e].astype(BF16), v[:, hs[e]]) for e in heads]
        for e in heads:
            if on_diagonal:
                o_ref[:, e * V_HEAD_DIM:(e + 1) * V_HEAD_DIM] = acc_new[e] / l_new[e]
            else:
                l_scr[e] = l_new[e]
                acc_scr[e] = acc_new[e]
                m_scr[e] = m_new[e]

    @pl.when(ki < qi)
    def _():
        step(False)

    @pl.when(ki == qi)
    def _():
        step(True)


def _flash(qall, q_rope, kvup, kr2, *, batch, seq, tq):
    tk = tq
    nq = seq // tq
    r = qall.shape[0]
    hp = MLA_HEADS // 2
    need = 2 * (tq * 256 * 4 + tq * 128 * 2 + 2 * tk * 256 * 4 + tk * 128 * 4 + tq * 256 * 4) + 4 * tq * 128 * 4 + 8 * tq * tk * 4
    pairs = [(qi, ki) for qi in range(nq) for ki in range(qi + 1)]
    q_tab = jnp.asarray([qk[0] for qk in pairs], jnp.int32)
    k_tab = jnp.asarray([qk[1] for qk in pairs], jnp.int32)
    grid_spec = pltpu.PrefetchScalarGridSpec(
        num_scalar_prefetch=2,
        grid=(batch, hp, len(pairs)),
        in_specs=[
            pl.BlockSpec((tq, 2 * QK_NOPE_DIM), lambda b, h, p, qt, kt: (b * nq + qt[p], h)),
            pl.BlockSpec((tq, V7X_LANES), lambda b, h, p, qt, kt: (b * nq + qt[p], h)),
            pl.BlockSpec((tk, 2 * QK_NOPE_DIM), lambda b, h, p, qt, kt: (b * nq + kt[p], h)),
            pl.BlockSpec((tk, V7X_LANES), lambda b, h, p, qt, kt: (b * nq + kt[p], 0)),
            pl.BlockSpec((tk, 2 * V_HEAD_DIM), lambda b, h, p, qt, kt: (b * nq + kt[p], hp + h)),
        ],
        out_specs=pl.BlockSpec((tq, 2 * V_HEAD_DIM), lambda b, h, p, qt, kt: (b * nq + qt[p], h)),
        scratch_shapes=[
            pltpu.VMEM((2, tq, 1), F32),
            pltpu.VMEM((2, tq, 1), F32),
            pltpu.VMEM((2, tq, V_HEAD_DIM), F32),
        ],
    )
    return pl.pallas_call(
        functools.partial(_flash_kernel, tq=tq, tk=tk),
        grid_spec=grid_spec,
        out_shape=jax.ShapeDtypeStruct((r, MLA_HEADS * V_HEAD_DIM), F32),
        compiler_params=_params(("parallel", "parallel", "arbitrary"), need),
        name="flash",
    )(q_tab, k_tab, qall, q_rope, kvup, kr2, kvup)


def _paged_kernel(pt_ref, ql_ref, qr_ref, cn_ref, krn_ref, lat_hbm, krt_hbm, o_ref,
                  lat_buf, krt_buf, sem, m_scr, l_scr, acc_scr, *, npg, nchunk, new_len, heads):
    c = pl.program_id(1)
    step = pl.program_id(0) * nchunk + c
    n_steps = pl.num_programs(0) * nchunk
    slot = step % 2

    def page_copies(step_, slot_):
        cps = []
        for i in range(npg):
            page = pt_ref[step_ * npg + i]
            cps.append(pltpu.make_async_copy(lat_hbm.at[0, page], lat_buf.at[slot_, i], sem.at[0, slot_]))
            cps.append(pltpu.make_async_copy(krt_hbm.at[0, page], krt_buf.at[slot_, i], sem.at[1, slot_]))
        return cps

    @pl.when(step == 0)
    def _():
        for cp in page_copies(step, slot):
            cp.start()

    @pl.when(step + 1 < n_steps)
    def _():
        for cp in page_copies(step + 1, 1 - slot):
            cp.start()

    @pl.when(c == 0)
    def _():
        m_scr[...] = jnp.full_like(m_scr, -jnp.inf)
        l_scr[...] = jnp.zeros_like(l_scr)
        acc_scr[...] = jnp.zeros_like(acc_scr)

    ql = ql_ref[...]
    qr = qr_ref[...].astype(BF16)

    def update(state, s, vals):
        m_prev, l_prev, acc_prev = state
        m_new = jnp.maximum(m_prev, jnp.max(s, axis=-1, keepdims=True))
        corr = jnp.exp(m_prev - m_new)
        p = jnp.exp(s - m_new)
        w = s.shape[1] // len(vals)
        pv = _dot(p[:, :w].astype(BF16), vals[0])
        for i in range(1, len(vals)):
            pv += _dot(p[:, i * w:(i + 1) * w].astype(BF16), vals[i])
        return m_new, l_prev * corr + jnp.sum(p, axis=-1, keepdims=True), acc_prev * corr + pv

    for cp in page_copies(step, slot):
        cp.wait()
    kls = [lat_buf[slot, i].astype(BF16) for i in range(npg)]

    def scores(lo, hi):
        return jnp.concatenate(
            [_dot_nt(ql, kls[i]) + _dot(qr, krt_buf[slot, i].astype(BF16)) for i in range(lo, hi)], axis=1) * SM_SCALE

    half = npg // 2
    s_lo, s_hi = scores(0, half), scores(half, npg)
    state = update((m_scr[...], l_scr[...], acc_scr[...]), s_lo, kls[:half])
    state = update(state, s_hi, kls[half:])
    m_scr[...], l_scr[...], acc_scr[...] = state

    @pl.when(c == nchunk - 1)
    def _():
        kn = cn_ref[...].astype(BF16)
        s2 = (_dot_nt(ql, kn) + _dot_nt(qr, krn_ref[...].astype(BF16))) * SM_SCALE
        rows = s2.shape[0]
        t_row = lax.broadcasted_iota(jnp.int32, (rows, 1), 0) // heads
        t_col = lax.broadcasted_iota(jnp.int32, (1, s2.shape[1]), 1)
        ok = (t_col <= t_row) & (t_col < new_len)
        _, l_fin, acc_fin = update(state, jnp.where(ok, s2, -jnp.inf), [kn])
        o_ref[...] = acc_fin / l_fin


def _paged(page_table, q_lat, q_rope, ckv_new, kr_new, cache_lat, cache_kr_t, *, npg, new_len):
    db, rows, lat = q_lat.shape
    n_pages = page_table.shape[1]
    nchunk = n_pages // npg
    npad = ckv_new.shape[1]
    assert n_pages == nchunk * npg
    pt_flat = page_table.reshape(-1)
    in_specs = [
        pl.BlockSpec((None, rows, lat), lambda b, c, pt: (b, 0, 0)),
        pl.BlockSpec((None, rows, QK_ROPE_DIM), lambda b, c, pt: (b, 0, 0)),
        pl.BlockSpec((None, npad, lat), lambda b, c, pt: (b, 0, 0)),
        pl.BlockSpec((None, npad, QK_ROPE_DIM), lambda b, c, pt: (b, 0, 0)),
        pl.BlockSpec(memory_space=pl.ANY),
        pl.BlockSpec(memory_space=pl.ANY),
    ]
    need = (2 * npg * PAGE_SIZE * (lat + QK_ROPE_DIM) * 4 + npg * PAGE_SIZE * lat * 2
            + 6 * rows * npg * PAGE_SIZE * 4 + 8 * rows * lat * 4)
    grid_spec = pltpu.PrefetchScalarGridSpec(
        num_scalar_prefetch=1,
        grid=(db, nchunk),
        in_specs=in_specs,
        out_specs=pl.BlockSpec((None, rows, lat), lambda b, c, pt: (b, 0, 0)),
        scratch_shapes=[
            pltpu.VMEM((2, npg, PAGE_SIZE, lat), F32),
            pltpu.VMEM((2, npg, QK_ROPE_DIM, PAGE_SIZE), F32),
            pltpu.SemaphoreType.DMA((2, 2)),
            pltpu.VMEM((rows, 1), F32),
            pltpu.VMEM((rows, 1), F32),
            pltpu.VMEM((rows, lat), F32),
        ],
    )
    return pl.pallas_call(
        functools.partial(_paged_kernel, npg=npg, nchunk=nchunk, new_len=new_len, heads=MLA_HEADS),
        grid_spec=grid_spec,
        out_shape=jax.ShapeDtypeStruct((db, rows, lat), F32),
        compiler_params=_params(("arbitrary", "arbitrary"), need),
        name="paged",
    )(pt_flat, q_lat, q_rope, ckv_new, kr_new, cache_lat, cache_kr_t)


def _prep_kernel(u_ref, tail_ref, s0_ref, mu_ref, w0_ref, a0_ref, kk_ref, ka_ref, wd_ref, wi_ref, wg_ref,
                 r_ref, k_ref, v_ref, kkr_ref, a_ref, lw_ref, g_ref, *, seq_len):
    tm = u_ref.shape[0]
    u = u_ref[...]
    row = lax.broadcasted_iota(jnp.int32, (tm, 1), 0)
    prev = jnp.where(row == 0, tail_ref[SUBLANES - 1:SUBLANES, :], pltpu.roll(u, 1, 0))
    starts = ((pl.program_id(0) * tm + row) & (seq_len - 1)) == 0
    prev = jnp.where(starts, s0_ref[...], prev)
    um = u + (prev - u) * mu_ref[...]
    n = RWKV_DIM
    k = um[:, n:2 * n]
    dw = um[:, 3 * n:3 * n + DECAY_LORA]
    da = um[:, 3 * n + DECAY_LORA:3 * n + DECAY_LORA + ICLR_LORA]
    dg = um[:, 3 * n + DECAY_LORA + ICLR_LORA:]
    z = -(w0_ref[...] + _dot(jnp.tanh(dw).astype(BF16), wd_ref[...]))
    softplus = jnp.maximum(z, 0.0) + jnp.log1p(jnp.exp(-jnp.abs(z)))
    lw_ref[...] = -jnp.exp(-softplus - 0.5)
    a = jax.nn.sigmoid(a0_ref[...] + _dot(da.astype(BF16), wi_ref[...]))
    a_ref[...] = a
    g_ref[...] = _dot(jax.nn.sigmoid(dg).astype(BF16), wg_ref[...])
    r_ref[...] = um[:, :n]
    v_ref[...] = um[:, 2 * n:3 * n]
    kkr_ref[...] = k * kk_ref[...]
    k_ref[...] = k * (1.0 + (a - 1.0) * ka_ref[...])


def _rwkv_prep(u, shift0, mu, w0, a0, k_k, k_a, w_decay, w_iclr, w_gate, *, tm, seq_len):
    r, wu = u.shape
    n = RWKV_DIM
    assert seq_len & (seq_len - 1) == 0
    tiles_per_group = max(seq_len // tm, 1)
    mr = shift0.shape[1]
    row = lambda width: pl.BlockSpec((1, width), lambda i: (0, 0))
    full = lambda arr: pl.BlockSpec(arr.shape, lambda i: (0, 0))
    out_spec = pl.BlockSpec((tm, n), lambda i: (i, 0))
    need = 2 * tm * wu * 4 + 2 * mr * wu * 4 + 2 * 7 * tm * n * 4 + 6 * tm * wu * 4
    return pl.pallas_call(
        functools.partial(_prep_kernel, seq_len=seq_len),
        grid=(r // tm,),
        in_specs=[
            pl.BlockSpec((tm, wu), lambda i: (i, 0)),
            pl.BlockSpec((SUBLANES, wu), lambda i: (jnp.maximum(i * (tm // SUBLANES) - 1, 0), 0)),
            pl.BlockSpec((None, mr, wu), lambda i: (i // tiles_per_group, 0, 0)),
            row(wu), row(n), row(n), row(n), row(n),
            full(w_decay), full(w_iclr), full(w_gate),
        ],
        out_specs=[out_spec] * 7,
        out_shape=[jax.ShapeDtypeStruct((r, n), F32)] * 7,
        compiler_params=_params(("parallel",), need),
        name="rwkv_prep",
    )(u, u, shift0, mu.reshape(1, wu), w0.reshape(1, n), a0.reshape(1, n), k_k.reshape(1, n), k_a.reshape(1, n),
      w_decay, w_iclr, w_gate)


def _wkv_kernel(*refs, c_len, hg, gps, n_chunks, has_s0):
    if has_s0:
        (r_ref, k_ref, v_ref, kkr_ref, a_ref, lw_ref, g_ref, rk_ref, lng_ref, lnb_ref, s0_ref,
         o_ref, sout_ref, st_scr) = refs
    else:
        (r_ref, k_ref, v_ref, kkr_ref, a_ref, lw_ref, g_ref, rk_ref, lng_ref, lnb_ref,
         o_ref, sout_ref, st_scr) = refs
    n = RWKV_HEAD_DIM
    w = hg * n
    rw = hg * c_len
    shift = int(math.log2(n))
    ci = pl.program_id(2)

    lane_head = lax.broadcasted_iota(jnp.int32, (1, w), 1) >> shift
    key_head = lax.broadcasted_iota(jnp.int32, (w, 1), 0) >> shift
    same_head = key_head == lane_head
    row_head = lax.broadcasted_iota(jnp.int32, (rw, 1), 0) // c_len
    stack_mask = row_head == lane_head

    @pl.when(ci == 0)
    def _():
        for gi in range(gps):
            if has_s0:
                s0t = jnp.transpose(s0_ref[gi * hg:(gi + 1) * hg].reshape(w, n))
                st_scr[gi] = jnp.where(same_head, jnp.concatenate([s0t] * hg, axis=0), 0.0)
            else:
                st_scr[gi] = jnp.zeros((w, w), F32)

    new_states = _wkv_groups(gps, r_ref, k_ref, v_ref, kkr_ref, a_ref, lw_ref, g_ref, rk_ref, lng_ref, lnb_ref,
                             o_ref, st_scr, c_len=c_len, hg=hg, same_head=same_head, stack_mask=stack_mask)

    @pl.when(ci == n_chunks - 1)
    def _():
        for gi, st_new in enumerate(new_states):
            z = st_new[0:n]
            for e in range(1, hg):
                z = z + st_new[e * n:(e + 1) * n]
            sout_ref[gi * hg:(gi + 1) * hg] = jnp.transpose(z).reshape(hg, n, n)


def _wkv_groups(gps, r_ref, k_ref, v_ref, kkr_ref, a_ref, lw_ref, g_ref, rk_ref, lng_ref, lnb_ref, o_ref, st_scr, *,
                c_len, hg, same_head, stack_mask):
    n = RWKV_HEAD_DIM
    w = hg * n
    rw = hg * c_len
    groups = range(gps)
    lanes = [slice(gi * w, (gi + 1) * w) for gi in groups]
    each = lambda fn, *cols: [fn(*xs) for xs in zip(*cols)]

    seg_ones = jnp.where(same_head, 1.0, 0.0).astype(BF16)

    def seg_sum(x):
        hi, lo = _split_bf16(x, 2)
        return _dot(hi, seg_ones) + _dot(lo, seg_ones)

    def stack(x):
        return jnp.where(stack_mask, jnp.concatenate([x] * hg, axis=0), 0.0).astype(BF16)

    t_r = lax.broadcasted_iota(jnp.int32, (c_len, 1), 0)
    t_c = lax.broadcasted_iota(jnp.int32, (1, c_len), 1)
    tril = jnp.where(t_c <= t_r, 1.0, 0.0).astype(BF16)
    ri = lax.broadcasted_iota(jnp.int32, (rw, 1), 0)
    cj = lax.broadcasted_iota(jnp.int32, (1, rw), 1)
    strict = cj < ri
    incl = cj <= ri

    r = [r_ref[:, ls] for ls in lanes]
    k = [k_ref[:, ls] for ls in lanes]
    v = [v_ref[:, ls] for ls in lanes]
    a = [a_ref[:, ls] for ls in lanes]
    lw = [lw_ref[:, ls] for ls in lanes]
    kkr = [kkr_ref[:, ls] for ls in lanes]

    def cumsum(x):
        l1, l2, l3 = _split_bf16(x, 3)
        return _dot(tril, l1) + _dot(tril, l2) + _dot(tril, l3)

    cum = each(cumsum, lw)
    cum_end = [c[c_len - 1:c_len, :] for c in cum]
    kkn = each(lambda x: x / jnp.maximum(jnp.sqrt(seg_sum(x * x)), 1e-12), kkr)
    kka = each(lambda x, y: x * y, kkn, a)
    p_inv = each(lambda c: jnp.exp(-c), cum)
    p_tail = each(lambda ce, c: jnp.exp(ce - c), cum_end, cum)

    a_s = each(lambda c, l, x: stack(jnp.exp(c - l) * (-x)), cum, lw, kkn)
    r_s = each(lambda c, x: stack(jnp.exp(c) * x), cum, r)
    b_s = each(lambda p, x: stack(p * x), p_inv, kka)
    k_s = each(lambda p, x: stack(p * x), p_inv, k)
    v_s = each(stack, v)
    be_s = each(lambda p, x: stack(p * x), p_tail, kka)
    ke_s = each(lambda p, x: stack(p * x), p_tail, k)

    l_ab = each(lambda x, y: jnp.where(strict, _dot_nt(x, y), 0.0).astype(BF16), a_s, b_s)
    l_ak = each(lambda x, y: jnp.where(strict, _dot_nt(x, y), 0.0).astype(BF16), a_s, k_s)
    m_rb = each(lambda x, y: jnp.where(incl, _dot_nt(x, y), 0.0).astype(BF16), r_s, b_s)
    m_rk = each(lambda x, y: jnp.where(incl, _dot_nt(x, y), 0.0).astype(BF16), r_s, k_s)

    st = [st_scr[gi] for gi in groups]
    st_b = [s.astype(BF16) for s in st]
    x = each(lambda p, s, l, q: _dot(p, s) + _dot(l, q), a_s, st_b, l_ak, v_s)
    n_it = int(math.log2(c_len))
    li = l_ab
    for it in range(n_it):
        x = each(lambda xx, l: xx + _dot(l, xx.astype(BF16)), x, li)
        if it < n_it - 1:
            li = each(lambda l: _dot(l, l).astype(BF16), li)
    u_s = [xx.astype(BF16) for xx in x]

    def fold_heads(y_s):
        y = y_s[0:c_len]
        for e in range(1, hg):
            y = y + y_s[e * c_len:(e + 1) * c_len]
        return y

    y = each(lambda p, s, m1, u, m2, q: fold_heads(_dot(p, s) + _dot(m1, u) + _dot(m2, q)),
             r_s, st_b, m_rb, u_s, m_rk, v_s)

    def new_state(ce, s, b, u, kk_, q):
        pc_col = jnp.transpose(jnp.broadcast_to(jnp.exp(ce), (V7X_LANES, w)))[:, 0:1]
        return pc_col * s + _dot_tn(b, u) + _dot_tn(kk_, q)

    st_new = each(new_state, cum_end, st, be_s, u_s, ke_s, v_s)
    for gi in groups:
        st_scr[gi] = st_new[gi]

    inv_n = 1.0 / n
    mean = each(lambda yy: seg_sum(yy) * inv_n, y)
    d = each(lambda yy, m: yy - m, y, mean)
    var = each(lambda dd: seg_sum(dd * dd) * inv_n, d)
    bonus = each(lambda rr, kk_, ls, vv: seg_sum(rr * kk_ * rk_ref[:, ls]) * vv, r, k, lanes, v)
    for gi in groups:
        ls = lanes[gi]
        yn = d[gi] * lax.rsqrt(var[gi] + GN_EPS) * lng_ref[:, ls] + lnb_ref[:, ls]
        o_ref[:, ls] = (yn + bonus[gi]) * g_ref[:, ls]
    return st_new


def _wkv(r, k, v, kkr, a, lw, g, r_k, lnx_g, lnx_b, s0, *, batch, c_len, n_chunks, gps):
    rows, dim = r.shape
    hg = WKV_HG
    n = RWKV_HEAD_DIM
    w = hg * n
    wb = gps * w
    blk = pl.BlockSpec((c_len, wb), lambda b, gi, ci: (b * n_chunks + ci, gi))
    vec = pl.BlockSpec((1, wb), lambda b, gi, ci: (0, gi))
    st_spec = pl.BlockSpec((None, gps * hg, n, n), lambda b, gi, ci: (b, gi, 0, 0))
    in_specs = [blk] * 7 + [vec] * 3
    args = [r, k, v, kkr, a, lw, g, r_k.reshape(1, dim), lnx_g.reshape(1, dim), lnx_b.reshape(1, dim)]
    if s0 is not None:
        in_specs.append(st_spec)
        args.append(s0)
    rw = hg * c_len
    need = 2 * 8 * c_len * wb * 4 + gps * (5 * w * w * 4 + 16 * rw * w * 4 + 8 * rw * rw * 4)
    return pl.pallas_call(
        functools.partial(_wkv_kernel, c_len=c_len, hg=hg, gps=gps, n_chunks=n_chunks, has_s0=s0 is not None),
        grid=(batch, dim // wb, n_chunks),
        in_specs=in_specs,
        out_specs=[blk, st_spec],
        out_shape=[
            jax.ShapeDtypeStruct((rows, dim), F32),
            jax.ShapeDtypeStruct((batch, dim // n, n, n), F32),
        ],
        scratch_shapes=[pltpu.VMEM((gps, w, w), F32)],
        compiler_params=_params(("parallel", "parallel", "arbitrary"), need),
        name="wkv",
    )(*args)


def _wkv_lanes_kernel(r_ref, k_ref, v_ref, kkr_ref, a_ref, lw_ref, g_ref, rk_ref, lng_ref, lnb_ref, s0_ref,
                      o_ref, sout_ref, y_scr, *, steps):
    n = RWKV_HEAD_DIM
    decay, kkn, kka, kt, rt = [], [], [], [], []
    for t in range(steps):
        kkr = kkr_ref[t]
        norm = jnp.sqrt(jnp.sum(kkr * kkr, axis=0, keepdims=True))
        kk = kkr / jnp.maximum(norm, 1e-12)
        decay.append(jnp.exp(lw_ref[t]))
        kkn.append(kk)
        kka.append(kk * a_ref[t])
        kt.append(k_ref[t])
        rt.append(r_ref[t])

    def row(vi, carry):
        s = s0_ref[vi]
        for t in range(steps):
            sa = jnp.sum(s * kkn[t], axis=0, keepdims=True)
            s = s * decay[t] - sa * kka[t] + v_ref[t, pl.ds(vi, 1), :] * kt[t]
            y_scr[t, pl.ds(vi, 1), :] = jnp.sum(s * rt[t], axis=0, keepdims=True)
        sout_ref[vi] = s
        return carry

    lax.fori_loop(0, n, row, 0)

    for t in range(steps):
        y = y_scr[t]
        mean = jnp.mean(y, axis=0, keepdims=True)
        d = y - mean
        var = jnp.mean(d * d, axis=0, keepdims=True)
        yn = d * lax.rsqrt(var + GN_EPS) * lng_ref[...] + lnb_ref[...]
        bonus = jnp.sum(rt[t] * kt[t] * rk_ref[...], axis=0, keepdims=True) * v_ref[t]
        o_ref[t] = (yn + bonus) * g_ref[t]


def _wkv_lanes(r, k, v, kkr, a, lw, g, r_k, lnx_g, lnx_b, s0):
    steps, dim, nb = r.shape
    n = RWKV_HEAD_DIM
    heads = dim // n
    blk = pl.BlockSpec((steps, n, nb), lambda h: (0, h, 0))
    par = pl.BlockSpec((n, nb), lambda h: (h, 0))
    st = pl.BlockSpec((None, n, n, nb), lambda h: (h, 0, 0, 0))
    need = 2 * 8 * steps * n * nb * 4 + 4 * n * n * nb * 4 + 32 * n * nb * 4
    return pl.pallas_call(
        functools.partial(_wkv_lanes_kernel, steps=steps),
        grid=(heads,),
        in_specs=[blk] * 7 + [par] * 3 + [st],
        out_specs=[blk, st],
        out_shape=[jax.ShapeDtypeStruct((steps, dim, nb), F32), jax.ShapeDtypeStruct((heads, n, n, nb), F32)],
        scratch_shapes=[pltpu.VMEM((steps, n, nb), F32)],
        compiler_params=_params(("parallel",), need),
        name="wkv_lanes",
    )(r, k, v, kkr, a, lw, g, r_k, lnx_g, lnx_b, s0)


def _rope_tables(pos):
    inv = ROPE_THETA ** (-jnp.arange(0, QK_ROPE_DIM, 2, dtype=F32) / QK_ROPE_DIM)
    ang = pos[:, None] * inv[None, :]
    cos, sin = jnp.cos(ang), jnp.sin(ang)
    return jnp.concatenate([cos] * 4, axis=1), jnp.concatenate([-sin, sin] * 2, axis=1)


def _swap_halves(w):
    half = w.shape[-1] // 2
    return jnp.concatenate([w[..., half:], w[..., :half]], axis=-1)


def _prepare_weights(w_in, w_uq, w_ukv, w_o, w_decay, w_iclr, w_gate):
    d = w_in.shape[0]
    rope = w_in[:, Q_LORA_RANK + KV_LORA_RANK:W_MLA_IN]
    rope_sw = _swap_halves(rope)
    zeros = lambda c: jnp.zeros((d, c), w_in.dtype)
    w_mla = jnp.concatenate(
        [w_in[:, :Q_LORA_RANK], zeros(MLA_KV_OFF - Q_LORA_RANK), w_in[:, Q_LORA_RANK:Q_LORA_RANK + KV_LORA_RANK],
         rope, rope, rope_sw, rope_sw, zeros(MLA_COLS - MLA_KV_OFF - KV_LORA_RANK - 4 * QK_ROPE_DIM)], axis=1).astype(BF16)
    w_rwkv = w_in[:, W_MLA_IN:].astype(BF16)
    uq = w_uq.reshape(Q_LORA_RANK, MLA_HEADS, QK_NOPE_DIM + QK_ROPE_DIM)
    uq_rope = uq[..., QK_NOPE_DIM:]
    w_q = jnp.concatenate(
        [uq[..., :QK_NOPE_DIM].reshape(Q_LORA_RANK, -1), uq_rope.reshape(Q_LORA_RANK, -1),
         _swap_halves(uq_rope).reshape(Q_LORA_RANK, -1)], axis=1).astype(BF16)
    ukv = w_ukv.reshape(KV_LORA_RANK, MLA_HEADS, QK_NOPE_DIM + V_HEAD_DIM)
    w_uk, w_uv = ukv[..., :QK_NOPE_DIM], ukv[..., QK_NOPE_DIM:]
    w_kvup = jnp.concatenate([w_uk.reshape(KV_LORA_RANK, -1), w_uv.reshape(KV_LORA_RANK, -1)], axis=1).astype(BF16)
    w_uk_t = jnp.transpose(w_uk, (1, 2, 0)).astype(BF16)
    w_uv_h = jnp.transpose(w_uv, (1, 0, 2)).astype(BF16)
    return dict(w_mla=_tile_major(w_mla, MM_TN), w_rwkv=_tile_major(w_rwkv, MM_TN), w_q=_tile_major(w_q, MM_TN),
                w_kvup=_tile_major(w_kvup, MM_TN), w_uk_t=w_uk_t, w_uv_h=w_uv_h,
                w_o=w_o.astype(BF16), w_decay=w_decay.astype(BF16), w_iclr=w_iclr.astype(BF16),
                w_gate=w_gate.astype(BF16))


def _group_layer(x, mods, tiles_per_group, tm, wts, p, *, cos4, sin4, ffn, attend, wkv_run, shift0, seq_len):
    sh2, sc2, g2 = mods
    x1 = ffn(0, x)
    mla = _mm(x1, wts["w_mla"], tm=tm, mod=(sh2, sc2), tiles_per_group=tiles_per_group, name="proj_mla")
    u = _mm(x1, wts["w_rwkv"], tm=tm, mod=(sh2, sc2), tiles_per_group=tiles_per_group, name="proj_rwkv")
    ckv, kr, kr2 = _lat_prep(mla, p["g_kv"], cos4, sin4, tm=tm)
    qall = _mm(mla, wts["w_q"], tm=tm, k=Q_LORA_RANK, rms_g=p["g_q"], name="q_proj")
    q_rope = _rope_q(qall, cos4, sin4, tm=tm)
    attn = attend(qall, q_rope, ckv, kr, kr2)
    prep = _rwkv_prep(u, shift0, p["mu_shift"], p["w0"], p["a0"], p["k_k"], p["k_a"],
                      wts["w_decay"], wts["w_iclr"], wts["w_gate"], tm=PREP_TM, seq_len=seq_len)
    rwkv, s_new = wkv_run(prep)
    x2 = _oproj(x1, g2, attn, rwkv, wts["w_o"], p["ln_g"][1], p["ln_b"][1], tm=tm, tk=512,
                tiles_per_group=tiles_per_group)
    return ffn(1, x2), ckv, kr, s_new, u


def kernel(x_prompt, x_sample, c_prompt, c_sample, cache_kv_latent, cache_k_rope, state_wkv, state_shift, page_table, w_ada, b_ada, ln_g, ln_b, w_ffn1_in, w_ffn1_out, w_ffn2_in, w_ffn2_out, w_in, g_q, g_kv, w_uq, w_ukv, mu_shift, w0, w_decay, a0, w_iclr, w_gate, k_k, k_a, r_k, lnx_g, lnx_b, w_o):
    batch, seq, d = x_prompt.shape
    db, dseq, _ = x_sample.shape
    depth = w_ada.shape[0]
    assert depth == DEPTH == 1
    n_ada = w_ada.shape[2] // d
    past = page_table.shape[1] * PAGE_SIZE
    wu = state_shift.shape[2]

    c_prompt_rows = jnp.pad(c_prompt, ((0, (-batch) % SUBLANES), (0, 0)))
    cos_p, sin_p = _rope_tables(jnp.arange(seq, dtype=F32) + 0)
    pos_s = jnp.tile(jnp.arange(dseq, dtype=F32) + past, db)
    cos_s, sin_s = _rope_tables(pos_s)

    yp = x_prompt.reshape(batch * seq, d)
    ys = x_sample.reshape(db * dseq, d)
    outs_p, outs_s = [], []
    for l in range(depth):
        p = dict(ln_g=ln_g[l], ln_b=ln_b[l], g_q=g_q[l], g_kv=g_kv[l], mu_shift=mu_shift[l], w0=w0[l], a0=a0[l],
                 k_k=k_k[l], k_a=k_a[l], r_k=r_k[l], lnx_g=lnx_g[l], lnx_b=lnx_b[l])
        wts = _prepare_weights(w_in[l], w_uq[l], w_ukv[l], w_o[l], w_decay[l], w_iclr[l], w_gate[l])
        ffn_w32 = ((w_ffn1_in[l], w_ffn1_out[l]), (w_ffn2_in[l], w_ffn2_out[l]))
        ffn_w16 = [None, None]
        ada_p, ada_s = _ada(c_prompt_rows, c_sample, w_ada[l], b_ada[l])
        mods_p = [ada_p[i, :batch][:, None, :] for i in range(n_ada)]
        mods_s = [ada_s[i][None] for i in range(n_ada)]
        per_token = lambda m: jnp.repeat(m[0], dseq, axis=0)[None]

        tm_s = db * dseq
        c_pad = 16
        t_major = lambda a: a.reshape(db, dseq, d).transpose(1, 0, 2).reshape(tm_s, d)
        b_major = lambda a: a.reshape(dseq, db, d).transpose(1, 0, 2).reshape(tm_s, d)

        def ffn_s(i, x):
            sh, sc, g = mods_s[6 * i:6 * i + 3]
            y, w_in16, w_out16 = _ffn(t_major(x), sh, sc, g, *ffn_w32[i], p["ln_g"][2 * i], p["ln_b"][2 * i],
                                      tm=tm_s, tf=128, tiles_per_group=1, emit_bf16_weights=True)
            ffn_w16[i] = (w_in16, w_out16)
            return b_major(y)

        def attend_s(qall, q_rope, ckv, kr, kr2):
            rows = dseq * MLA_HEADS
            q_lat = _bmm(qall, wts["w_uk_t"], tm=tm_s, out_dtype=BF16, name="q_absorb")
            pad = lambda t: jnp.pad(t.reshape(db, dseq, -1), ((0, 0), (0, c_pad - dseq), (0, 0)))
            o_lat = _paged(page_table, q_lat.reshape(db, rows, KV_LORA_RANK), q_rope.reshape(db, rows, QK_ROPE_DIM),
                           pad(ckv), pad(kr), cache_kv_latent, jnp.swapaxes(cache_k_rope, 2, 3),
                           npg=PAGES_PER_STEP, new_len=dseq)
            return _bmm(o_lat.reshape(db * dseq, MLA_HEADS * KV_LORA_RANK), wts["w_uv_h"], tm=tm_s, name="v_up")

        def wkv_s(prep):
            lanes = lambda a: a.reshape(db, dseq, -1).transpose(1, 2, 0)
            over_batch = lambda v: jnp.broadcast_to(v.reshape(-1, 1), (v.size, db))
            out, s_new = _wkv_lanes(*[lanes(t) for t in prep], over_batch(p["r_k"]), over_batch(p["lnx_g"]),
                                    over_batch(p["lnx_b"]), jnp.transpose(state_wkv[l], (1, 2, 3, 0)))
            return out.transpose(2, 0, 1).reshape(db * dseq, -1), jnp.transpose(s_new, (3, 0, 1, 2))

        shift_rows = jnp.repeat(state_shift[l], dseq, axis=0).reshape(db * dseq // PREP_TM, PREP_TM, wu)
        ys, ckv_s, kr_s, s_s, u_s = _group_layer(
            ys, [per_token(m) for m in mods_s[3:6]], 1, tm_s, wts, p, cos4=cos_s, sin4=sin_s, ffn=ffn_s,
            attend=attend_s, wkv_run=wkv_s, shift0=shift_rows, seq_len=dseq)
        outs_s.append((ckv_s.reshape(db, dseq, -1), kr_s.reshape(db, dseq, -1), s_s,
                       u_s.reshape(db, dseq, wu)[:, -1]))

        tm_p = 512

        def ffn_p(i, x):
            sh, sc, g = mods_p[6 * i:6 * i + 3]
            return _ffn(x, sh, sc, g, *ffn_w16[i], p["ln_g"][2 * i], p["ln_b"][2 * i],
                        tm=tm_p, tf=FFN_TF, tiles_per_group=seq // tm_p)

        def attend_p(qall, q_rope, ckv, kr, kr2):
            kvup = _mm(ckv, wts["w_kvup"], tm=tm_p, name="kv_up")
            return _flash(qall, q_rope, kvup, kr2, batch=batch, seq=seq, tq=512)

        def wkv_p(prep):
            return _wkv(*prep, p["r_k"], p["lnx_g"], p["lnx_b"], None, batch=batch, c_len=64, n_chunks=seq // 64,
                        gps=8)

        yp, ckv_p, kr_p, s_p, u_p = _group_layer(
            yp, mods_p[3:6], seq // tm_p, tm_p, wts, p, cos4=cos_p, sin4=sin_p, ffn=ffn_p, attend=attend_p,
            wkv_run=wkv_p, shift0=jnp.zeros((batch, 1, wu), F32), seq_len=seq)
        outs_p.append((ckv_p.reshape(batch, seq, -1), kr_p.reshape(batch, seq, -1), s_p,
                       u_p.reshape(batch, seq, wu)[:, -1]))

    stack = lambda outs, i: jnp.stack([o[i] for o in outs])
    return (yp.reshape(batch, seq, d), ys.reshape(db, dseq, d),
            stack(outs_p, 0), stack(outs_p, 1), stack(outs_p, 2), stack(outs_p, 3),
            stack(outs_s, 0), stack(outs_s, 1), stack(outs_s, 2), stack(outs_s, 3))
```

```python
import functools
import math

import jax
import jax.numpy as jnp
from jax import lax
from jax.experimental import pallas as pl
from jax.experimental.pallas import tpu as pltpu

F32 = jnp.float32
BF16 = jnp.bfloat16

QK_NOPE_DIM = 128
QK_ROPE_DIM = 64
V_HEAD_DIM = 128
MLA_HEADS = 16
Q_LORA_RANK = 896
KV_LORA_RANK = 512
RWKV_HEAD_DIM = 64
RWKV_HEADS = 32
RWKV_DIM = RWKV_HEADS * RWKV_HEAD_DIM
DECAY_LORA = 128
ICLR_LORA = 128
GATE_LORA = 256
W_MLA_IN = Q_LORA_RANK + KV_LORA_RANK + QK_ROPE_DIM
PAGE_SIZE = 128
ROPE_THETA = 10000.0
SM_SCALE = (QK_NOPE_DIM + QK_ROPE_DIM) ** -0.5
DEPTH = 1
ALPHA = (2 * DEPTH) ** 0.25
LN_EPS = 1e-5
RMS_EPS = 1e-6
GN_EPS = 64e-5

V7X_VMEM_BYTES = 64 * 1024 * 1024
V7X_LANES = 128
SUBLANES = 8
MIB = 1024 * 1024

MLA_COLS = 2048
MLA_KV_OFF = 1024

ROW_CHUNK = 64
N_CHUNK = 512
MM_TN = 512
FFN_TF = 256
PREP_TM = 128
PAGE_BUFFERS = 3
PAGES_PER_STEP = 32
WKV_HG = 4


def _params(sem, vmem_bytes):
    limit = min(int(vmem_bytes) + 8 * MIB, V7X_VMEM_BYTES - 6 * MIB)
    return pltpu.CompilerParams(dimension_semantics=sem, vmem_limit_bytes=limit)


def _dot(a, b):
    return jnp.dot(a, b, preferred_element_type=F32)


def _dot_nt(a, b):
    return lax.dot_general(a, b, (((1,), (1,)), ((), ())), preferred_element_type=F32)


def _dot_tn(a, b):
    return lax.dot_general(a, b, (((0,), (0,)), ((), ())), preferred_element_type=F32)


def _split_bf16(x, parts):
    out = []
    rem = x
    for _ in range(parts):
        h = rem.astype(BF16)
        out.append(h)
        rem = rem - h.astype(F32)
    return out


def _layernorm_rows(z, g, b):
    mu = jnp.mean(z, axis=-1, keepdims=True)
    d = z - mu
    var = jnp.mean(d * d, axis=-1, keepdims=True)
    return d * lax.rsqrt(var + LN_EPS) * g + b


def _for_row_chunks(n_rows, fn, rows=ROW_CHUNK):
    def body(i, carry):
        fn(pl.ds(pl.multiple_of(i * rows, rows), rows))
        return carry

    lax.fori_loop(0, n_rows // rows, body, 0)


def _rows_of(ref, rs):
    mr = ref.shape[0]
    if mr == 1:
        return ref[...]
    return ref[pl.ds(pl.multiple_of(rs.start % mr, rs.size), rs.size), :]


def _modulate_into(h_scr, x_ref, sh_ref, sc_ref):
    def chunk(rs):
        h_scr[rs, :] = (x_ref[rs, :] * (1.0 + _rows_of(sc_ref, rs)) + _rows_of(sh_ref, rs)).astype(BF16)

    _for_row_chunks(h_scr.shape[0], chunk)


def _residual_layernorm(o_ref, x_ref, gate_ref, lng_ref, lnb_ref, gate_scale):
    def chunk(rs):
        z = ALPHA * x_ref[rs, :] + gate_scale * _rows_of(gate_ref, rs) * o_ref[rs, :]
        o_ref[rs, :] = _layernorm_rows(z, lng_ref[...], lnb_ref[...])

    _for_row_chunks(o_ref.shape[0], chunk)


def _ada_kernel(cp_ref, cs_ref, w_ref, b_ref, op_ref, os_ref):
    w = w_ref[...].astype(BF16)
    for c_ref, o_ref in ((cp_ref, op_ref), (cs_ref, os_ref)):
        c = c_ref[...]
        o_ref[...] = _dot((c * jax.nn.sigmoid(c)).astype(BF16), w) + b_ref[...]


def _ada(c_prompt, c_sample, w_ada, b_ada, tn=512):
    mp, k = c_prompt.shape
    ms = c_sample.shape[0]
    n = w_ada.shape[1]
    n_ada = n // k
    per = k // tn
    need = 2 * (k * tn * 4) + k * tn * 2 + 2 * (mp + ms) * k * 4 + 4 * (mp + ms) * tn * 4
    return pl.pallas_call(
        _ada_kernel,
        grid=(n_ada, per),
        in_specs=[
            pl.BlockSpec((mp, k), lambda i, j: (0, 0)),
            pl.BlockSpec((ms, k), lambda i, j: (0, 0)),
            pl.BlockSpec((k, tn), lambda i, j: (0, i * per + j)),
            pl.BlockSpec((1, tn), lambda i, j: (0, i * per + j)),
        ],
        out_specs=[pl.BlockSpec((None, mp, tn), lambda i, j: (i, 0, j)),
                   pl.BlockSpec((None, ms, tn), lambda i, j: (i, 0, j))],
        out_shape=[jax.ShapeDtypeStruct((n_ada, mp, k), F32), jax.ShapeDtypeStruct((n_ada, ms, k), F32)],
        compiler_params=_params(("parallel", "parallel"), need),
        name="ada",
    )(c_prompt, c_sample, w_ada, b_ada.reshape(1, n))


def _ffn_kernel(x_ref, sh_ref, sc_ref, g_ref, wg_ref, wu_ref, wo_ref, lng_ref, lnb_ref, o_ref, *rest, nj):
    h_scr = rest[-1]
    j = pl.program_id(1)

    @pl.when(j == 0)
    def _():
        _modulate_into(h_scr, x_ref, sh_ref, sc_ref)
        o_ref[...] = jnp.zeros_like(o_ref)

    wg, wu, wo = wg_ref[...], wu_ref[...], wo_ref[...]
    if len(rest) > 1:
        wg, wu, wo = wg.astype(BF16), wu.astype(BF16), wo.astype(BF16)
        rest[0][...] = wg
        rest[1][...] = wu
        rest[2][...] = wo
    h = h_scr[...]
    gate = _dot(h, wg)
    up = _dot(h, wu)
    act = (gate * jax.nn.sigmoid(gate) * up).astype(BF16)
    for n0 in range(0, o_ref.shape[1], N_CHUNK):
        o_ref[:, n0:n0 + N_CHUNK] += _dot(act, wo[:, n0:n0 + N_CHUNK])

    @pl.when(j == nj - 1)
    def _():
        _residual_layernorm(o_ref, x_ref, g_ref, lng_ref, lnb_ref, 0.5)


def _ffn(x, sh, sc, g, w_in, w_out, ln_g, ln_b, *, tm, tf, tiles_per_group, emit_bf16_weights=False):
    r, d = x.shape
    ff = w_out.shape[0]
    nj = ff // tf
    mr = sh.shape[1]
    if isinstance(w_in, tuple):
        w_gate, w_up = w_in
        assert w_gate.shape[2] == tf
        gate_spec = up_spec = pl.BlockSpec((None, d, tf), lambda i, j: (j, 0, 0))
    else:
        w_gate = w_up = w_in
        gate_spec = pl.BlockSpec((d, tf), lambda i, j: (0, j))
        up_spec = pl.BlockSpec((d, tf), lambda i, j: (0, nj + j))
    wbytes = w_out.dtype.itemsize
    mod_spec = pl.BlockSpec((None, mr, d), lambda i, j: (i // tiles_per_group, 0, 0))
    out_specs = [pl.BlockSpec((tm, d), lambda i, j: (i, 0))]
    out_shape = [jax.ShapeDtypeStruct((r, d), F32)]
    need = 4 * tm * d * 4 + tm * d * 2 + 2 * 3 * d * tf * wbytes + 6 * mr * d * 4 + 6 * tm * tf * 4
    if emit_bf16_weights:
        assert r == tm and w_out.dtype == F32 and FFN_TF % tf == 0
        per_tile = FFN_TF // tf
        emit_spec = pl.BlockSpec((None, d, tf), lambda i, j: (j // per_tile, 0, j % per_tile))
        out_specs += [emit_spec, emit_spec, pl.BlockSpec((tf, d), lambda i, j: (j, 0))]
        out_shape += [jax.ShapeDtypeStruct((ff // FFN_TF, d, FFN_TF), BF16)] * 2 + [jax.ShapeDtypeStruct((ff, d), BF16)]
        need += 3 * 3 * d * tf * 2
    outs = pl.pallas_call(
        functools.partial(_ffn_kernel, nj=nj),
        grid=(r // tm, nj),
        in_specs=[
            pl.BlockSpec((tm, d), lambda i, j: (i, 0)),
            mod_spec, mod_spec, mod_spec,
            gate_spec, up_spec,
            pl.BlockSpec((tf, d), lambda i, j: (j, 0)),
            pl.BlockSpec((1, d), lambda i, j: (0, 0)),
            pl.BlockSpec((1, d), lambda i, j: (0, 0)),
        ],
        out_specs=out_specs,
        out_shape=out_shape,
        scratch_shapes=[pltpu.VMEM((tm, d), BF16)],
        compiler_params=_params(("parallel", "arbitrary"), need),
        name="ffn",
    )(x, sh, sc, g, w_gate, w_up, w_out, ln_g.reshape(1, d), ln_b.reshape(1, d))
    return (outs[0], (outs[1], outs[2]), outs[3]) if emit_bf16_weights else outs[0]


def _mm_mod_kernel(a_ref, sh_ref, sc_ref, w_ref, o_ref, h_scr):
    @pl.when(pl.program_id(1) == 0)
    def _():
        _modulate_into(h_scr, a_ref, sh_ref, sc_ref)

    o_ref[...] = _dot(h_scr[...], w_ref[...]).astype(o_ref.dtype)


def _mm_rms_kernel(a_ref, g_ref, w_ref, o_ref, h_scr):
    @pl.when(pl.program_id(1) == 0)
    def _():
        a = a_ref[...]
        ms = jnp.mean(a * a, axis=-1, keepdims=True)
        h_scr[...] = (a * lax.rsqrt(ms + RMS_EPS) * g_ref[...]).astype(BF16)

    o_ref[...] = _dot(h_scr[...], w_ref[...]).astype(o_ref.dtype)


def _mm_plain_kernel(a_ref, w_ref, o_ref, h_scr):
    @pl.when(pl.program_id(1) == 0)
    def _():
        h_scr[...] = a_ref[...].astype(BF16)

    o_ref[...] = _dot(h_scr[...], w_ref[...]).astype(o_ref.dtype)


def _tile_major(w, tn):
    k, n = w.shape
    return w.reshape(k, n // tn, tn).transpose(1, 0, 2)


def _mm(a, w, *, tm, k=None, a_colblk=0, mod=None, rms_g=None, tiles_per_group=1, out_dtype=F32, name="mm"):
    r = a.shape[0]
    k = a.shape[1] if k is None else k
    tn = w.shape[2]
    n = w.shape[0] * tn
    a_spec = pl.BlockSpec((tm, k), lambda i, j: (i, a_colblk))
    w_spec = pl.BlockSpec((None, k, tn), lambda i, j: (j, 0, 0))
    need = 2 * tm * k * a.dtype.itemsize + tm * k * 2 + 2 * k * tn * 2 + 4 * tm * tn * 4
    if mod is not None:
        sh, sc = mod
        mr = sh.shape[1]
        mod_spec = pl.BlockSpec((None, mr, k), lambda i, j: (i // tiles_per_group, 0, 0))
        kern, ins, args = _mm_mod_kernel, [a_spec, mod_spec, mod_spec, w_spec], (a, sh, sc, w)
        need += 4 * mr * k * 4
    elif rms_g is not None:
        kern, ins, args = _mm_rms_kernel, [a_spec, pl.BlockSpec((1, k), lambda i, j: (0, 0)), w_spec], (a, rms_g.reshape(1, k), w)
    else:
        kern, ins, args = _mm_plain_kernel, [a_spec, w_spec], (a, w)
    return pl.pallas_call(
        kern,
        grid=(r // tm, n // tn),
        in_specs=ins,
        out_specs=pl.BlockSpec((tm, tn), lambda i, j: (i, j)),
        out_shape=jax.ShapeDtypeStruct((r, n), out_dtype),
        scratch_shapes=[pltpu.VMEM((tm, k), BF16)],
        compiler_params=_params(("parallel", "arbitrary"), need),
        name=name,
    )(*args)


def _bmm_kernel(a_ref, w_ref, o_ref):
    o_ref[...] = _dot(a_ref[...].astype(BF16), w_ref[...]).astype(o_ref.dtype)


def _bmm(a, w, *, tm, out_dtype=F32, name="bmm"):
    r = a.shape[0]
    h, ka, nb = w.shape
    need = 2 * tm * ka * 4 + 2 * ka * nb * 2 + 2 * tm * nb * 4
    return pl.pallas_call(
        _bmm_kernel,
        grid=(r // tm, h),
        in_specs=[
            pl.BlockSpec((tm, ka), lambda i, hh: (i, hh)),
            pl.BlockSpec((None, ka, nb), lambda i, hh: (hh, 0, 0)),
        ],
        out_specs=pl.BlockSpec((tm, nb), lambda i, hh: (i, hh)),
        out_shape=jax.ShapeDtypeStruct((r, h * nb), out_dtype),
        compiler_params=_params(("parallel", "parallel"), need),
        name=name,
    )(a, w)


def _oproj_kernel(x_ref, g_ref, a1_ref, a2_ref, w1_ref, w2_ref, lng_ref, lnb_ref, o_ref, *, nk):
    kk = pl.program_id(1)

    @pl.when(kk == 0)
    def _():
        o_ref[...] = jnp.zeros_like(o_ref)

    a1 = a1_ref[...].astype(BF16)
    a2 = a2_ref[...].astype(BF16)
    for n0 in range(0, o_ref.shape[1], N_CHUNK):
        cols = slice(n0, n0 + N_CHUNK)
        o_ref[:, cols] += _dot(a1, w1_ref[:, cols]) + _dot(a2, w2_ref[:, cols])

    @pl.when(kk == nk - 1)
    def _():
        _residual_layernorm(o_ref, x_ref, g_ref, lng_ref, lnb_ref, 1.0)


def _oproj(x, g, attn, rwkv, w_o, ln_g, ln_b, *, tm, tk, tiles_per_group):
    r, d = x.shape
    half = attn.shape[1]
    nk = half // tk
    mr = g.shape[1]
    need = 4 * tm * d * 4 + 2 * mr * d * 4 + 4 * tm * tk * 4 + 4 * tk * d * 2
    return pl.pallas_call(
        functools.partial(_oproj_kernel, nk=nk),
        grid=(r // tm, nk),
        in_specs=[
            pl.BlockSpec((tm, d), lambda i, kk: (i, 0)),
            pl.BlockSpec((None, mr, d), lambda i, kk: (i // tiles_per_group, 0, 0)),
            pl.BlockSpec((tm, tk), lambda i, kk: (i, kk)),
            pl.BlockSpec((tm, tk), lambda i, kk: (i, kk)),
            pl.BlockSpec((tk, d), lambda i, kk: (kk, 0)),
            pl.BlockSpec((tk, d), lambda i, kk: (nk + kk, 0)),
            pl.BlockSpec((1, d), lambda i, kk: (0, 0)),
            pl.BlockSpec((1, d), lambda i, kk: (0, 0)),
        ],
        out_specs=pl.BlockSpec((tm, d), lambda i, kk: (i, 0)),
        out_shape=jax.ShapeDtypeStruct((r, d), F32),
        compiler_params=_params(("parallel", "arbitrary"), need),
        name="oproj",
    )(x, g, attn, rwkv, w_o, w_o, ln_g.reshape(1, d), ln_b.reshape(1, d))


def _lat_kernel(m_ref, g_ref, cos_ref, sin_ref, ckv_ref, kr_ref, kr2_ref):
    m = m_ref[...]
    kv = m[:, :KV_LORA_RANK]
    ms = jnp.mean(kv * kv, axis=-1, keepdims=True)
    ckv_ref[...] = kv * lax.rsqrt(ms + RMS_EPS) * g_ref[...]
    a2 = m[:, KV_LORA_RANK:KV_LORA_RANK + V7X_LANES]
    b2 = m[:, KV_LORA_RANK + V7X_LANES:KV_LORA_RANK + 2 * V7X_LANES]
    kr2 = a2 * cos_ref[...] + b2 * sin_ref[...]
    kr2_ref[...] = kr2
    kr_ref[...] = kr2[:, :QK_ROPE_DIM]


def _lat_prep(mla, g_kv, cos4, sin4, *, tm):
    r = mla.shape[0]
    nb = cos4.shape[0] // tm
    wblk = MLA_COLS - MLA_KV_OFF
    need = 2 * tm * wblk * 4 + 2 * tm * (KV_LORA_RANK + 3 * V7X_LANES + 2 * V7X_LANES) * 4
    return pl.pallas_call(
        _lat_kernel,
        grid=(r // tm,),
        in_specs=[
            pl.BlockSpec((tm, wblk), lambda i: (i, MLA_KV_OFF // wblk)),
            pl.BlockSpec((1, KV_LORA_RANK), lambda i: (0, 0)),
            pl.BlockSpec((tm, V7X_LANES), lambda i: (i % nb, 0)),
            pl.BlockSpec((tm, V7X_LANES), lambda i: (i % nb, 0)),
        ],
        out_specs=[
            pl.BlockSpec((tm, KV_LORA_RANK), lambda i: (i, 0)),
            pl.BlockSpec((tm, QK_ROPE_DIM), lambda i: (i, 0)),
            pl.BlockSpec((tm, V7X_LANES), lambda i: (i, 0)),
        ],
        out_shape=[
            jax.ShapeDtypeStruct((r, KV_LORA_RANK), F32),
            jax.ShapeDtypeStruct((r, QK_ROPE_DIM), F32),
            jax.ShapeDtypeStruct((r, V7X_LANES), F32),
        ],
        compiler_params=_params(("parallel",), need),
        name="lat_prep",
    )(mla, g_kv.reshape(1, KV_LORA_RANK), cos4, sin4)


def _ropeq_kernel(a_ref, b_ref, cos_ref, sin_ref, o_ref):
    reps = a_ref.shape[1] // V7X_LANES
    c = jnp.concatenate([cos_ref[...]] * reps, axis=1)
    s = jnp.concatenate([sin_ref[...]] * reps, axis=1)
    o_ref[...] = (a_ref[...] * c + b_ref[...] * s).astype(o_ref.dtype)


def _rope_q(qall, cos4, sin4, *, tm):
    r = qall.shape[0]
    w = MLA_HEADS * QK_ROPE_DIM
    nb = cos4.shape[0] // tm
    need = 4 * tm * w * 4 + 2 * tm * w * 2 + 4 * tm * w * 4
    return pl.pallas_call(
        _ropeq_kernel,
        grid=(r // tm,),
        in_specs=[
            pl.BlockSpec((tm, w), lambda i: (i, 2)),
            pl.BlockSpec((tm, w), lambda i: (i, 3)),
            pl.BlockSpec((tm, V7X_LANES), lambda i: (i % nb, 0)),
            pl.BlockSpec((tm, V7X_LANES), lambda i: (i % nb, 0)),
        ],
        out_specs=pl.BlockSpec((tm, w), lambda i: (i, 0)),
        out_shape=jax.ShapeDtypeStruct((r, w), F32),
        compiler_params=_params(("parallel",), need),
        name="rope_q",
    )(qall, qall, cos4, sin4)


def _flash_kernel(qt_ref, kt_ref, qn_ref, qr_ref, kn_ref, kr_ref, v_ref, o_ref, m_scr, l_scr, acc_scr, *, tq, tk):
    qi = qt_ref[pl.program_id(2)]
    ki = kt_ref[pl.program_id(2)]

    @pl.when(ki == 0)
    def _():
        m_scr[...] = jnp.full_like(m_scr, -jnp.inf)
        l_scr[...] = jnp.zeros_like(l_scr)
        acc_scr[...] = jnp.zeros_like(acc_scr)

    def step(on_diagonal):
        qn = (qn_ref[...] * SM_SCALE).astype(BF16)
        qr = qr_ref[...] * SM_SCALE
        kn = kn_ref[...].astype(BF16)
        kr2 = kr_ref[...].astype(BF16)
        v = v_ref[...].astype(BF16)
        lane = lax.broadcasted_iota(jnp.int32, (1, V7X_LANES), 1)
        if on_diagonal:
            causal = lax.broadcasted_iota(jnp.int32, (tq, 1), 0) >= lax.broadcasted_iota(jnp.int32, (1, tk), 1)
        heads = range(2)
        hs = [slice(e * QK_NOPE_DIM, (e + 1) * QK_NOPE_DIM) for e in heads]
        qre = [jnp.where((lane >= QK_ROPE_DIM) if e else (lane < QK_ROPE_DIM), qr, 0.0).astype(BF16) for e in heads]
        s = [_dot_nt(qn[:, hs[e]], kn[:, hs[e]]) + _dot_nt(qre[e], kr2) for e in heads]
        if on_diagonal:
            s = [jnp.where(causal, s[e], -jnp.inf) for e in heads]
        m_prev = [m_scr[e] for e in heads]
        m_new = [jnp.maximum(m_prev[e], jnp.max(s[e], axis=-1, keepdims=True)) for e in heads]
        corr = [jnp.exp(m_prev[e] - m_new[e]) for e in heads]
        p = [jnp.exp(s[e] - m_new[e]) for e in heads]
        l_new = [l_scr[e] * corr[e] + jnp.sum(p[e], axis=-1, keepdims=True) for e in heads]
        acc_new = [acc_scr[e] * corr[e] + _dot(p[e].astype(BF16), v[:, hs[e]]) for e in heads]
        for e in heads:
            if on_diagonal:
                o_ref[:, e * V_HEAD_DIM:(e + 1) * V_HEAD_DIM] = acc_new[e] / l_new[e]
            else:
                l_scr[e] = l_new[e]
                acc_scr[e] = acc_new[e]
                m_scr[e] = m_new[e]

    @pl.when(ki < qi)
    def _():
        step(False)

    @pl.when(ki == qi)
    def _():
        step(True)


def _flash(qall, q_rope, kvup, kr2, *, batch, seq, tq):
    tk = tq
    nq = seq // tq
    r = qall.shape[0]
    hp = MLA_HEADS // 2
    need = 2 * (tq * 256 * 4 + tq * 128 * 2 + 2 * tk * 256 * 4 + tk * 128 * 4 + tq * 256 * 4) + 4 * tq * 128 * 4 + 8 * tq * tk * 4
    pairs = [(qi, ki) for qi in range(nq) for ki in range(qi + 1)]
    q_tab = jnp.asarray([qk[0] for qk in pairs], jnp.int32)
    k_tab = jnp.asarray([qk[1] for qk in pairs], jnp.int32)
    grid_spec = pltpu.PrefetchScalarGridSpec(
        num_scalar_prefetch=2,
        grid=(batch, hp, len(pairs)),
        in_specs=[
            pl.BlockSpec((tq, 2 * QK_NOPE_DIM), lambda b, h, p, qt, kt: (b * nq + qt[p], h)),
            pl.BlockSpec((tq, V7X_LANES), lambda b, h, p, qt, kt: (b * nq + qt[p], h)),
            pl.BlockSpec((tk, 2 * QK_NOPE_DIM), lambda b, h, p, qt, kt: (b * nq + kt[p], h)),
            pl.BlockSpec((tk, V7X_LANES), lambda b, h, p, qt, kt: (b * nq + kt[p], 0)),
            pl.BlockSpec((tk, 2 * V_HEAD_DIM), lambda b, h, p, qt, kt: (b * nq + kt[p], hp + h)),
        ],
        out_specs=pl.BlockSpec((tq, 2 * V_HEAD_DIM), lambda b, h, p, qt, kt: (b * nq + qt[p], h)),
        scratch_shapes=[
            pltpu.VMEM((2, tq, 1), F32),
            pltpu.VMEM((2, tq, 1), F32),
            pltpu.VMEM((2, tq, V_HEAD_DIM), F32),
        ],
    )
    return pl.pallas_call(
        functools.partial(_flash_kernel, tq=tq, tk=tk),
        grid_spec=grid_spec,
        out_shape=jax.ShapeDtypeStruct((r, MLA_HEADS * V_HEAD_DIM), F32),
        compiler_params=_params(("parallel", "parallel", "arbitrary"), need),
        name="flash",
    )(q_tab, k_tab, qall, q_rope, kvup, kr2, kvup)


def _paged_kernel(pt_ref, ql_ref, qr_ref, cn_ref, krn_ref, lat_hbm, krt_hbm, o_ref,
                  lat_buf, krt_buf, sem, m_scr, l_scr, acc_scr, *, npg, nchunk, new_len, heads):
    n_buf = lat_buf.shape[0]
    ahead = n_buf - 1
    c = pl.program_id(1)
    step = pl.program_id(0) * nchunk + c
    n_steps = pl.num_programs(0) * nchunk
    slot = step % n_buf

    def page_copies(step_, slot_):
        cps = []
        for i in range(npg):
            page = pt_ref[step_ * npg + i]
            cps.append(pltpu.make_async_copy(lat_hbm.at[0, page], lat_buf.at[slot_, i], sem.at[0, slot_]))
            cps.append(pltpu.make_async_copy(krt_hbm.at[0, page], krt_buf.at[slot_, i], sem.at[1, slot_]))
        return cps

    @pl.when(step == 0)
    def _():
        for s in range(ahead):
            for cp in page_copies(s, s):
                cp.start()

    @pl.when(step + ahead < n_steps)
    def _():
        for cp in page_copies(step + ahead, (step + ahead) % n_buf):
            cp.start()

    @pl.when(c == 0)
    def _():
        m_scr[...] = jnp.full_like(m_scr, -jnp.inf)
        l_scr[...] = jnp.zeros_like(l_scr)
        acc_scr[...] = jnp.zeros_like(acc_scr)

    ql = ql_ref[...]
    qr = qr_ref[...].astype(BF16)

    def update(state, s, vals):
        m_prev, l_prev, acc_prev = state
        m_new = jnp.maximum(m_prev, jnp.max(s, axis=-1, keepdims=True))
        corr = jnp.exp(m_prev - m_new)
        p = jnp.exp(s - m_new)
        w = s.shape[1] // len(vals)
        pv = _dot(p[:, :w].astype(BF16), vals[0])
        for i in range(1, len(vals)):
            pv += _dot(p[:, i * w:(i + 1) * w].astype(BF16), vals[i])
        return m_new, l_prev * corr + jnp.sum(p, axis=-1, keepdims=True), acc_prev * corr + pv

    for cp in page_copies(step, slot):
        cp.wait()
    kls = [lat_buf[slot, i].astype(BF16) for i in range(npg)]

    def scores(lo, hi):
        return jnp.concatenate(
            [_dot_nt(ql, kls[i]) + _dot(qr, krt_buf[slot, i].astype(BF16)) for i in range(lo, hi)], axis=1) * SM_SCALE

    half = npg // 2
    s_lo, s_hi = scores(0, half), scores(half, npg)
    state = update((m_scr[...], l_scr[...], acc_scr[...]), s_lo, kls[:half])
    state = update(state, s_hi, kls[half:])
    m_scr[...], l_scr[...], acc_scr[...] = state

    @pl.when(c == nchunk - 1)
    def _():
        kn = cn_ref[...].astype(BF16)
        s2 = (_dot_nt(ql, kn) + _dot_nt(qr, krn_ref[...].astype(BF16))) * SM_SCALE
        rows = s2.shape[0]
        t_row = lax.broadcasted_iota(jnp.int32, (rows, 1), 0) // heads
        t_col = lax.broadcasted_iota(jnp.int32, (1, s2.shape[1]), 1)
        ok = (t_col <= t_row) & (t_col < new_len)
        _, l_fin, acc_fin = update(state, jnp.where(ok, s2, -jnp.inf), [kn])
        o_ref[...] = acc_fin / l_fin


def _paged(page_table, q_lat, q_rope, ckv_new, kr_new, cache_lat, cache_kr_t, *, npg, new_len):
    db, rows, lat = q_lat.shape
    n_pages = page_table.shape[1]
    nchunk = n_pages // npg
    npad = ckv_new.shape[1]
    assert n_pages == nchunk * npg
    pt_flat = page_table.reshape(-1)
    in_specs = [
        pl.BlockSpec((None, rows, lat), lambda b, c, pt: (b, 0, 0)),
        pl.BlockSpec((None, rows, QK_ROPE_DIM), lambda b, c, pt: (b, 0, 0)),
        pl.BlockSpec((None, npad, lat), lambda b, c, pt: (b, 0, 0)),
        pl.BlockSpec((None, npad, QK_ROPE_DIM), lambda b, c, pt: (b, 0, 0)),
        pl.BlockSpec(memory_space=pl.ANY),
        pl.BlockSpec(memory_space=pl.ANY),
    ]
    n_buf = PAGE_BUFFERS
    assert db * nchunk >= n_buf
    need = (n_buf * npg * PAGE_SIZE * (lat + QK_ROPE_DIM) * 4 + npg * PAGE_SIZE * lat * 2
            + 6 * rows * npg * PAGE_SIZE * 4 + 8 * rows * lat * 4)
    grid_spec = pltpu.PrefetchScalarGridSpec(
        num_scalar_prefetch=1,
        grid=(db, nchunk),
        in_specs=in_specs,
        out_specs=pl.BlockSpec((None, rows, lat), lambda b, c, pt: (b, 0, 0)),
        scratch_shapes=[
            pltpu.VMEM((n_buf, npg, PAGE_SIZE, lat), F32),
            pltpu.VMEM((n_buf, npg, QK_ROPE_DIM, PAGE_SIZE), F32),
            pltpu.SemaphoreType.DMA((2, n_buf)),
            pltpu.VMEM((rows, 1), F32),
            pltpu.VMEM((rows, 1), F32),
            pltpu.VMEM((rows, lat), F32),
        ],
    )
    return pl.pallas_call(
        functools.partial(_paged_kernel, npg=npg, nchunk=nchunk, new_len=new_len, heads=MLA_HEADS),
        grid_spec=grid_spec,
        out_shape=jax.ShapeDtypeStruct((db, rows, lat), F32),
        compiler_params=_params(("arbitrary", "arbitrary"), need),
        name="paged",
    )(pt_flat, q_lat, q_rope, ckv_new, kr_new, cache_lat, cache_kr_t)


def _prep_kernel(u_ref, tail_ref, s0_ref, mu_ref, w0_ref, a0_ref, kk_ref, ka_ref, wd_ref, wi_ref, wg_ref,
                 r_ref, k_ref, v_ref, kkr_ref, a_ref, lw_ref, g_ref, *, seq_len):
    tm = u_ref.shape[0]
    u = u_ref[...]
    row = lax.broadcasted_iota(jnp.int32, (tm, 1), 0)
    prev = jnp.where(row == 0, tail_ref[SUBLANES - 1:SUBLANES, :], pltpu.roll(u, 1, 0))
    starts = ((pl.program_id(0) * tm + row) & (seq_len - 1)) == 0
    prev = jnp.where(starts, s0_ref[...], prev)
    um = u + (prev - u) * mu_ref[...]
    n = RWKV_DIM
    k = um[:, n:2 * n]
    dw = um[:, 3 * n:3 * n + DECAY_LORA]
    da = um[:, 3 * n + DECAY_LORA:3 * n + DECAY_LORA + ICLR_LORA]
    dg = um[:, 3 * n + DECAY_LORA + ICLR_LORA:]
    z = -(w0_ref[...] + _dot(jnp.tanh(dw).astype(BF16), wd_ref[...]))
    softplus = jnp.maximum(z, 0.0) + jnp.log1p(jnp.exp(-jnp.abs(z)))
    lw_ref[...] = -jnp.exp(-softplus - 0.5)
    a = jax.nn.sigmoid(a0_ref[...] + _dot(da.astype(BF16), wi_ref[...]))
    a_ref[...] = a
    g_ref[...] = _dot(jax.nn.sigmoid(dg).astype(BF16), wg_ref[...])
    r_ref[...] = um[:, :n]
    v_ref[...] = um[:, 2 * n:3 * n]
    kkr_ref[...] = k * kk_ref[...]
    k_ref[...] = k * (1.0 + (a - 1.0) * ka_ref[...])


def _rwkv_prep(u, shift0, mu, w0, a0, k_k, k_a, w_decay, w_iclr, w_gate, *, tm, seq_len):
    r, wu = u.shape
    n = RWKV_DIM
    assert seq_len & (seq_len - 1) == 0
    tiles_per_group = max(seq_len // tm, 1)
    mr = shift0.shape[1]
    row = lambda width: pl.BlockSpec((1, width), lambda i: (0, 0))
    full = lambda arr: pl.BlockSpec(arr.shape, lambda i: (0, 0))
    out_spec = pl.BlockSpec((tm, n), lambda i: (i, 0))
    need = 2 * tm * wu * 4 + 2 * mr * wu * 4 + 2 * 7 * tm * n * 4 + 6 * tm * wu * 4
    return pl.pallas_call(
        functools.partial(_prep_kernel, seq_len=seq_len),
        grid=(r // tm,),
        in_specs=[
            pl.BlockSpec((tm, wu), lambda i: (i, 0)),
            pl.BlockSpec((SUBLANES, wu), lambda i: (jnp.maximum(i * (tm // SUBLANES) - 1, 0), 0)),
            pl.BlockSpec((None, mr, wu), lambda i: (i // tiles_per_group, 0, 0)),
            row(wu), row(n), row(n), row(n), row(n),
            full(w_decay), full(w_iclr), full(w_gate),
        ],
        out_specs=[out_spec] * 7,
        out_shape=[jax.ShapeDtypeStruct((r, n), F32)] * 7,
        compiler_params=_params(("parallel",), need),
        name="rwkv_prep",
    )(u, u, shift0, mu.reshape(1, wu), w0.reshape(1, n), a0.reshape(1, n), k_k.reshape(1, n), k_a.reshape(1, n),
      w_decay, w_iclr, w_gate)


def _wkv_kernel(*refs, c_len, hg, gps, n_chunks, has_s0):
    if has_s0:
        (r_ref, k_ref, v_ref, kkr_ref, a_ref, lw_ref, g_ref, rk_ref, lng_ref, lnb_ref, s0_ref,
         o_ref, sout_ref, st_scr) = refs
    else:
        (r_ref, k_ref, v_ref, kkr_ref, a_ref, lw_ref, g_ref, rk_ref, lng_ref, lnb_ref,
         o_ref, sout_ref, st_scr) = refs
    n = RWKV_HEAD_DIM
    w = hg * n
    rw = hg * c_len
    shift = int(math.log2(n))
    ci = pl.program_id(2)

    lane_head = lax.broadcasted_iota(jnp.int32, (1, w), 1) >> shift
    key_head = lax.broadcasted_iota(jnp.int32, (w, 1), 0) >> shift
    same_head = key_head == lane_head
    row_head = lax.broadcasted_iota(jnp.int32, (rw, 1), 0) // c_len
    stack_mask = row_head == lane_head

    @pl.when(ci == 0)
    def _():
        for gi in range(gps):
            if has_s0:
                s0t = jnp.transpose(s0_ref[gi * hg:(gi + 1) * hg].reshape(w, n))
                st_scr[gi] = jnp.where(same_head, jnp.concatenate([s0t] * hg, axis=0), 0.0)
            else:
                st_scr[gi] = jnp.zeros((w, w), F32)

    new_states = _wkv_groups(gps, r_ref, k_ref, v_ref, kkr_ref, a_ref, lw_ref, g_ref, rk_ref, lng_ref, lnb_ref,
                             o_ref, st_scr, c_len=c_len, hg=hg, same_head=same_head, stack_mask=stack_mask)

    @pl.when(ci == n_chunks - 1)
    def _():
        for gi, st_new in enumerate(new_states):
            z = st_new[0:n]
            for e in range(1, hg):
                z = z + st_new[e * n:(e + 1) * n]
            sout_ref[gi * hg:(gi + 1) * hg] = jnp.transpose(z).reshape(hg, n, n)


def _wkv_groups(gps, r_ref, k_ref, v_ref, kkr_ref, a_ref, lw_ref, g_ref, rk_ref, lng_ref, lnb_ref, o_ref, st_scr, *,
                c_len, hg, same_head, stack_mask):
    n = RWKV_HEAD_DIM
    w = hg * n
    rw = hg * c_len
    groups = range(gps)
    lanes = [slice(gi * w, (gi + 1) * w) for gi in groups]
    each = lambda fn, *cols: [fn(*xs) for xs in zip(*cols)]

    seg_ones = jnp.where(same_head, 1.0, 0.0).astype(BF16)

    def seg_sum(x):
        hi, lo = _split_bf16(x, 2)
        return _dot(hi, seg_ones) + _dot(lo, seg_ones)

    def stack(x):
        return jnp.where(stack_mask, jnp.concatenate([x] * hg, axis=0), 0.0).astype(BF16)

    t_r = lax.broadcasted_iota(jnp.int32, (c_len, 1), 0)
    t_c = lax.broadcasted_iota(jnp.int32, (1, c_len), 1)
    tril = jnp.where(t_c <= t_r, 1.0, 0.0).astype(BF16)
    ri = lax.broadcasted_iota(jnp.int32, (rw, 1), 0)
    cj = lax.broadcasted_iota(jnp.int32, (1, rw), 1)
    strict = cj < ri
    incl = cj <= ri

    r = [r_ref[:, ls] for ls in lanes]
    k = [k_ref[:, ls] for ls in lanes]
    v = [v_ref[:, ls] for ls in lanes]
    a = [a_ref[:, ls] for ls in lanes]
    lw = [lw_ref[:, ls] for ls in lanes]
    kkr = [kkr_ref[:, ls] for ls in lanes]

    def cumsum(x):
        l1, l2, l3 = _split_bf16(x, 3)
        return _dot(tril, l1) + _dot(tril, l2) + _dot(tril, l3)

    cum = each(cumsum, lw)
    cum_end = [c[c_len - 1:c_len, :] for c in cum]
    kkn = each(lambda x: x / jnp.maximum(jnp.sqrt(seg_sum(x * x)), 1e-12), kkr)
    kka = each(lambda x, y: x * y, kkn, a)
    p_inv = each(lambda c: jnp.exp(-c), cum)
    p_tail = each(lambda ce, c: jnp.exp(ce - c), cum_end, cum)

    a_s = each(lambda c, l, x: stack(jnp.exp(c - l) * (-x)), cum, lw, kkn)
    r_s = each(lambda c, x: stack(jnp.exp(c) * x), cum, r)
    b_s = each(lambda p, x: stack(p * x), p_inv, kka)
    k_s = each(lambda p, x: stack(p * x), p_inv, k)
    v_s = each(stack, v)
    be_s = each(lambda p, x: stack(p * x), p_tail, kka)
    ke_s = each(lambda p, x: stack(p * x), p_tail, k)

    l_ab = each(lambda x, y: jnp.where(strict, _dot_nt(x, y), 0.0).astype(BF16), a_s, b_s)
    l_ak = each(lambda x, y: jnp.where(strict, _dot_nt(x, y), 0.0).astype(BF16), a_s, k_s)
    m_rb = each(lambda x, y: jnp.where(incl, _dot_nt(x, y), 0.0).astype(BF16), r_s, b_s)
    m_rk = each(lambda x, y: jnp.where(incl, _dot_nt(x, y), 0.0).astype(BF16), r_s, k_s)

    st = [st_scr[gi] for gi in groups]
    st_b = [s.astype(BF16) for s in st]
    x = each(lambda p, s, l, q: _dot(p, s) + _dot(l, q), a_s, st_b, l_ak, v_s)
    n_it = int(math.log2(c_len))
    li = l_ab
    for it in range(n_it):
        x = each(lambda xx, l: xx + _dot(l, xx.astype(BF16)), x, li)
        if it < n_it - 1:
            li = each(lambda l: _dot(l, l).astype(BF16), li)
    u_s = [xx.astype(BF16) for xx in x]

    def fold_heads(y_s):
        y = y_s[0:c_len]
        for e in range(1, hg):
            y = y + y_s[e * c_len:(e + 1) * c_len]
        return y

    y = each(lambda p, s, m1, u, m2, q: fold_heads(_dot(p, s) + _dot(m1, u) + _dot(m2, q)),
             r_s, st_b, m_rb, u_s, m_rk, v_s)

    def new_state(ce, s, b, u, kk_, q):
        pc_col = jnp.transpose(jnp.broadcast_to(jnp.exp(ce), (V7X_LANES, w)))[:, 0:1]
        return pc_col * s + _dot_tn(b, u) + _dot_tn(kk_, q)

    st_new = each(new_state, cum_end, st, be_s, u_s, ke_s, v_s)
    for gi in groups:
        st_scr[gi] = st_new[gi]

    inv_n = 1.0 / n
    mean = each(lambda yy: seg_sum(yy) * inv_n, y)
    d = each(lambda yy, m: yy - m, y, mean)
    var = each(lambda dd: seg_sum(dd * dd) * inv_n, d)
    bonus = each(lambda rr, kk_, ls, vv: seg_sum(rr * kk_ * rk_ref[:, ls]) * vv, r, k, lanes, v)
    for gi in groups:
        ls = lanes[gi]
        yn = d[gi] * lax.rsqrt(var[gi] + GN_EPS) * lng_ref[:, ls] + lnb_ref[:, ls]
        o_ref[:, ls] = (yn + bonus[gi]) * g_ref[:, ls]
    return st_new


def _wkv(r, k, v, kkr, a, lw, g, r_k, lnx_g, lnx_b, s0, *, batch, c_len, n_chunks, gps):
    rows, dim = r.shape
    hg = WKV_HG
    n = RWKV_HEAD_DIM
    w = hg * n
    wb = gps * w
    blk = pl.BlockSpec((c_len, wb), lambda b, gi, ci: (b * n_chunks + ci, gi))
    vec = pl.BlockSpec((1, wb), lambda b, gi, ci: (0, gi))
    st_spec = pl.BlockSpec((None, gps * hg, n, n), lambda b, gi, ci: (b, gi, 0, 0))
    in_specs = [blk] * 7 + [vec] * 3
    args = [r, k, v, kkr, a, lw, g, r_k.reshape(1, dim), lnx_g.reshape(1, dim), lnx_b.reshape(1, dim)]
    if s0 is not None:
        in_specs.append(st_spec)
        args.append(s0)
    rw = hg * c_len
    need = 2 * 8 * c_len * wb * 4 + gps * (5 * w * w * 4 + 16 * rw * w * 4 + 8 * rw * rw * 4)
    return pl.pallas_call(
        functools.partial(_wkv_kernel, c_len=c_len, hg=hg, gps=gps, n_chunks=n_chunks, has_s0=s0 is not None),
        grid=(batch, dim // wb, n_chunks),
        in_specs=in_specs,
        out_specs=[blk, st_spec],
        out_shape=[
            jax.ShapeDtypeStruct((rows, dim), F32),
            jax.ShapeDtypeStruct((batch, dim // n, n, n), F32),
        ],
        scratch_shapes=[pltpu.VMEM((gps, w, w), F32)],
        compiler_params=_params(("parallel", "parallel", "arbitrary"), need),
        name="wkv",
    )(*args)


def _wkv_lanes_kernel(r_ref, k_ref, v_ref, kkr_ref, a_ref, lw_ref, g_ref, rk_ref, lng_ref, lnb_ref, s0_ref,
                      o_ref, sout_ref, y_scr, *, steps):
    n = RWKV_HEAD_DIM
    decay, kkn, kka, kt, rt = [], [], [], [], []
    for t in range(steps):
        kkr = kkr_ref[t]
        norm = jnp.sqrt(jnp.sum(kkr * kkr, axis=0, keepdims=True))
        kk = kkr / jnp.maximum(norm, 1e-12)
        decay.append(jnp.exp(lw_ref[t]))
        kkn.append(kk)
        kka.append(kk * a_ref[t])
        kt.append(k_ref[t])
        rt.append(r_ref[t])

    def row(vi, carry):
        s = s0_ref[vi]
        for t in range(steps):
            sa = jnp.sum(s * kkn[t], axis=0, keepdims=True)
            s = s * decay[t] - sa * kka[t] + v_ref[t, pl.ds(vi, 1), :] * kt[t]
            y_scr[t, pl.ds(vi, 1), :] = jnp.sum(s * rt[t], axis=0, keepdims=True)
        sout_ref[vi] = s
        return carry

    lax.fori_loop(0, n, row, 0)

    for t in range(steps):
        y = y_scr[t]
        mean = jnp.mean(y, axis=0, keepdims=True)
        d = y - mean
        var = jnp.mean(d * d, axis=0, keepdims=True)
        yn = d * lax.rsqrt(var + GN_EPS) * lng_ref[...] + lnb_ref[...]
        bonus = jnp.sum(rt[t] * kt[t] * rk_ref[...], axis=0, keepdims=True) * v_ref[t]
        o_ref[t] = (yn + bonus) * g_ref[t]


def _wkv_lanes(r, k, v, kkr, a, lw, g, r_k, lnx_g, lnx_b, s0):
    steps, dim, nb = r.shape
    n = RWKV_HEAD_DIM
    heads = dim // n
    blk = pl.BlockSpec((steps, n, nb), lambda h: (0, h, 0))
    par = pl.BlockSpec((n, nb), lambda h: (h, 0))
    st = pl.BlockSpec((None, n, n, nb), lambda h: (h, 0, 0, 0))
    need = 2 * 8 * steps * n * nb * 4 + 4 * n * n * nb * 4 + 32 * n * nb * 4
    return pl.pallas_call(
        functools.partial(_wkv_lanes_kernel, steps=steps),
        grid=(heads,),
        in_specs=[blk] * 7 + [par] * 3 + [st],
        out_specs=[blk, st],
        out_shape=[jax.ShapeDtypeStruct((steps, dim, nb), F32), jax.ShapeDtypeStruct((heads, n, n, nb), F32)],
        scratch_shapes=[pltpu.VMEM((steps, n, nb), F32)],
        compiler_params=_params(("parallel",), need),
        name="wkv_lanes",
    )(r, k, v, kkr, a, lw, g, r_k, lnx_g, lnx_b, s0)


def _rope_tables(pos):
    inv = ROPE_THETA ** (-jnp.arange(0, QK_ROPE_DIM, 2, dtype=F32) / QK_ROPE_DIM)
    ang = pos[:, None] * inv[None, :]
    cos, sin = jnp.cos(ang), jnp.sin(ang)
    return jnp.concatenate([cos] * 4, axis=1), jnp.concatenate([-sin, sin] * 2, axis=1)


def _swap_halves(w):
    half = w.shape[-1] // 2
    return jnp.concatenate([w[..., half:], w[..., :half]], axis=-1)


def _prepare_weights(w_in, w_uq, w_ukv, w_o, w_decay, w_iclr, w_gate):
    d = w_in.shape[0]
    rope = w_in[:, Q_LORA_RANK + KV_LORA_RANK:W_MLA_IN]
    rope_sw = _swap_halves(rope)
    zeros = lambda c: jnp.zeros((d, c), w_in.dtype)
    w_mla = jnp.concatenate(
        [w_in[:, :Q_LORA_RANK], zeros(MLA_KV_OFF - Q_LORA_RANK), w_in[:, Q_LORA_RANK:Q_LORA_RANK + KV_LORA_RANK],
         rope, rope, rope_sw, rope_sw, zeros(MLA_COLS - MLA_KV_OFF - KV_LORA_RANK - 4 * QK_ROPE_DIM)], axis=1).astype(BF16)
    w_rwkv = w_in[:, W_MLA_IN:].astype(BF16)
    uq = w_uq.reshape(Q_LORA_RANK, MLA_HEADS, QK_NOPE_DIM + QK_ROPE_DIM)
    uq_rope = uq[..., QK_NOPE_DIM:]
    w_q = jnp.concatenate(
        [uq[..., :QK_NOPE_DIM].reshape(Q_LORA_RANK, -1), uq_rope.reshape(Q_LORA_RANK, -1),
         _swap_halves(uq_rope).reshape(Q_LORA_RANK, -1)], axis=1).astype(BF16)
    ukv = w_ukv.reshape(KV_LORA_RANK, MLA_HEADS, QK_NOPE_DIM + V_HEAD_DIM)
    w_uk, w_uv = ukv[..., :QK_NOPE_DIM], ukv[..., QK_NOPE_DIM:]
    w_kvup = jnp.concatenate([w_uk.reshape(KV_LORA_RANK, -1), w_uv.reshape(KV_LORA_RANK, -1)], axis=1).astype(BF16)
    w_uk_t = jnp.transpose(w_uk, (1, 2, 0)).astype(BF16)
    w_uv_h = jnp.transpose(w_uv, (1, 0, 2)).astype(BF16)
    return dict(w_mla=_tile_major(w_mla, MM_TN), w_rwkv=_tile_major(w_rwkv, MM_TN), w_q=_tile_major(w_q, MM_TN),
                w_kvup=_tile_major(w_kvup, MM_TN), w_uk_t=w_uk_t, w_uv_h=w_uv_h,
                w_o=w_o.astype(BF16), w_decay=w_decay.astype(BF16), w_iclr=w_iclr.astype(BF16),
                w_gate=w_gate.astype(BF16))


def _group_layer(x, mods, tiles_per_group, tm, wts, p, *, cos4, sin4, ffn, attend, wkv_run, shift0, seq_len):
    sh2, sc2, g2 = mods
    x1 = ffn(0, x)
    mla = _mm(x1, wts["w_mla"], tm=tm, mod=(sh2, sc2), tiles_per_group=tiles_per_group, name="proj_mla")
    u = _mm(x1, wts["w_rwkv"], tm=tm, mod=(sh2, sc2), tiles_per_group=tiles_per_group, name="proj_rwkv")
    ckv, kr, kr2 = _lat_prep(mla, p["g_kv"], cos4, sin4, tm=tm)
    qall = _mm(mla, wts["w_q"], tm=tm, k=Q_LORA_RANK, rms_g=p["g_q"], name="q_proj")
    q_rope = _rope_q(qall, cos4, sin4, tm=tm)
    attn = attend(qall, q_rope, ckv, kr, kr2)
    prep = _rwkv_prep(u, shift0, p["mu_shift"], p["w0"], p["a0"], p["k_k"], p["k_a"],
                      wts["w_decay"], wts["w_iclr"], wts["w_gate"], tm=PREP_TM, seq_len=seq_len)
    rwkv, s_new = wkv_run(prep)
    x2 = _oproj(x1, g2, attn, rwkv, wts["w_o"], p["ln_g"][1], p["ln_b"][1], tm=tm, tk=512,
                tiles_per_group=tiles_per_group)
    return ffn(1, x2), ckv, kr, s_new, u


def kernel(x_prompt, x_sample, c_prompt, c_sample, cache_kv_latent, cache_k_rope, state_wkv, state_shift, page_table, w_ada, b_ada, ln_g, ln_b, w_ffn1_in, w_ffn1_out, w_ffn2_in, w_ffn2_out, w_in, g_q, g_kv, w_uq, w_ukv, mu_shift, w0, w_decay, a0, w_iclr, w_gate, k_k, k_a, r_k, lnx_g, lnx_b, w_o):
    batch, seq, d = x_prompt.shape
    db, dseq, _ = x_sample.shape
    depth = w_ada.shape[0]
    assert depth == DEPTH == 1
    n_ada = w_ada.shape[2] // d
    past = page_table.shape[1] * PAGE_SIZE
    wu = state_shift.shape[2]

    c_prompt_rows = jnp.pad(c_prompt, ((0, (-batch) % SUBLANES), (0, 0)))
    cos_p, sin_p = _rope_tables(jnp.arange(seq, dtype=F32) + 0)
    pos_s = jnp.tile(jnp.arange(dseq, dtype=F32) + past, db)
    cos_s, sin_s = _rope_tables(pos_s)

    yp = x_prompt.reshape(batch * seq, d)
    ys = x_sample.reshape(db * dseq, d)
    outs_p, outs_s = [], []
    for l in range(depth):
        p = dict(ln_g=ln_g[l], ln_b=ln_b[l], g_q=g_q[l], g_kv=g_kv[l], mu_shift=mu_shift[l], w0=w0[l], a0=a0[l],
                 k_k=k_k[l], k_a=k_a[l], r_k=r_k[l], lnx_g=lnx_g[l], lnx_b=lnx_b[l])
        wts = _prepare_weights(w_in[l], w_uq[l], w_ukv[l], w_o[l], w_decay[l], w_iclr[l], w_gate[l])
        ffn_w32 = ((w_ffn1_in[l], w_ffn1_out[l]), (w_ffn2_in[l], w_ffn2_out[l]))
        ffn_w16 = [None, None]
        ada_p, ada_s = _ada(c_prompt_rows, c_sample, w_ada[l], b_ada[l])
        mods_p = [ada_p[i, :batch][:, None, :] for i in range(n_ada)]
        mods_s = [ada_s[i][None] for i in range(n_ada)]
        per_token = lambda m: jnp.repeat(m[0], dseq, axis=0)[None]

        tm_s = db * dseq
        c_pad = 16
        t_major = lambda a: a.reshape(db, dseq, d).transpose(1, 0, 2).reshape(tm_s, d)
        b_major = lambda a: a.reshape(dseq, db, d).transpose(1, 0, 2).reshape(tm_s, d)

        def ffn_s(i, x):
            sh, sc, g = mods_s[6 * i:6 * i + 3]
            y, w_in16, w_out16 = _ffn(t_major(x), sh, sc, g, *ffn_w32[i], p["ln_g"][2 * i], p["ln_b"][2 * i],
                                      tm=tm_s, tf=128, tiles_per_group=1, emit_bf16_weights=True)
            ffn_w16[i] = (w_in16, w_out16)
            return b_major(y)

        def attend_s(qall, q_rope, ckv, kr, kr2):
            rows = dseq * MLA_HEADS
            q_lat = _bmm(qall, wts["w_uk_t"], tm=tm_s, out_dtype=BF16, name="q_absorb")
            pad = lambda t: jnp.pad(t.reshape(db, dseq, -1), ((0, 0), (0, c_pad - dseq), (0, 0)))
            o_lat = _paged(page_table, q_lat.reshape(db, rows, KV_LORA_RANK), q_rope.reshape(db, rows, QK_ROPE_DIM),
                           pad(ckv), pad(kr), cache_kv_latent, jnp.swapaxes(cache_k_rope, 2, 3),
                           npg=PAGES_PER_STEP, new_len=dseq)
            return _bmm(o_lat.reshape(db * dseq, MLA_HEADS * KV_LORA_RANK), wts["w_uv_h"], tm=tm_s, name="v_up")

        def wkv_s(prep):
            lanes = lambda a: a.reshape(db, dseq, -1).transpose(1, 2, 0)
            over_batch = lambda v: jnp.broadcast_to(v.reshape(-1, 1), (v.size, db))
            out, s_new = _wkv_lanes(*[lanes(t) for t in prep], over_batch(p["r_k"]), over_batch(p["lnx_g"]),
                                    over_batch(p["lnx_b"]), jnp.transpose(state_wkv[l], (1, 2, 3, 0)))
            return out.transpose(2, 0, 1).reshape(db * dseq, -1), jnp.transpose(s_new, (3, 0, 1, 2))

        shift_rows = jnp.repeat(state_shift[l], dseq, axis=0).reshape(db * dseq // PREP_TM, PREP_TM, wu)
        ys, ckv_s, kr_s, s_s, u_s = _group_layer(
            ys, [per_token(m) for m in mods_s[3:6]], 1, tm_s, wts, p, cos4=cos_s, sin4=sin_s, ffn=ffn_s,
            attend=attend_s, wkv_run=wkv_s, shift0=shift_rows, seq_len=dseq)
        outs_s.append((ckv_s.reshape(db, dseq, -1), kr_s.reshape(db, dseq, -1), s_s,
                       u_s.reshape(db, dseq, wu)[:, -1]))

        tm_p = 512

        def ffn_p(i, x):
            sh, sc, g = mods_p[6 * i:6 * i + 3]
            return _ffn(x, sh, sc, g, *ffn_w16[i], p["ln_g"][2 * i], p["ln_b"][2 * i],
                        tm=tm_p, tf=FFN_TF, tiles_per_group=seq // tm_p)

        def attend_p(qall, q_rope, ckv, kr, kr2):
            kvup = _mm(ckv, wts["w_kvup"], tm=tm_p, name="kv_up")
            return _flash(qall, q_rope, kvup, kr2, batch=batch, seq=seq, tq=512)

        def wkv_p(prep):
            return _wkv(*prep, p["r_k"], p["lnx_g"], p["lnx_b"], None, batch=batch, c_len=64, n_chunks=seq // 64,
                        gps=8)

        yp, ckv_p, kr_p, s_p, u_p = _group_layer(
            yp, mods_p[3:6], seq // tm_p, tm_p, wts, p, cos4=cos_p, sin4=sin_p, ffn=ffn_p, attend=attend_p,
            wkv_run=wkv_p, shift0=jnp.zeros((batch, 1, wu), F32), seq_len=seq)
        outs_p.append((ckv_p.reshape(batch, seq, -1), kr_p.reshape(batch, seq, -1), s_p,
                       u_p.reshape(batch, seq, wu)[:, -1]))

    stack = lambda outs, i: jnp.stack([o[i] for o in outs])
    return (yp.reshape(batch, seq, d), ys.reshape(db, dseq, d),
            stack(outs_p, 0), stack(outs_p, 1), stack(outs_p, 2), stack(outs_p, 3),
            stack(outs_s, 0), stack(outs_s, 1), stack(outs_s, 2), stack(outs_s, 3))
```
